```python
import jax, jax.numpy as jnp
from jax import lax
import numpy as np

D_MODEL = 1024
BATCH = 32
SEQ = 2048
DEPTH = 4
DEC_BATCH = 32
DEC_SEQ = 32
PAST_LEN = 1024

CHUNK = 64
HEAD_DIM = 64
ROPE_THETA = 500000.0
NORM_EPS = 1e-6
NEG_INF = -1e30
Q_BLOCK = 128
A_HEADS = 8
A_Q_LORA = 256
A_KV_LORA = 128
A_NOPE = 64
A_ROPE = 32
A_QK = A_NOPE + A_ROPE
A_V = 64
A_SCALE = A_QK ** -0.5
B_HEADS = 8
B_BACK_CHUNKS = 8
B_REACH = B_BACK_CHUNKS * CHUNK
B_MAX_REL = 128
C_HEADS = 16
C_KV_HEADS = 2
C_WINDOW = 128
C_ROT = HEAD_DIM // 4
HEAD_SCALE = HEAD_DIM ** -0.5
D_FF = 2816
N_EVEN = (DEPTH + 1) // 2
N_ODD = DEPTH // 2
EVEN_IN = A_Q_LORA + A_KV_LORA + A_ROPE + 3 * B_HEADS * HEAD_DIM
MIX_EVEN = A_HEADS * A_V + B_HEADS * HEAD_DIM
ODD_IN = (C_HEADS + 2 * C_KV_HEADS) * HEAD_DIM
MIX_ODD = C_HEADS * HEAD_DIM

kernel_name = "hybrid_streaming_encoder_step"


def rms_norm(x, g):
    xf = x.astype(jnp.float32)
    y = xf * lax.rsqrt(jnp.mean(xf * xf, axis=-1, keepdims=True) + NORM_EPS)
    return (y * g.astype(jnp.float32)).astype(x.dtype)


def rope(x, pos, n_rot):
    half = n_rot // 2
    inv = ROPE_THETA ** (-jnp.arange(half, dtype=jnp.float32) / half)
    ang = pos.astype(jnp.float32)[:, None] * inv[None, :]
    cos = jnp.cos(ang)[:, None, :]
    sin = jnp.sin(ang)[:, None, :]
    xf = x.astype(jnp.float32)
    x1, x2 = xf[..., :half], xf[..., half:n_rot]
    out = jnp.concatenate([x1 * cos - x2 * sin, x2 * cos + x1 * sin, xf[..., n_rot:]], axis=-1)
    return out.astype(x.dtype)


def swiglu(x, w_gate, w_up, w_down):
    a = jnp.einsum('btd,df->btf', x, w_gate)
    b = jnp.einsum('btd,df->btf', x, w_up)
    return jnp.einsum('btf,fd->btd', jax.nn.silu(a) * b, w_down)


def attn_probs(s, mask, sink=None):
    s = jnp.where(mask, s, NEG_INF)
    if sink is not None:
        sk = jnp.broadcast_to(sink.astype(jnp.float32)[None, :, None, None], s.shape[:-1] + (1,))
        return jax.nn.softmax(jnp.concatenate([s, sk], axis=-1), axis=-1)[..., :-1]
    return jax.nn.softmax(s, axis=-1)


def mla_attend(qn, qp, k_nope, kpe, v, mask):
    b, q = qn.shape[:2]
    s = jnp.einsum('bqhd,bkhd->bhqk', qn, k_nope) + jnp.einsum('bqhd,bkd->bhqk', qp, kpe)
    p = attn_probs(s.astype(jnp.float32) * A_SCALE, mask)
    return jnp.einsum('bhqk,bkhd->bqhd', p.astype(v.dtype), v).reshape(b, q, A_HEADS * A_V)


def mla_prompt(qn, qp, k_nope, kpe, v):
    b, s_len = qn.shape[:2]
    key_chunk = jnp.arange(s_len) // CHUNK

    def one_block(i):
        start = i * Q_BLOCK
        qnb = lax.dynamic_slice_in_dim(qn, start, Q_BLOCK, axis=1)
        qpb = lax.dynamic_slice_in_dim(qp, start, Q_BLOCK, axis=1)
        q_chunk = (start + jnp.arange(Q_BLOCK)) // CHUNK
        mask = key_chunk[None, :] <= q_chunk[:, None]
        return mla_attend(qnb, qpb, k_nope, kpe, v, mask)

    out = lax.map(one_block, jnp.arange(s_len // Q_BLOCK))
    return jnp.moveaxis(out, 0, 1).reshape(b, s_len, -1)


def band_attend(q, k, v, q_pos, k_pos, rel_table):
    b, nq, h, d = q.shape
    rel = jnp.clip(q_pos[:, None] - k_pos[None, :], -B_MAX_REL, B_MAX_REL) + B_MAX_REL
    bias = rel_table.astype(jnp.float32)[:, rel]
    s = jnp.einsum('bqhd,bkhd->bhqk', q, k).astype(jnp.float32) * HEAD_SCALE + bias[None]
    p = attn_probs(s, (k_pos >= 0)[None, :])
    return jnp.einsum('bhqk,bkhd->bqhd', p.astype(v.dtype), v).reshape(b, nq, h * d)


def swa_attend(q, k, v, k_pos, sinks):
    b, nq, h, d = q.shape
    kvh = k.shape[2]
    g = h // kvh
    nk = k.shape[1]
    qg = q.reshape(b, nq, kvh, g, d)
    s = jnp.einsum('bqkgd,bskd->bkgqs', qg, k).astype(jnp.float32).reshape(b, h, nq, nk) * HEAD_SCALE
    p = attn_probs(s, (k_pos >= 0)[None, :], sinks).reshape(b, kvh, g, nq, nk)
    return jnp.einsum('bkgqs,bskd->bqkgd', p.astype(v.dtype), v).reshape(b, nq, h * d)


def sweep_chunks(attend, q, k, v, reach):
    b, s_len = q.shape[:2]
    pad = ((0, 0), (reach, 0), (0, 0), (0, 0))
    kp, vp = jnp.pad(k, pad), jnp.pad(v, pad)
    band = reach + CHUNK

    def one_chunk(c):
        start = c * CHUNK
        qc = lax.dynamic_slice_in_dim(q, start, CHUNK, axis=1)
        kc = lax.dynamic_slice_in_dim(kp, start, band, axis=1)
        vc = lax.dynamic_slice_in_dim(vp, start, band, axis=1)
        q_pos = start + jnp.arange(CHUNK)
        k_pos = start - reach + jnp.arange(band)
        return attend(qc, kc, vc, q_pos, k_pos)

    out = lax.map(one_chunk, jnp.arange(s_len // CHUNK))
    return jnp.moveaxis(out, 0, 1).reshape(b, s_len, -1)


def even_mixer(h, pos, i, w, cache):
    b, t, _ = h.shape
    proj = jnp.einsum('btd,dc->btc', h, w['even_w_in'][i])
    c0 = A_Q_LORA
    c1 = c0 + A_KV_LORA
    c2 = c1 + A_ROPE
    cq = rms_norm(proj[..., :c0], w['mla_q_norm'][i])
    ckv = rms_norm(proj[..., c0:c1], w['mla_kv_norm'][i])
    kpe = rope(proj[:, :, None, c1:c2], pos, A_ROPE)[:, :, 0, :]
    qkv_b = proj[..., c2:].reshape(b, t, 3, B_HEADS, HEAD_DIM)
    q_b, k_b, v_b = qkv_b[:, :, 0], qkv_b[:, :, 1], qkv_b[:, :, 2]
    q_a = jnp.einsum('btr,rc->btc', cq, w['mla_w_uq'][i]).reshape(b, t, A_HEADS, A_QK)
    qn = q_a[..., :A_NOPE]
    qp = rope(q_a[..., A_NOPE:], pos, A_ROPE)
    w_ukv = w['mla_w_ukv'][i]
    rel_table = w['band_rel_bias'][i]
    band_fn = lambda qc, kc, vc, qpos, kpos: band_attend(qc, kc, vc, qpos, kpos, rel_table)
    if cache is None:
        kv = jnp.einsum('btr,rc->btc', ckv, w_ukv).reshape(b, t, A_HEADS, A_NOPE + A_V)
        out_a = mla_prompt(qn, qp, kv[..., :A_NOPE], kpe, kv[..., A_NOPE:])
        out_b = sweep_chunks(band_fn, q_b, k_b, v_b, B_REACH)
        keep = min(B_REACH, t)
        new = (ckv, kpe, k_b[:, t - keep:], v_b[:, t - keep:])
    else:
        c_ckv, c_kpe, c_k, c_v = cache
        ckv_all = jnp.concatenate([c_ckv, ckv], axis=1)
        kpe_all = jnp.concatenate([c_kpe, kpe], axis=1)
        n_all = ckv_all.shape[1]
        kv = jnp.einsum('btr,rc->btc', ckv_all, w_ukv).reshape(b, n_all, A_HEADS, A_NOPE + A_V)
        out_a = mla_attend(qn, qp, kv[..., :A_NOPE], kpe_all, kv[..., A_NOPE:], jnp.ones((t, n_all), bool))
        nb = c_k.shape[1]
        k_pos = pos[0] - nb + jnp.arange(nb + t)
        out_b = band_fn(q_b, jnp.concatenate([c_k, k_b], axis=1), jnp.concatenate([c_v, v_b], axis=1), pos, k_pos)
        new = (ckv, kpe, k_b, v_b)
    out = jnp.concatenate([out_a, out_b], axis=-1)
    return jnp.einsum('btc,cd->btd', out, w['even_w_out'][i]), new


def odd_mixer(h, pos, i, w, cache):
    b, t, _ = h.shape
    proj = jnp.einsum('btd,dc->btc', h, w['odd_w_in'][i])
    nq = C_HEADS * HEAD_DIM
    nk = C_KV_HEADS * HEAD_DIM
    q = rope(proj[..., :nq].reshape(b, t, C_HEADS, HEAD_DIM), pos, C_ROT)
    k = rope(proj[..., nq:nq + nk].reshape(b, t, C_KV_HEADS, HEAD_DIM), pos, C_ROT)
    v = proj[..., nq + nk:].reshape(b, t, C_KV_HEADS, HEAD_DIM)
    sinks = w['swa_sinks'][i]
    swa_fn = lambda qc, kc, vc, qpos, kpos: swa_attend(qc, kc, vc, kpos, sinks)
    if cache is None:
        out = sweep_chunks(swa_fn, q, k, v, C_WINDOW)
        keep = min(C_WINDOW, t)
        new = (k[:, t - keep:], v[:, t - keep:])
    else:
        c_k, c_v = cache
        nc = c_k.shape[1]
        k_pos = pos[0] - nc + jnp.arange(nc + t)
        out = swa_fn(q, jnp.concatenate([c_k, k], axis=1), jnp.concatenate([c_v, v], axis=1), pos, k_pos)
        new = (k, v)
    return jnp.einsum('btc,cd->btd', out, w['odd_w_out'][i]), new


def trunk(x, pos, caches, w):
    new = ([], [], [], [], [], [])
    for l in range(DEPTH):
        g = w['norm_g'][l]
        ff_a = swiglu(rms_norm(x, g[0]), w['ffn_w_gate'][l, 0], w['ffn_w_up'][l, 0], w['ffn_w_down'][l, 0])
        x = x + 0.5 * rms_norm(ff_a, g[1])
        h = rms_norm(x, g[2])
        i = l // 2
        if l % 2 == 0:
            lc = None if caches is None else (caches[0][i], caches[1][i], caches[2][i], caches[3][i])
            mix, st = even_mixer(h, pos, i, w, lc)
            for j in range(4):
                new[j].append(st[j])
        else:
            lc = None if caches is None else (caches[4][i], caches[5][i])
            mix, st = odd_mixer(h, pos, i, w, lc)
            new[4].append(st[0])
            new[5].append(st[1])
        x = x + rms_norm(mix, g[3])
        ff_b = swiglu(rms_norm(x, g[4]), w['ffn_w_gate'][l, 1], w['ffn_w_up'][l, 1], w['ffn_w_down'][l, 1])
        x = x + 0.5 * rms_norm(ff_b, g[5])
    return x, [jnp.stack(s, axis=0) for s in new]


def setup_inputs(seed: int = 0) -> dict:
    key = jax.random.key(seed)
    ks = jax.random.split(key, 22)
    nrm = lambda k, shape, s: jax.random.normal(k, shape, jnp.float32) * s
    b_cache = min(B_REACH, PAST_LEN)
    c_cache = min(C_WINDOW, PAST_LEN)
    return {
        'x_prompt': nrm(ks[0], (BATCH, SEQ, D_MODEL), 1.0),
        'x_sample': nrm(ks[1], (DEC_BATCH, DEC_SEQ, D_MODEL), 1.0),
        'cache_mla_ckv': nrm(ks[2], (N_EVEN, DEC_BATCH, PAST_LEN, A_KV_LORA), 1.0),
        'cache_mla_kpe': nrm(ks[3], (N_EVEN, DEC_BATCH, PAST_LEN, A_ROPE), 1.0),
        'cache_band_k': nrm(ks[4], (N_EVEN, DEC_BATCH, b_cache, B_HEADS, HEAD_DIM), 1.0),
        'cache_band_v': nrm(ks[5], (N_EVEN, DEC_BATCH, b_cache, B_HEADS, HEAD_DIM), 1.0),
        'cache_swa_k': nrm(ks[6], (N_ODD, DEC_BATCH, c_cache, C_KV_HEADS, HEAD_DIM), 1.0),
        'cache_swa_v': nrm(ks[7], (N_ODD, DEC_BATCH, c_cache, C_KV_HEADS, HEAD_DIM), 1.0),
        'norm_g': 1.0 + nrm(ks[8], (DEPTH, 6, D_MODEL), 0.02),
        'ffn_w_gate': nrm(ks[9], (DEPTH, 2, D_MODEL, D_FF), D_MODEL ** -0.5),
        'ffn_w_up': nrm(ks[10], (DEPTH, 2, D_MODEL, D_FF), D_MODEL ** -0.5),
        'ffn_w_down': nrm(ks[11], (DEPTH, 2, D_FF, D_MODEL), D_FF ** -0.5),
        'even_w_in': nrm(ks[12], (N_EVEN, D_MODEL, EVEN_IN), D_MODEL ** -0.5),
        'mla_q_norm': 1.0 + nrm(ks[13], (N_EVEN, A_Q_LORA), 0.02),
        'mla_w_uq': nrm(ks[14], (N_EVEN, A_Q_LORA, A_HEADS * A_QK), A_Q_LORA ** -0.5),
        'mla_kv_norm': 1.0 + nrm(ks[15], (N_EVEN, A_KV_LORA), 0.02),
        'mla_w_ukv': nrm(ks[16], (N_EVEN, A_KV_LORA, A_HEADS * (A_NOPE + A_V)), A_KV_LORA ** -0.5),
        'band_rel_bias': nrm(ks[17], (N_EVEN, B_HEADS, 2 * B_MAX_REL + 1), 0.5),
        'even_w_out': nrm(ks[18], (N_EVEN, MIX_EVEN, D_MODEL), MIX_EVEN ** -0.5),
        'odd_w_in': nrm(ks[19], (N_ODD, D_MODEL, ODD_IN), D_MODEL ** -0.5),
        'swa_sinks': nrm(ks[20], (N_ODD, C_HEADS), 1.0),
        'odd_w_out': nrm(ks[21], (N_ODD, MIX_ODD, D_MODEL), MIX_ODD ** -0.5),
    }


def reference(x_prompt, x_sample, cache_mla_ckv, cache_mla_kpe, cache_band_k, cache_band_v,
              cache_swa_k, cache_swa_v, norm_g, ffn_w_gate, ffn_w_up, ffn_w_down, even_w_in,
              mla_q_norm, mla_w_uq, mla_kv_norm, mla_w_ukv, band_rel_bias, even_w_out,
              odd_w_in, swa_sinks, odd_w_out):
    w = dict(norm_g=norm_g, ffn_w_gate=ffn_w_gate, ffn_w_up=ffn_w_up, ffn_w_down=ffn_w_down,
             even_w_in=even_w_in, mla_q_norm=mla_q_norm, mla_w_uq=mla_w_uq, mla_kv_norm=mla_kv_norm,
             mla_w_ukv=mla_w_ukv, band_rel_bias=band_rel_bias, even_w_out=even_w_out,
             odd_w_in=odd_w_in, swa_sinks=swa_sinks, odd_w_out=odd_w_out)
    seq = x_prompt.shape[1]
    past = cache_mla_ckv.shape[2]
    t = x_sample.shape[1]
    pos_p = jnp.arange(seq, dtype=jnp.int32)
    pos_s = past + jnp.arange(t, dtype=jnp.int32)
    y_prompt, st_p = trunk(x_prompt, pos_p, None, w)
    caches = (cache_mla_ckv, cache_mla_kpe, cache_band_k, cache_band_v, cache_swa_k, cache_swa_v)
    y_sample, st_s = trunk(x_sample, pos_s, caches, w)
    ckv_p, kpe_p, bk_p, bv_p, sk_p, sv_p = st_p
    ckv_s, kpe_s, bk_s, bv_s, sk_s, sv_s = st_s
    return (y_prompt, y_sample, ckv_p, kpe_p, bk_p, bv_p, sk_p, sv_p, ckv_s, kpe_s, bk_s, bv_s, sk_s, sv_s)
```

```python
import functools

import jax
import jax.numpy as jnp
from jax import lax
from jax.experimental import pallas as pl
from jax.experimental.pallas import tpu as pltpu

F32 = jnp.float32
BF16 = jnp.bfloat16

D_MODEL = 1024
D_FF = 2816
CHUNK = 64
HEAD_DIM = 64
ROPE_THETA = 500000.0
NORM_EPS = 1e-6
NEG_INF = -1e30
A_HEADS = 8
A_Q_LORA = 256
A_KV_LORA = 128
A_NOPE = 64
A_ROPE = 32
A_QK = A_NOPE + A_ROPE
A_V = 64
A_SCALE = A_QK ** -0.5
B_HEADS = 8
B_REACH = 512
B_MAX_REL = 128
C_HEADS = 16
C_KV_HEADS = 2
C_WINDOW = 128
C_ROT = HEAD_DIM // 4
HEAD_SCALE = HEAD_DIM ** -0.5

LANES = 128
TOKEN_TILE = 512
FF_TILE = 256
ATT_TILE = 128
MLA_TILE = 256
VMEM_LIMIT = 56 * 1024 * 1024


def _params(*sem):
    return pltpu.CompilerParams(dimension_semantics=sem, vmem_limit_bytes=VMEM_LIMIT)


def _const_spec(shape):
    nd = len(shape)
    return pl.BlockSpec(shape, lambda *_: (0,) * nd, pipeline_mode=pl.Buffered(1))


def _rms(x, g):
    ms = jnp.mean(x * x, axis=-1, keepdims=True)
    return x * lax.rsqrt(ms + NORM_EPS) * g


def _dot(a, b):
    return jnp.dot(a, b, preferred_element_type=F32)


def _dot_nt(a, b):
    return lax.dot_general(a, b, (((1,), (1,)), ((), ())), preferred_element_type=F32)


def _rope_group(x, c, s1, s2, half):
    return x * c + pltpu.roll(x, half, 1) * s1 + pltpu.roll(x, LANES - half, 1) * s2


def _ffn_kernel(x_ref, g_ref, wg_ref, wu_ref, wd_ref, o_ref):
    x = x_ref[...]
    xn = _rms(x, g_ref[0:1, :]).astype(BF16)
    acc = jnp.zeros(x.shape, F32)
    for c in range(D_FF // FF_TILE):
        sl = slice(c * FF_TILE, (c + 1) * FF_TILE)
        a = _dot(xn, wg_ref[:, sl])
        b = _dot(xn, wu_ref[:, sl])
        h = (a * jax.nn.sigmoid(a) * b).astype(BF16)
        acc = acc + _dot(h, wd_ref[sl, :])
    o_ref[...] = x + 0.5 * _rms(acc, g_ref[1:2, :])


def _ffn(x, g, wg, wu, wd):
    t = x.shape[0]
    tile = pl.BlockSpec((TOKEN_TILE, D_MODEL), lambda i: (i, 0))
    return pl.pallas_call(
        _ffn_kernel,
        out_shape=jax.ShapeDtypeStruct((t, D_MODEL), F32),
        grid=(t // TOKEN_TILE,),
        in_specs=[tile, _const_spec((2, D_MODEL)), _const_spec((D_MODEL, D_FF)),
                  _const_spec((D_MODEL, D_FF)), _const_spec((D_FF, D_MODEL))],
        out_specs=tile,
        compiler_params=_params("parallel"),
        name="ffn",
    )(x, g, wg, wu, wd)


def _even_proj_kernel(x_ref, g_ref, win_ref, qn_ref, kvn_ref, wuq_ref, wukv_ref,
                      c_ref, s1_ref, s2_ref,
                      qa_ref, ka_ref, va_ref, qb_ref, kb_ref, vb_ref,
                      ckv_ref, kpe_ref, kbf_ref, vbf_ref):
    h = _rms(x_ref[...], g_ref[...]).astype(BF16)
    proj = _dot(h, win_ref[...])
    cq = _rms(proj[:, 0:256], qn_ref[...]).astype(BF16)
    ckv = _rms(proj[:, 256:384], kvn_ref[...])
    c, s1, s2 = c_ref[...], s1_ref[...], s2_ref[...]
    half = A_ROPE // 2
    kpe = _rope_group(proj[:, 384:512], c, s1, s2, half)
    ckv_ref[...] = ckv
    kpe_ref[...] = pltpu.roll(kpe, LANES - A_NOPE, 1)[:, 0:A_ROPE]
    qa = _dot(cq, wuq_ref[...])
    kv = _dot(ckv.astype(BF16), wukv_ref[...])
    for hd in range(A_HEADS):
        sl = slice(hd * LANES, (hd + 1) * LANES)
        qa_ref[:, sl] = (_rope_group(qa[:, sl], c, s1, s2, half) * A_SCALE).astype(BF16)
        ka_ref[:, sl] = (kv[:, sl] + kpe).astype(BF16)
    va_ref[...] = kv[:, A_HEADS * LANES:].astype(BF16)
    qb_ref[...] = (proj[:, 512:1024] * HEAD_SCALE).astype(BF16)
    kb = proj[:, 1024:1536]
    vb = proj[:, 1536:2048]
    kb_ref[...] = kb.astype(BF16)
    vb_ref[...] = vb.astype(BF16)
    kbf_ref[...] = kb
    vbf_ref[...] = vb


def _rope_specs(n_prompt_tiles, tiles_per_seq):
    def idx(i):
        return (jnp.where(i < n_prompt_tiles, i % tiles_per_seq, tiles_per_seq), 0)
    return [pl.BlockSpec((TOKEN_TILE, LANES), idx)] * 3


def _even_proj(x, g, win, qn, kvn, wuq, wukv, tabs, n_prompt_tiles, tiles_per_seq):
    t = x.shape[0]

    def tile(w):
        return pl.BlockSpec((TOKEN_TILE, w), lambda i: (i, 0))

    widths = [(8 * LANES, BF16), (8 * LANES, BF16), (512, BF16), (512, BF16), (512, BF16), (512, BF16),
              (A_KV_LORA, F32), (A_ROPE, F32), (512, F32), (512, F32)]
    return pl.pallas_call(
        _even_proj_kernel,
        out_shape=[jax.ShapeDtypeStruct((t, w), d) for w, d in widths],
        grid=(t // TOKEN_TILE,),
        in_specs=[tile(D_MODEL), _const_spec((1, D_MODEL)), _const_spec(win.shape),
                  _const_spec((1, A_Q_LORA)), _const_spec((1, A_KV_LORA)),
                  _const_spec(wuq.shape), _const_spec(wukv.shape)]
                 + _rope_specs(n_prompt_tiles, tiles_per_seq),
        out_specs=[tile(w) for w, _ in widths],
        compiler_params=_params("parallel"),
        name="even_proj",
    )(x, g, win, qn, kvn, wuq, wukv, *tabs)


def _odd_proj_kernel(x_ref, g_ref, win_ref, c_ref, s1_ref, s2_ref,
                     q_ref, k_ref, v_ref, kf_ref, vf_ref):
    h = _rms(x_ref[...], g_ref[...]).astype(BF16)
    proj = _dot(h, win_ref[...])
    c, s1, s2 = c_ref[...], s1_ref[...], s2_ref[...]
    half = C_ROT // 2
    nq = C_HEADS * HEAD_DIM
    for grp in range(nq // LANES):
        sl = slice(grp * LANES, (grp + 1) * LANES)
        q_ref[:, sl] = (_rope_group(proj[:, sl], c, s1, s2, half) * HEAD_SCALE).astype(BF16)
    k = _rope_group(proj[:, nq:nq + LANES], c, s1, s2, half)
    v = proj[:, nq + LANES:nq + 2 * LANES]
    k_ref[...] = k.astype(BF16)
    v_ref[...] = v.astype(BF16)
    kf_ref[...] = k
    vf_ref[...] = v


def _odd_proj(x, g, win, tabs, n_prompt_tiles, tiles_per_seq):
    t = x.shape[0]

    def tile(w):
        return pl.BlockSpec((TOKEN_TILE, w), lambda i: (i, 0))

    widths = [(C_HEADS * HEAD_DIM, BF16), (LANES, BF16), (LANES, BF16), (LANES, F32), (LANES, F32)]
    return pl.pallas_call(
        _odd_proj_kernel,
        out_shape=[jax.ShapeDtypeStruct((t, w), d) for w, d in widths],
        grid=(t // TOKEN_TILE,),
        in_specs=[tile(D_MODEL), _const_spec((1, D_MODEL)), _const_spec(win.shape)]
                 + _rope_specs(n_prompt_tiles, tiles_per_seq),
        out_specs=[tile(w) for w, _ in widths],
        compiler_params=_params("parallel"),
        name="odd_proj",
    )(x, g, win, *tabs)


def _out_proj_kernel(x_ref, mix_ref, g_ref, wo_ref, o_ref):
    y = _dot(mix_ref[...], wo_ref[...])
    o_ref[...] = x_ref[...] + _rms(y, g_ref[...])


def _out_proj(x, mix, g, wo):
    t = x.shape[0]
    tile = pl.BlockSpec((TOKEN_TILE, D_MODEL), lambda i: (i, 0))
    return pl.pallas_call(
        _out_proj_kernel,
        out_shape=jax.ShapeDtypeStruct((t, D_MODEL), F32),
        grid=(t // TOKEN_TILE,),
        in_specs=[tile, tile, _const_spec((1, D_MODEL)), _const_spec(wo.shape)],
        out_specs=tile,
        compiler_params=_params("parallel"),
        name="out_proj",
    )(x, mix, g, wo)


def _band_bias_kernel(tab_ref, o_ref):
    hd = pl.program_id(0)
    w = B_REACH + ATT_TILE
    row = lax.broadcasted_iota(jnp.int32, (ATT_TILE, w), 0)
    col = lax.broadcasted_iota(jnp.int32, (ATT_TILE, w), 1)
    rel = jnp.clip(row - col + B_REACH, -B_MAX_REL, B_MAX_REL) + B_MAX_REL
    lo = (row // CHUNK) * CHUNK
    visible = (col >= lo) & (col < lo + B_REACH + CHUNK)

    def body(r, acc):
        return jnp.where(rel == r, tab_ref[hd, r], acc)

    bias = lax.fori_loop(0, 2 * B_MAX_REL + 1, body, jnp.zeros((ATT_TILE, w), F32))
    o_ref[...] = jnp.where(visible, bias, NEG_INF)


def _band_bias(table):
    w = B_REACH + ATT_TILE
    return pl.pallas_call(
        _band_bias_kernel,
        out_shape=jax.ShapeDtypeStruct((B_HEADS, ATT_TILE, w), F32),
        grid=(B_HEADS,),
        in_specs=[pl.BlockSpec(memory_space=pltpu.SMEM)],
        out_specs=pl.BlockSpec((None, ATT_TILE, w), lambda h: (h, 0, 0)),
        compiler_params=_params("arbitrary"),
        name="band_bias",
    )(table)


def _softmax_pv(s, v, sink=None):
    m = jnp.max(s, axis=-1, keepdims=True)
    if sink is not None:
        m = jnp.maximum(m, sink)
    p = jnp.exp(s - m)
    l = jnp.sum(p, axis=-1, keepdims=True)
    if sink is not None:
        l = l + jnp.exp(sink - m)
    return _dot(p.astype(BF16), v) / l


def _mla_prompt_kernel(q_ref, k_ref, v_ref, o_ref):
    i = pl.program_id(1)
    t = MLA_TILE
    qc = lax.broadcasted_iota(jnp.int32, (t, t), 0) // CHUNK
    kc = lax.broadcasted_iota(jnp.int32, (t, t), 1) // CHUNK
    diag_ok = kc <= qc
    outs = []
    for hd in range(A_HEADS):
        q = q_ref[:, hd * LANES:(hd + 1) * LANES]

        def step(j, carry, masked, q=q, hd=hd):
            m, l, acc = carry
            ks = pl.multiple_of(j * t, t)
            k = k_ref[pl.ds(ks, t), hd * LANES:(hd + 1) * LANES]
            v = v_ref[pl.ds(ks, t), hd * A_V:(hd + 1) * A_V]
            s = _dot_nt(q, k)
            if masked:
                s = jnp.where(diag_ok, s, NEG_INF)
            m_new = jnp.maximum(m, jnp.max(s, axis=-1, keepdims=True))
            alpha = jnp.exp(m - m_new)
            p = jnp.exp(s - m_new)
            l = alpha * l + jnp.sum(p, axis=-1, keepdims=True)
            acc = alpha * acc + _dot(p.astype(BF16), v)
            return m_new, l, acc

        init = (jnp.full((t, 1), NEG_INF, F32), jnp.zeros((t, 1), F32), jnp.zeros((t, A_V), F32))
        carry = lax.fori_loop(0, i, functools.partial(step, masked=False), init)
        _, l, acc = step(i, carry, True)
        outs.append(acc / l)
    o_ref[...] = jnp.concatenate(outs, axis=-1).astype(BF16)


def _mla_prompt(qa, ka, va, batch, seq):
    nq = seq // MLA_TILE
    return pl.pallas_call(
        _mla_prompt_kernel,
        out_shape=jax.ShapeDtypeStruct((batch * seq, A_HEADS * A_V), BF16),
        grid=(batch, nq),
        in_specs=[pl.BlockSpec((MLA_TILE, A_HEADS * LANES), lambda b, i: (b * nq + i, 0)),
                  pl.BlockSpec((seq, A_HEADS * LANES), lambda b, i: (b, 0)),
                  pl.BlockSpec((seq, A_HEADS * A_V), lambda b, i: (b, 0))],
        out_specs=pl.BlockSpec((MLA_TILE, A_HEADS * A_V), lambda b, i: (b * nq + i, 0)),
        compiler_params=_params("parallel", "arbitrary"),
        name="mla_prompt",
    )(qa, ka, va)


def _fill_padded(pad_ref, src_ref, reach):
    pad_ref[0:reach, :] = jnp.zeros((reach, pad_ref.shape[1]), pad_ref.dtype)
    pad_ref[reach:, :] = src_ref[...]


def _band_prompt_kernel(q_ref, k_ref, v_ref, bias_ref, o_ref, kpad, vpad):
    i = pl.program_id(1)

    @pl.when(i == 0)
    def _():
        _fill_padded(kpad, k_ref, B_REACH)
        _fill_padded(vpad, v_ref, B_REACH)

    w = B_REACH + ATT_TILE
    start = pl.multiple_of(i * ATT_TILE, ATT_TILE)
    col = lax.broadcasted_iota(jnp.int32, (ATT_TILE, w), 1)
    in_seq = col >= B_REACH - i * ATT_TILE
    outs = []
    for hd in range(B_HEADS):
        sl = slice(hd * HEAD_DIM, (hd + 1) * HEAD_DIM)
        s = _dot_nt(q_ref[:, sl], kpad[pl.ds(start, w), sl]) + bias_ref[hd]
        s = jnp.where(in_seq, s, NEG_INF)
        outs.append(_softmax_pv(s, vpad[pl.ds(start, w), sl]))
    o_ref[...] = jnp.concatenate(outs, axis=-1).astype(BF16)


def _band_prompt(qb, kb, vb, bias, batch, seq):
    nq = seq // ATT_TILE
    wd = B_HEADS * HEAD_DIM
    return pl.pallas_call(
        _band_prompt_kernel,
        out_shape=jax.ShapeDtypeStruct((batch * seq, wd), BF16),
        grid=(batch, nq),
        in_specs=[pl.BlockSpec((ATT_TILE, wd), lambda b, i: (b * nq + i, 0)),
                  pl.BlockSpec((seq, wd), lambda b, i: (b, 0)),
                  pl.BlockSpec((seq, wd), lambda b, i: (b, 0)),
                  _const_spec(bias.shape)],
        out_specs=pl.BlockSpec((ATT_TILE, wd), lambda b, i: (b * nq + i, 0)),
        scratch_shapes=[pltpu.VMEM((seq + B_REACH, wd), BF16), pltpu.VMEM((seq + B_REACH, wd), BF16)],
        compiler_params=_params("parallel", "arbitrary"),
        name="band_prompt",
    )(qb, kb, vb, bias)


def _swa_prompt_kernel(sink_ref, q_ref, k_ref, v_ref, o_ref, kpad, vpad, *, layer):
    i = pl.program_id(1)

    @pl.when(i == 0)
    def _():
        _fill_padded(kpad, k_ref, C_WINDOW)
        _fill_padded(vpad, v_ref, C_WINDOW)

    w = C_WINDOW + ATT_TILE
    start = pl.multiple_of(i * ATT_TILE, ATT_TILE)
    row = lax.broadcasted_iota(jnp.int32, (ATT_TILE, w), 0)
    col = lax.broadcasted_iota(jnp.int32, (ATT_TILE, w), 1)
    lo = (row // CHUNK) * CHUNK
    visible = (col >= lo) & (col < lo + C_WINDOW + CHUNK) & (col >= C_WINDOW - i * ATT_TILE)
    group = C_HEADS // C_KV_HEADS
    outs = []
    for hd in range(C_HEADS):
        kv = hd // group
        ksl = slice(kv * HEAD_DIM, (kv + 1) * HEAD_DIM)
        s = _dot_nt(q_ref[:, hd * HEAD_DIM:(hd + 1) * HEAD_DIM], kpad[pl.ds(start, w), ksl])
        s = jnp.where(visible, s, NEG_INF)
        outs.append(_softmax_pv(s, vpad[pl.ds(start, w), ksl], sink_ref[layer, hd]))
    o_ref[...] = jnp.concatenate(outs, axis=-1).astype(BF16)


def _swa_prompt(sinks, layer, q, k, v, batch, seq):
    nq = seq // ATT_TILE
    wq = C_HEADS * HEAD_DIM
    wk = C_KV_HEADS * HEAD_DIM
    return pl.pallas_call(
        functools.partial(_swa_prompt_kernel, layer=layer),
        out_shape=jax.ShapeDtypeStruct((batch * seq, wq), BF16),
        grid=(batch, nq),
        in_specs=[pl.BlockSpec(memory_space=pltpu.SMEM),
                  pl.BlockSpec((ATT_TILE, wq), lambda b, i: (b * nq + i, 0)),
                  pl.BlockSpec((seq, wk), lambda b, i: (b, 0)),
                  pl.BlockSpec((seq, wk), lambda b, i: (b, 0))],
        out_specs=pl.BlockSpec((ATT_TILE, wq), lambda b, i: (b * nq + i, 0)),
        scratch_shapes=[pltpu.VMEM((seq + C_WINDOW, wk), BF16), pltpu.VMEM((seq + C_WINDOW, wk), BF16)],
        compiler_params=_params("parallel", "arbitrary"),
        name="swa_prompt",
    )(sinks, q, k, v)


def _softmax_pv2(s_c, s_n, v_c, v_n, sink=None):
    m = jnp.maximum(jnp.max(s_c, axis=-1, keepdims=True), jnp.max(s_n, axis=-1, keepdims=True))
    if sink is not None:
        m = jnp.maximum(m, sink)
    p_c = jnp.exp(s_c - m)
    p_n = jnp.exp(s_n - m)
    l = jnp.sum(p_c, axis=-1, keepdims=True) + jnp.sum(p_n, axis=-1, keepdims=True)
    if sink is not None:
        l = l + jnp.exp(sink - m)
    return (_dot(p_c.astype(BF16), v_c) + _dot(p_n.astype(BF16), v_n)) / l


def _even_sample_kernel(qa_ref, ka_ref, va_ref, qb_ref, kb_ref, vb_ref,
                        cckv_ref, ckpe_ref, cbk_ref, cbv_ref, wukv_ref, place_ref, bias_ref,
                        oa_ref, ob_ref):
    t = qa_ref.shape[0]
    nb = cbk_ref.shape[0]
    kvc = _dot(cckv_ref[...].astype(BF16), wukv_ref[...])
    kpe = _dot(ckpe_ref[...].astype(BF16), place_ref[...])
    outs_a, outs_b = [], []
    for hd in range(A_HEADS):
        sl = slice(hd * LANES, (hd + 1) * LANES)
        vsl = slice(hd * A_V, (hd + 1) * A_V)
        k_c = (kvc[:, sl] + kpe).astype(BF16)
        v_c = kvc[:, A_HEADS * LANES + hd * A_V:A_HEADS * LANES + (hd + 1) * A_V].astype(BF16)
        q = qa_ref[:, sl]
        outs_a.append(_softmax_pv2(_dot_nt(q, k_c), _dot_nt(q, ka_ref[:, sl]), v_c, va_ref[:, vsl]))
    for hd in range(B_HEADS):
        sl = slice(hd * HEAD_DIM, (hd + 1) * HEAD_DIM)
        q = qb_ref[:, sl]
        s_c = _dot_nt(q, cbk_ref[:, sl].astype(BF16)) + bias_ref[hd, 0:t, 0:nb]
        s_n = _dot_nt(q, kb_ref[:, sl]) + bias_ref[hd, 0:t, nb:nb + t]
        outs_b.append(_softmax_pv2(s_c, s_n, cbv_ref[:, sl].astype(BF16), vb_ref[:, sl]))
    oa_ref[...] = jnp.concatenate(outs_a, axis=-1).astype(BF16)
    ob_ref[...] = jnp.concatenate(outs_b, axis=-1).astype(BF16)


def _even_sample(layer, qa, ka, va, qb, kb, vb, c_ckv, c_kpe, c_bk, c_bv, wukv, place, bias,
                 row0, dec_batch, t):
    past = c_ckv.shape[2]
    nb = c_bk.shape[2]
    wd = B_HEADS * HEAD_DIM
    base = row0 // t

    def new(w):
        return pl.BlockSpec((t, w), lambda b: (base + b, 0))

    def cache(n, w):
        return pl.BlockSpec((None, None, n, w), lambda b: (layer, b, 0, 0))

    out = pl.BlockSpec((t, wd), lambda b: (b, 0))
    return pl.pallas_call(
        _even_sample_kernel,
        out_shape=[jax.ShapeDtypeStruct((dec_batch * t, wd), BF16)] * 2,
        grid=(dec_batch,),
        in_specs=[new(A_HEADS * LANES), new(A_HEADS * LANES), new(wd), new(wd), new(wd), new(wd),
                  cache(past, A_KV_LORA), cache(past, A_ROPE), cache(nb, wd), cache(nb, wd),
                  _const_spec(wukv.shape), _const_spec(place.shape), _const_spec(bias.shape)],
        out_specs=[out, out],
        compiler_params=_params("parallel"),
        name="even_sample",
    )(qa, ka, va, qb, kb, vb, c_ckv, c_kpe, c_bk, c_bv, wukv, place, bias)


def _odd_sample_kernel(sink_ref, q_ref, k_ref, v_ref, ck_ref, cv_ref, o_ref, *, layer):
    group = C_HEADS // C_KV_HEADS
    outs = []
    for hd in range(C_HEADS):
        kv = hd // group
        ksl = slice(kv * HEAD_DIM, (kv + 1) * HEAD_DIM)
        q = q_ref[:, hd * HEAD_DIM:(hd + 1) * HEAD_DIM]
        outs.append(_softmax_pv2(_dot_nt(q, ck_ref[:, ksl].astype(BF16)), _dot_nt(q, k_ref[:, ksl]),
                                 cv_ref[:, ksl].astype(BF16), v_ref[:, ksl], sink_ref[layer, hd]))
    o_ref[...] = jnp.concatenate(outs, axis=-1).astype(BF16)


def _odd_sample(sinks, layer, q, k, v, c_k, c_v, row0, dec_batch, t):
    nc = c_k.shape[2]
    wq = C_HEADS * HEAD_DIM
    wk = C_KV_HEADS * HEAD_DIM
    base = row0 // t

    def new(w):
        return pl.BlockSpec((t, w), lambda b: (base + b, 0))

    def cache():
        return pl.BlockSpec((None, None, nc, wk), lambda b: (layer, b, 0, 0))

    return pl.pallas_call(
        functools.partial(_odd_sample_kernel, layer=layer),
        out_shape=jax.ShapeDtypeStruct((dec_batch * t, wq), BF16),
        grid=(dec_batch,),
        in_specs=[pl.BlockSpec(memory_space=pltpu.SMEM), new(wq), new(wk), new(wk), cache(), cache()],
        out_specs=pl.BlockSpec((t, wq), lambda b: (b, 0)),
        compiler_params=_params("parallel"),
        name="odd_sample",
    )(sinks, q, k, v, c_k, c_v)


def _rope_tables(pos, n_rot, group):
    half = n_rot // 2
    inv = ROPE_THETA ** (-jnp.arange(half, dtype=F32) / half)
    ang = pos.astype(F32)[:, None] * inv[None, :]
    cos, sin = jnp.cos(ang), jnp.sin(ang)
    n = pos.shape[0]
    ones = jnp.ones((n, 1), F32)
    zeros = jnp.zeros((n, 1), F32)

    def head(first, second, fill, pre, post):
        return jnp.concatenate([jnp.tile(fill, (1, pre)), first, second, jnp.tile(fill, (1, post))], axis=1)

    return half, cos, sin, ones, zeros, head


def _mla_rope_tables(pos):
    half, cos, sin, ones, zeros, head = _rope_tables(pos, A_ROPE, LANES)
    pre, post = A_NOPE, LANES - A_NOPE - A_ROPE
    c = head(cos, cos, ones, pre, post)
    s1 = head(jnp.zeros_like(sin), sin, zeros, pre, post)
    s2 = head(-sin, jnp.zeros_like(sin), zeros, pre, post)
    return c, s1, s2


def _swa_rope_tables(pos):
    half, cos, sin, ones, zeros, head = _rope_tables(pos, C_ROT, HEAD_DIM)
    post = HEAD_DIM - C_ROT
    rep = LANES // HEAD_DIM
    c = jnp.tile(head(cos, cos, ones, 0, post), (1, rep))
    s1 = jnp.tile(head(jnp.zeros_like(sin), sin, zeros, 0, post), (1, rep))
    s2 = jnp.tile(head(-sin, jnp.zeros_like(sin), zeros, 0, post), (1, rep))
    return c, s1, s2


def kernel(x_prompt, x_sample, cache_mla_ckv, cache_mla_kpe, cache_band_k, cache_band_v,
           cache_swa_k, cache_swa_v, norm_g, ffn_w_gate, ffn_w_up, ffn_w_down, even_w_in,
           mla_q_norm, mla_w_uq, mla_kv_norm, mla_w_ukv, band_rel_bias, even_w_out,
           odd_w_in, swa_sinks, odd_w_out):
    batch, seq, _ = x_prompt.shape
    dec_batch, t_new, _ = x_sample.shape
    depth = norm_g.shape[0]
    n_even = even_w_in.shape[0]
    n_odd = odd_w_in.shape[0]
    past = cache_mla_ckv.shape[2]
    n_p = batch * seq
    n_s = dec_batch * t_new
    assert seq % TOKEN_TILE == 0 and n_s % TOKEN_TILE == 0 and TOKEN_TILE % t_new == 0
    assert seq % MLA_TILE == 0 and seq >= B_REACH and t_new <= CHUNK
    assert cache_band_k.shape[2] == B_REACH and cache_swa_k.shape[2] == C_WINDOW and past >= B_REACH
    n_prompt_tiles = n_p // TOKEN_TILE
    tiles_per_seq = seq // TOKEN_TILE

    pos = jnp.concatenate([jnp.arange(seq, dtype=jnp.int32),
                           past + (jnp.arange(TOKEN_TILE, dtype=jnp.int32) % t_new)])
    mla_tabs = _mla_rope_tables(pos)
    swa_tabs = _swa_rope_tables(pos)

    x = jnp.concatenate([x_prompt.reshape(n_p, D_MODEL), x_sample.reshape(n_s, D_MODEL)], axis=0)

    wg = ffn_w_gate.astype(BF16)
    wu = ffn_w_up.astype(BF16)
    wd = ffn_w_down.astype(BF16)

    place = jnp.zeros((A_ROPE, LANES), F32).at[jnp.arange(A_ROPE), A_NOPE + jnp.arange(A_ROPE)].set(1.0).astype(BF16)

    c0 = A_Q_LORA + A_KV_LORA
    c1 = c0 + A_ROPE
    new_p = [[] for _ in range(6)]
    new_s = [[] for _ in range(6)]
    for l in range(depth):
        g = norm_g[l]
        x = _ffn(x, g[0:2], wg[l, 0], wu[l, 0], wd[l, 0])
        i = l // 2
        if l % 2 == 0:
            w = even_w_in[i]
            zc = lambda n: jnp.zeros((D_MODEL, n), F32)
            win = jnp.concatenate([w[:, :c0], zc(A_NOPE), w[:, c0:c1], zc(LANES - A_NOPE - A_ROPE), w[:, c1:]],
                                  axis=1).astype(BF16)
            wuq = jnp.pad(mla_w_uq[i].reshape(A_Q_LORA, A_HEADS, A_QK),
                          ((0, 0), (0, 0), (0, LANES - A_QK))).reshape(A_Q_LORA, A_HEADS * LANES).astype(BF16)
            wkv = mla_w_ukv[i].reshape(A_KV_LORA, A_HEADS, A_NOPE + A_V)
            wk = jnp.pad(wkv[:, :, :A_NOPE], ((0, 0), (0, 0), (0, LANES - A_NOPE))).reshape(A_KV_LORA, A_HEADS * LANES)
            wv = wkv[:, :, A_NOPE:].reshape(A_KV_LORA, A_HEADS * A_V)
            wukv = jnp.concatenate([wk, wv], axis=1).astype(BF16)
            bias = _band_bias(band_rel_bias[i])
            qa, ka, va, qb, kb, vb, ckv, kpe, kbf, vbf = _even_proj(
                x, g[2:3], win, mla_q_norm[i][None], mla_kv_norm[i][None], wuq, wukv, mla_tabs,
                n_prompt_tiles, tiles_per_seq)
            oa_p = _mla_prompt(qa, ka, va, batch, seq)
            ob_p = _band_prompt(qb, kb, vb, bias, batch, seq)
            oa_s, ob_s = _even_sample(
                i, qa, ka, va, qb, kb, vb, cache_mla_ckv, cache_mla_kpe,
                cache_band_k.reshape(n_even, dec_batch, B_REACH, B_HEADS * HEAD_DIM),
                cache_band_v.reshape(n_even, dec_batch, B_REACH, B_HEADS * HEAD_DIM),
                wukv, place, bias, n_p, dec_batch, t_new)
            mix = jnp.concatenate([jnp.concatenate([oa_p, ob_p], axis=1),
                                   jnp.concatenate([oa_s, ob_s], axis=1)], axis=0)
            wo = even_w_out[i].astype(BF16)
            keep = min(B_REACH, seq)
            bshape = (B_HEADS, HEAD_DIM)
            new_p[0].append(ckv[:n_p].reshape(batch, seq, A_KV_LORA))
            new_p[1].append(kpe[:n_p].reshape(batch, seq, A_ROPE))
            new_p[2].append(kbf[:n_p].reshape(batch, seq, *bshape)[:, seq - keep:])
            new_p[3].append(vbf[:n_p].reshape(batch, seq, *bshape)[:, seq - keep:])
            new_s[0].append(ckv[n_p:].reshape(dec_batch, t_new, A_KV_LORA))
            new_s[1].append(kpe[n_p:].reshape(dec_batch, t_new, A_ROPE))
            new_s[2].append(kbf[n_p:].reshape(dec_batch, t_new, *bshape))
            new_s[3].append(vbf[n_p:].reshape(dec_batch, t_new, *bshape))
        else:
            q, k, v, kf, vf = _odd_proj(x, g[2:3], odd_w_in[i].astype(BF16), swa_tabs,
                                        n_prompt_tiles, tiles_per_seq)
            o_p = _swa_prompt(swa_sinks, i, q, k, v, batch, seq)
            o_s = _odd_sample(swa_sinks, i, q, k, v,
                              cache_swa_k.reshape(n_odd, dec_batch, C_WINDOW, C_KV_HEADS * HEAD_DIM),
                              cache_swa_v.reshape(n_odd, dec_batch, C_WINDOW, C_KV_HEADS * HEAD_DIM),
                              n_p, dec_batch, t_new)
            mix = jnp.concatenate([o_p, o_s], axis=0)
            wo = odd_w_out[i].astype(BF16)
            keep = min(C_WINDOW, seq)
            cshape = (C_KV_HEADS, HEAD_DIM)
            new_p[4].append(kf[:n_p].reshape(batch, seq, *cshape)[:, seq - keep:])
            new_p[5].append(vf[:n_p].reshape(batch, seq, *cshape)[:, seq - keep:])
            new_s[4].append(kf[n_p:].reshape(dec_batch, t_new, *cshape))
            new_s[5].append(vf[n_p:].reshape(dec_batch, t_new, *cshape))
        x = _out_proj(x, mix, g[3:4], wo)
        x = _ffn(x, g[4:6], wg[l, 1], wu[l, 1], wd[l, 1])

    y_prompt = x[:n_p].reshape(batch, seq, D_MODEL)
    y_sample = x[n_p:].reshape(dec_batch, t_new, D_MODEL)
    st_p = [jnp.stack(s, axis=0) for s in new_p]
    st_s = [jnp.stack(s, axis=0) for s in new_s]
    return (y_prompt, y_sample, *st_p, *st_s)
```

```python
import functools
import math

import jax
import jax.numpy as jnp
from jax import lax
from jax.experimental import pallas as pl
from jax.experimental.pallas import tpu as pltpu

F32 = jnp.float32
BF16 = jnp.bfloat16

D_MODEL = 1024
D_FF = 2816
CHUNK = 64
HEAD_DIM = 64
ROPE_THETA = 500000.0
NORM_EPS = 1e-6
NEG_INF = -1e30
A_HEADS = 8
A_Q_LORA = 256
A_KV_LORA = 128
A_NOPE = 64
A_ROPE = 32
A_QK = A_NOPE + A_ROPE
A_V = 64
A_SCALE = A_QK ** -0.5
B_HEADS = 8
B_REACH = 512
B_MAX_REL = 128
C_HEADS = 16
C_KV_HEADS = 2
C_WINDOW = 128
C_ROT = HEAD_DIM // 4
HEAD_SCALE = HEAD_DIM ** -0.5
LOG2E = math.log2(math.e)

LANES = 128
TOKEN_TILE = 512
FF_TILE = 256
ATT_TILE = 128
MLA_TILE = 256
MXU_LOOKAHEAD = 2
VMEM_LIMIT = 56 * 1024 * 1024
B_WIDTH = B_HEADS * HEAD_DIM
C_QWIDTH = C_HEADS * HEAD_DIM
C_KWIDTH = C_KV_HEADS * HEAD_DIM


def _params(*sem):
    return pltpu.CompilerParams(dimension_semantics=sem, vmem_limit_bytes=VMEM_LIMIT)


def _const_spec(shape, index=None):
    index = index or (0,) * len(shape)
    return pl.BlockSpec(shape, lambda *_: index, pipeline_mode=pl.Buffered(1))


def _rms(x, g):
    ms = jnp.mean(x * x, axis=-1, keepdims=True)
    return x * lax.rsqrt(ms + NORM_EPS) * g


def _dot(a, b):
    return jnp.dot(a, b, preferred_element_type=F32)


def _dot_nt(a, b):
    return lax.dot_general(a, b, (((1,), (1,)), ((), ())), preferred_element_type=F32)


def _rope_group(x, c, s1, s2, half):
    return x * c + pltpu.roll(x, half, 1) * s1 + pltpu.roll(x, LANES - half, 1) * s2


def _lane_mask(lo):
    lane = lax.broadcasted_iota(jnp.int32, (1, LANES), 1)
    return (lane >= lo) & (lane < lo + HEAD_DIM)


def _swiglu_update(x, g_ref, wg_ref, wu_ref, wd_ref):
    xn = _rms(x, g_ref[0:1, :]).astype(BF16)
    acc = jnp.zeros(x.shape, F32)
    for c in range(D_FF // FF_TILE):
        sl = slice(c * FF_TILE, (c + 1) * FF_TILE)
        a = _dot(xn, wg_ref[:, sl])
        b = _dot(xn, wu_ref[:, sl])
        h = (a * jax.nn.sigmoid(a) * b).astype(BF16)
        acc = acc + _dot(h, wd_ref[sl, :])
    return x + 0.5 * _rms(acc, g_ref[1:2, :])


def _ffn_kernel(x_ref, g_ref, wg_ref, wu_ref, wd_ref, o_ref):
    o_ref[...] = _swiglu_update(x_ref[...], g_ref, wg_ref, wu_ref, wd_ref)


def _mix_ffn_kernel(*refs, n_mix):
    x_ref = refs[0]
    mix_refs = refs[1:1 + n_mix]
    wo_ref, gm_ref, g_ref, wg_ref, wu_ref, wd_ref, o_ref = refs[1 + n_mix:]
    y = None
    row = 0
    for m_ref in mix_refs:
        w = m_ref.shape[1]
        part = _dot(m_ref[...], wo_ref[row:row + w, :])
        y = part if y is None else y + part
        row += w
    x = x_ref[...] + _rms(y, gm_ref[...])
    o_ref[...] = _swiglu_update(x, g_ref, wg_ref, wu_ref, wd_ref)


def _ffn_weight_specs(layer, which):
    idx = (layer, which, 0, 0)
    return [_const_spec((None, None, D_MODEL, D_FF), idx), _const_spec((None, None, D_MODEL, D_FF), idx),
            _const_spec((None, None, D_FF, D_MODEL), idx)]


def _ffn(x, g, wg, wu, wd, layer, which):
    t = x.shape[0]
    tile = pl.BlockSpec((TOKEN_TILE, D_MODEL), lambda i: (i, 0))
    return pl.pallas_call(
        _ffn_kernel,
        out_shape=jax.ShapeDtypeStruct((t, D_MODEL), F32),
        grid=(t // TOKEN_TILE,),
        in_specs=[tile, _const_spec((2, D_MODEL))] + _ffn_weight_specs(layer, which),
        out_specs=tile,
        compiler_params=_params("parallel"),
        name="ffn",
    )(x, g, wg, wu, wd)


def _mix_ffn(x, mixes, wo, gm, g, wg, wu, wd, layer, which):
    t = x.shape[0]
    tile = pl.BlockSpec((TOKEN_TILE, D_MODEL), lambda i: (i, 0))
    mix_specs = [pl.BlockSpec((TOKEN_TILE, m.shape[1]), lambda i: (i, 0)) for m in mixes]
    return pl.pallas_call(
        functools.partial(_mix_ffn_kernel, n_mix=len(mixes)),
        out_shape=jax.ShapeDtypeStruct((t, D_MODEL), F32),
        grid=(t // TOKEN_TILE,),
        in_specs=[tile] + mix_specs + [_const_spec(wo.shape), _const_spec((1, D_MODEL)), _const_spec((2, D_MODEL))]
                 + _ffn_weight_specs(layer, which),
        out_specs=tile,
        compiler_params=_params("parallel"),
        name="mix_ffn",
    )(x, *mixes, wo, gm, g, wg, wu, wd)


def _even_proj_kernel(x_ref, g_ref, win_ref, qn_ref, kvn_ref, wuq_ref, wukv_ref,
                      c_ref, s1_ref, s2_ref,
                      qa_ref, ka_ref, va_ref, qb_ref, kb_ref, vb_ref,
                      ckv_ref, kpe_ref, kbf_ref, vbf_ref, *, prompt, tiles_per_seq):
    h = _rms(x_ref[...], g_ref[...]).astype(BF16)
    proj = _dot(h, win_ref[...])
    cq = _rms(proj[:, 0:256], qn_ref[...]).astype(BF16)
    ckv = _rms(proj[:, 256:384], kvn_ref[...])
    c, s1, s2 = c_ref[...], s1_ref[...], s2_ref[...]
    half = A_ROPE // 2
    kpe = _rope_group(proj[:, 384:512], c, s1, s2, half)
    ckv_ref[...] = ckv
    kpe_ref[...] = pltpu.roll(kpe, LANES - A_NOPE, 1)[:, 0:A_ROPE]
    qa = _dot(cq, wuq_ref[...])
    kv = _dot(ckv.astype(BF16), wukv_ref[...])
    for hd in range(A_HEADS):
        sl = slice(hd * LANES, (hd + 1) * LANES)
        qa_ref[:, sl] = (_rope_group(qa[:, sl], c, s1, s2, half) * (A_SCALE * LOG2E)).astype(BF16)
        ka_ref[:, sl] = (kv[:, sl] + kpe).astype(BF16)
    for pair in range(B_HEADS // 2):
        grp = proj[:, 512 + pair * LANES:512 + (pair + 1) * LANES] * (HEAD_SCALE * LOG2E)
        for sub in range(2):
            qb_ref[2 * pair + sub] = jnp.where(_lane_mask(sub * HEAD_DIM), grp, 0.0).astype(BF16)
    va = kv[:, A_HEADS * LANES:]
    kb = proj[:, 1024:1536]
    vb = proj[:, 1536:2048]
    kb_ref[...] = kb.astype(BF16)
    if prompt:
        va_ref[...] = va.T.astype(BF16)
        vb_ref[...] = vb.T.astype(BF16)

        @pl.when(pl.program_id(0) % tiles_per_seq == tiles_per_seq - 1)
        def _():
            kbf_ref[...] = kb
            vbf_ref[...] = vb
    else:
        va_ref[...] = va.astype(BF16)
        vb_ref[...] = vb.astype(BF16)
        kbf_ref[...] = kb
        vbf_ref[...] = vb


def _last_tile_index(tiles_per_seq):
    return lambda i: (jnp.maximum((i + 1) // tiles_per_seq - 1, 0), 0)


def _even_proj(x, g, win, qn, kvn, wuq, wukv, tabs, prompt, batch, seq):
    t = x.shape[0]
    tiles_per_seq = seq // TOKEN_TILE if prompt else 1

    def tile(w):
        return pl.BlockSpec((TOKEN_TILE, w), lambda i: (i, 0))

    def tile_t(w):
        return pl.BlockSpec((w, TOKEN_TILE), lambda i: (0, i))

    def sds(shape, dtype):
        return jax.ShapeDtypeStruct(shape, dtype)

    wa = A_HEADS * LANES
    if prompt:
        v_shapes = [sds((A_HEADS * A_V, t), BF16), sds((B_WIDTH, t), BF16)]
        v_specs = [tile_t(A_HEADS * A_V), tile_t(B_WIDTH)]
        keep_shape = sds((batch * TOKEN_TILE, B_WIDTH), F32)
        keep_spec = pl.BlockSpec((TOKEN_TILE, B_WIDTH), _last_tile_index(tiles_per_seq))
        tab_idx = lambda i: (i % tiles_per_seq, 0)
    else:
        v_shapes = [sds((t, A_HEADS * A_V), BF16), sds((t, B_WIDTH), BF16)]
        v_specs = [tile(A_HEADS * A_V), tile(B_WIDTH)]
        keep_shape = sds((t, B_WIDTH), F32)
        keep_spec = tile(B_WIDTH)
        tab_idx = lambda i: (0, 0)
    qb_spec = pl.BlockSpec((B_HEADS, TOKEN_TILE, LANES), lambda i: (0, i, 0))
    out_shape = [sds((t, wa), BF16), sds((t, wa), BF16), v_shapes[0], sds((B_HEADS, t, LANES), BF16),
                 sds((t, B_WIDTH), BF16), v_shapes[1], sds((t, A_KV_LORA), F32), sds((t, A_ROPE), F32),
                 keep_shape, keep_shape]
    out_specs = [tile(wa), tile(wa), v_specs[0], qb_spec, tile(B_WIDTH), v_specs[1],
                 tile(A_KV_LORA), tile(A_ROPE), keep_spec, keep_spec]
    return pl.pallas_call(
        functools.partial(_even_proj_kernel, prompt=prompt, tiles_per_seq=tiles_per_seq),
        out_shape=out_shape,
        grid=(t // TOKEN_TILE,),
        in_specs=[tile(D_MODEL), _const_spec((1, D_MODEL)), _const_spec(win.shape),
                  _const_spec((1, A_Q_LORA)), _const_spec((1, A_KV_LORA)),
                  _const_spec(wuq.shape), _const_spec(wukv.shape)]
                 + [pl.BlockSpec((TOKEN_TILE, LANES), tab_idx)] * 3,
        out_specs=out_specs,
        compiler_params=_params("arbitrary"),
        name="even_proj",
    )(x, g, win, qn, kvn, wuq, wukv, *tabs)


def _odd_proj_kernel(x_ref, g_ref, win_ref, c_ref, s1_ref, s2_ref,
                     q_ref, k_ref, v_ref, kf_ref, vf_ref, *, prompt, tiles_per_seq):
    h = _rms(x_ref[...], g_ref[...]).astype(BF16)
    proj = _dot(h, win_ref[...])
    c, s1, s2 = c_ref[...], s1_ref[...], s2_ref[...]
    half = C_ROT // 2
    group = C_HEADS // C_KV_HEADS
    k = _rope_group(proj[:, C_QWIDTH:C_QWIDTH + LANES], c, s1, s2, half)
    v = proj[:, C_QWIDTH + LANES:C_QWIDTH + 2 * LANES]
    k_ref[...] = k.astype(BF16)
    if prompt:
        for pair in range(C_HEADS // 2):
            grp = _rope_group(proj[:, pair * LANES:(pair + 1) * LANES], c, s1, s2, half) * (HEAD_SCALE * LOG2E)
            swapped = pltpu.roll(grp, HEAD_DIM, 1)
            for sub in range(2):
                hd = 2 * pair + sub
                kv = hd // group
                src = grp if sub == kv else swapped
                q_ref[hd] = jnp.where(_lane_mask(kv * HEAD_DIM), src, 0.0).astype(BF16)
        v_ref[...] = v.T.astype(BF16)

        @pl.when(pl.program_id(0) % tiles_per_seq == tiles_per_seq - 1)
        def _():
            kf_ref[...] = k[TOKEN_TILE - C_WINDOW:, :]
            vf_ref[...] = v[TOKEN_TILE - C_WINDOW:, :]
    else:
        for grp in range(C_QWIDTH // LANES):
            sl = slice(grp * LANES, (grp + 1) * LANES)
            q_ref[:, sl] = (_rope_group(proj[:, sl], c, s1, s2, half) * (HEAD_SCALE * LOG2E)).astype(BF16)
        v_ref[...] = v.astype(BF16)
        kf_ref[...] = k
        vf_ref[...] = v


def _odd_proj(x, g, win, tabs, prompt, batch, seq):
    t = x.shape[0]
    tiles_per_seq = seq // TOKEN_TILE if prompt else 1

    def tile(w):
        return pl.BlockSpec((TOKEN_TILE, w), lambda i: (i, 0))

    def sds(shape, dtype):
        return jax.ShapeDtypeStruct(shape, dtype)

    if prompt:
        q_shape = sds((C_HEADS, t, LANES), BF16)
        q_spec = pl.BlockSpec((C_HEADS, TOKEN_TILE, LANES), lambda i: (0, i, 0))
        v_shape, v_spec = sds((LANES, t), BF16), pl.BlockSpec((LANES, TOKEN_TILE), lambda i: (0, i))
        keep_shape = sds((batch * C_WINDOW, LANES), F32)
        keep_spec = pl.BlockSpec((C_WINDOW, LANES), _last_tile_index(tiles_per_seq))
        tab_idx = lambda i: (i % tiles_per_seq, 0)
    else:
        q_shape, q_spec = sds((t, C_QWIDTH), BF16), tile(C_QWIDTH)
        v_shape, v_spec = sds((t, LANES), BF16), tile(LANES)
        keep_shape, keep_spec = sds((t, LANES), F32), tile(LANES)
        tab_idx = lambda i: (0, 0)
    return pl.pallas_call(
        functools.partial(_odd_proj_kernel, prompt=prompt, tiles_per_seq=tiles_per_seq),
        out_shape=[q_shape, sds((t, LANES), BF16), v_shape, keep_shape, keep_shape],
        grid=(t // TOKEN_TILE,),
        in_specs=[tile(D_MODEL), _const_spec((1, D_MODEL)), _const_spec(win.shape)]
                 + [pl.BlockSpec((TOKEN_TILE, LANES), tab_idx)] * 3,
        out_specs=[q_spec, tile(LANES), v_spec, keep_spec, keep_spec],
        compiler_params=_params("arbitrary"),
        name="odd_proj",
    )(x, g, win, *tabs)


def _band_bias_kernel(tab_ref, o_ref, ot_ref):
    hd = pl.program_id(0)
    w = B_REACH + ATT_TILE
    row = lax.broadcasted_iota(jnp.int32, (ATT_TILE, w), 0)
    col = lax.broadcasted_iota(jnp.int32, (ATT_TILE, w), 1)
    rel = jnp.clip(row - col + B_REACH, -B_MAX_REL, B_MAX_REL) + B_MAX_REL
    lo = (row // CHUNK) * CHUNK
    visible = (col >= lo) & (col < lo + B_REACH + CHUNK)

    def body(r, acc):
        return jnp.where(rel == r, tab_ref[hd, r], acc)

    bias = lax.fori_loop(0, 2 * B_MAX_REL + 1, body, jnp.zeros((ATT_TILE, w), F32))
    bias = jnp.where(visible, bias * LOG2E, NEG_INF)
    o_ref[...] = bias
    ot_ref[...] = bias.T


def _band_bias(table):
    w = B_REACH + ATT_TILE
    return pl.pallas_call(
        _band_bias_kernel,
        out_shape=[jax.ShapeDtypeStruct((B_HEADS, ATT_TILE, w), F32),
                   jax.ShapeDtypeStruct((B_HEADS, w, ATT_TILE), F32)],
        grid=(B_HEADS,),
        in_specs=[pl.BlockSpec(memory_space=pltpu.SMEM)],
        out_specs=[pl.BlockSpec((None, ATT_TILE, w), lambda h: (h, 0, 0)),
                   pl.BlockSpec((None, w, ATT_TILE), lambda h: (h, 0, 0))],
        compiler_params=_params("arbitrary"),
        name="band_bias",
    )(table)


def _mla_prompt_kernel(q_ref, k_ref, vt_ref, o_ref, m_ref, l_ref, acc_ref):
    i = pl.program_id(1)
    t = MLA_TILE
    m_ref[...] = jnp.full(m_ref.shape, NEG_INF, F32)
    l_ref[...] = jnp.zeros(l_ref.shape, F32)
    acc_ref[...] = jnp.zeros(acc_ref.shape, F32)
    kc = lax.broadcasted_iota(jnp.int32, (t, t), 0) // CHUNK
    qc = lax.broadcasted_iota(jnp.int32, (t, t), 1) // CHUNK
    diag_ok = kc <= qc

    def block(j, masked):
        ks = pl.multiple_of(j * t, t)
        m_all, l_all, acc_all = m_ref[...], l_ref[...], acc_ref[...]

        def scores(hd):
            sl = slice(hd * LANES, (hd + 1) * LANES)
            return _dot_nt(k_ref[pl.ds(ks, t), sl], q_ref[:, sl])

        pending = {hd: scores(hd) for hd in range(MXU_LOOKAHEAD)}
        new = []
        for hd in range(A_HEADS):
            if hd + MXU_LOOKAHEAD < A_HEADS:
                pending[hd + MXU_LOOKAHEAD] = scores(hd + MXU_LOOKAHEAD)
            s = pending.pop(hd)
            if masked:
                s = jnp.where(diag_ok, s, NEG_INF)
            m_new = jnp.maximum(m_all[hd], jnp.max(s, axis=0, keepdims=True))
            alpha = jnp.exp2(m_all[hd] - m_new)
            p = jnp.exp2(s - m_new)
            l_new = alpha * l_all[hd] + jnp.sum(p, axis=0, keepdims=True)
            pv = _dot(vt_ref[hd * A_V:(hd + 1) * A_V, pl.ds(ks, t)], p.astype(BF16))
            new.append((m_new, l_new, alpha * acc_all[hd] + pv))
        for hd, (m_new, l_new, acc_new) in enumerate(new):
            m_ref[hd] = m_new
            l_ref[hd] = l_new
            acc_ref[hd] = acc_new

    def body(j, carry):
        block(j, False)
        return carry

    lax.fori_loop(0, i, body, 0)
    block(i, True)
    outs = [acc_ref[hd] / l_ref[hd] for hd in range(A_HEADS)]
    o_ref[...] = jnp.concatenate(outs, axis=0).T.astype(BF16)


def _mla_prompt(qa, ka, vat, batch, seq):
    nq = seq // MLA_TILE
    wa = A_HEADS * LANES
    wv = A_HEADS * A_V
    return pl.pallas_call(
        _mla_prompt_kernel,
        out_shape=jax.ShapeDtypeStruct((batch * seq, wv), BF16),
        grid=(batch, nq),
        in_specs=[pl.BlockSpec((MLA_TILE, wa), lambda b, i: (b * nq + i, 0)),
                  pl.BlockSpec((seq, wa), lambda b, i: (b, 0)),
                  pl.BlockSpec((wv, seq), lambda b, i: (0, b))],
        out_specs=pl.BlockSpec((MLA_TILE, wv), lambda b, i: (b * nq + i, 0)),
        scratch_shapes=[pltpu.VMEM((A_HEADS, 1, MLA_TILE), F32), pltpu.VMEM((A_HEADS, 1, MLA_TILE), F32),
                        pltpu.VMEM((A_HEADS, A_V, MLA_TILE), F32)],
        compiler_params=_params("parallel", "arbitrary"),
        name="mla_prompt",
    )(qa, ka, vat)


def _fill_padded(kpad, vtpad, k_ref, vt_ref, reach):
    kpad[0:reach, :] = jnp.zeros((reach, kpad.shape[1]), kpad.dtype)
    kpad[reach:, :] = k_ref[...]
    vtpad[:, 0:reach] = jnp.zeros((vtpad.shape[0], reach), vtpad.dtype)
    vtpad[:, reach:] = vt_ref[...]


def _window_softmax(s, sink=None):
    m = jnp.max(s, axis=0, keepdims=True)
    if sink is not None:
        m = jnp.maximum(m, sink)
    p = jnp.exp2(s - m)
    l = jnp.sum(p, axis=0, keepdims=True)
    if sink is not None:
        l = l + jnp.exp2(sink - m)
    return p.astype(BF16), 1.0 / l


def _band_prompt_kernel(q_ref, k_ref, vt_ref, bias_ref, o_ref, kpad, vtpad):
    i = pl.program_id(1)

    @pl.when(i == 0)
    def _():
        _fill_padded(kpad, vtpad, k_ref, vt_ref, B_REACH)

    w = B_REACH + ATT_TILE
    start = pl.multiple_of(i * ATT_TILE, ATT_TILE)
    key = lax.broadcasted_iota(jnp.int32, (w, 2 * ATT_TILE), 0)
    in_seq = key >= B_REACH - i * ATT_TILE
    n_pairs = B_HEADS // 2

    def scores(pair):
        q2 = q_ref[2 * pair:2 * pair + 2].reshape(2 * ATT_TILE, LANES)
        return _dot_nt(kpad[pl.ds(start, w), pair * LANES:(pair + 1) * LANES], q2)

    pending = {pair: scores(pair) for pair in range(MXU_LOOKAHEAD)}
    outs = []
    for pair in range(n_pairs):
        if pair + MXU_LOOKAHEAD < n_pairs:
            pending[pair + MXU_LOOKAHEAD] = scores(pair + MXU_LOOKAHEAD)
        s = pending.pop(pair) + bias_ref[pair]
        p, inv_l = _window_softmax(jnp.where(in_seq, s, NEG_INF))
        o2 = _dot(vtpad[pair * LANES:(pair + 1) * LANES, pl.ds(start, w)], p) * inv_l
        outs.append(o2[0:HEAD_DIM, 0:ATT_TILE])
        outs.append(o2[HEAD_DIM:, ATT_TILE:])
    o_ref[...] = jnp.concatenate(outs, axis=0).T.astype(BF16)


def _band_prompt(qb, kb, vbt, bias_pairs, batch, seq):
    nq = seq // ATT_TILE
    return pl.pallas_call(
        _band_prompt_kernel,
        out_shape=jax.ShapeDtypeStruct((batch * seq, B_WIDTH), BF16),
        grid=(batch, nq),
        in_specs=[pl.BlockSpec((B_HEADS, ATT_TILE, LANES), lambda b, i: (0, b * nq + i, 0)),
                  pl.BlockSpec((seq, B_WIDTH), lambda b, i: (b, 0)),
                  pl.BlockSpec((B_WIDTH, seq), lambda b, i: (0, b)),
                  _const_spec(bias_pairs.shape)],
        out_specs=pl.BlockSpec((ATT_TILE, B_WIDTH), lambda b, i: (b * nq + i, 0)),
        scratch_shapes=[pltpu.VMEM((seq + B_REACH, B_WIDTH), BF16), pltpu.VMEM((B_WIDTH, seq + B_REACH), BF16)],
        compiler_params=_params("parallel", "arbitrary"),
        name="band_prompt",
    )(qb, kb, vbt, bias_pairs)


def _swa_prompt_kernel(sink_ref, q_ref, k_ref, vt_ref, o_ref, kpad, vtpad, *, layer):
    i = pl.program_id(1)

    @pl.when(i == 0)
    def _():
        _fill_padded(kpad, vtpad, k_ref, vt_ref, C_WINDOW)

    w = C_WINDOW + ATT_TILE
    group = C_HEADS // C_KV_HEADS
    start = pl.multiple_of(i * ATT_TILE, ATT_TILE)
    key = lax.broadcasted_iota(jnp.int32, (w, ATT_TILE), 0)
    qry = lax.broadcasted_iota(jnp.int32, (w, ATT_TILE), 1)
    lo = (qry // CHUNK) * CHUNK
    visible = (key >= lo) & (key < lo + C_WINDOW + CHUNK) & (key >= C_WINDOW - i * ATT_TILE)
    mask = jnp.concatenate([jnp.where(visible, 0.0, NEG_INF)] * group, axis=1)
    k = kpad[pl.ds(start, w), :]
    scores = [_dot_nt(k, q_ref[kv * group:(kv + 1) * group].reshape(group * ATT_TILE, LANES))
              for kv in range(C_KV_HEADS)]
    outs = []
    for kv in range(C_KV_HEADS):
        sink = jnp.concatenate([jnp.full((1, ATT_TILE), sink_ref[layer, kv * group + g] * LOG2E, F32)
                                for g in range(group)], axis=1)
        p, inv_l = _window_softmax(scores[kv] + mask, sink)
        og = _dot(vtpad[kv * HEAD_DIM:(kv + 1) * HEAD_DIM, pl.ds(start, w)], p) * inv_l
        outs.extend(og[:, g * ATT_TILE:(g + 1) * ATT_TILE] for g in range(group))
    o_ref[...] = jnp.concatenate(outs, axis=0).T.astype(BF16)


def _swa_prompt(sinks, layer, q, k, vt, batch, seq):
    nq = seq // ATT_TILE
    return pl.pallas_call(
        functools.partial(_swa_prompt_kernel, layer=layer),
        out_shape=jax.ShapeDtypeStruct((batch * seq, C_QWIDTH), BF16),
        grid=(batch, nq),
        in_specs=[pl.BlockSpec(memory_space=pltpu.SMEM),
                  pl.BlockSpec((C_HEADS, ATT_TILE, LANES), lambda b, i: (0, b * nq + i, 0)),
                  pl.BlockSpec((seq, C_KWIDTH), lambda b, i: (b, 0)),
                  pl.BlockSpec((C_KWIDTH, seq), lambda b, i: (0, b))],
        out_specs=pl.BlockSpec((ATT_TILE, C_QWIDTH), lambda b, i: (b * nq + i, 0)),
        scratch_shapes=[pltpu.VMEM((seq + C_WINDOW, C_KWIDTH), BF16), pltpu.VMEM((C_KWIDTH, seq + C_WINDOW), BF16)],
        compiler_params=_params("parallel", "arbitrary"),
        name="swa_prompt",
    )(sinks, q, k, vt)


def _softmax_pv2(s_c, s_n, v_c, v_n, sink=None):
    m = jnp.maximum(jnp.max(s_c, axis=-1, keepdims=True), jnp.max(s_n, axis=-1, keepdims=True))
    if sink is not None:
        m = jnp.maximum(m, sink)
    p_c = jnp.exp2(s_c - m)
    p_n = jnp.exp2(s_n - m)
    l = jnp.sum(p_c, axis=-1, keepdims=True) + jnp.sum(p_n, axis=-1, keepdims=True)
    if sink is not None:
        l = l + jnp.exp2(sink - m)
    return (_dot(p_c.astype(BF16), v_c) + _dot(p_n.astype(BF16), v_n)) / l


def _even_sample_kernel(qa_ref, ka_ref, va_ref, qb_ref, kb_ref, vb_ref,
                        cckv_ref, ckpe_ref, cbk_ref, cbv_ref, wukv_ref, place_ref, bias_ref,
                        oa_ref, ob_ref):
    t = qa_ref.shape[0]
    nb = cbk_ref.shape[0]
    kvc = _dot(cckv_ref[...].astype(BF16), wukv_ref[...])
    kpe = _dot(ckpe_ref[...].astype(BF16), place_ref[...])
    outs_a, outs_b = [], []
    for hd in range(A_HEADS):
        sl = slice(hd * LANES, (hd + 1) * LANES)
        vsl = slice(hd * A_V, (hd + 1) * A_V)
        k_c = (kvc[:, sl] + kpe).astype(BF16)
        v_c = kvc[:, A_HEADS * LANES + hd * A_V:A_HEADS * LANES + (hd + 1) * A_V].astype(BF16)
        q = qa_ref[:, sl]
        outs_a.append(_softmax_pv2(_dot_nt(q, k_c), _dot_nt(q, ka_ref[:, sl]), v_c, va_ref[:, vsl]))
    for hd in range(B_HEADS):
        sl = slice(hd * HEAD_DIM, (hd + 1) * HEAD_DIM)
        pair = slice((hd // 2) * LANES, (hd // 2 + 1) * LANES)
        q = qb_ref[hd]
        s_c =_dot_nt(q, cbk_ref[:, pair].astype(BF16)) + bias_ref[hd, 0:t, 0:nb]
        s_n = _dot_nt(q, kb_ref[:, pair]) + bias_ref[hd, 0:t, nb:nb + t]
        outs_b.append(_softmax_pv2(s_c, s_n, cbv_ref[:, sl].astype(BF16), vb_ref[:, sl]))
    oa_ref[...] = jnp.concatenate(outs_a, axis=-1).astype(BF16)
    ob_ref[...] = jnp.concatenate(outs_b, axis=-1).astype(BF16)


def _even_sample(layer, qa, ka, va, qb, kb, vb, c_ckv, c_kpe, c_bk, c_bv, wukv, place, bias, dec_batch, t):
    past = c_ckv.shape[2]
    nb = c_bk.shape[2]

    def new(w):
        return pl.BlockSpec((t, w), lambda b: (b, 0))

    def cache(n, w):
        return pl.BlockSpec((None, None, n, w), lambda b: (layer, b, 0, 0))

    return pl.pallas_call(
        _even_sample_kernel,
        out_shape=[jax.ShapeDtypeStruct((dec_batch * t, B_WIDTH), BF16)] * 2,
        grid=(dec_batch,),
        in_specs=[new(A_HEADS * LANES), new(A_HEADS * LANES), new(A_HEADS * A_V),
                  pl.BlockSpec((B_HEADS, t, LANES), lambda b: (0, b, 0)), new(B_WIDTH), new(B_WIDTH),
                  cache(past, A_KV_LORA), cache(past, A_ROPE), cache(nb, B_WIDTH), cache(nb, B_WIDTH),
                  _const_spec(wukv.shape), _const_spec(place.shape), _const_spec(bias.shape)],
        out_specs=[new(B_WIDTH), new(B_WIDTH)],
        compiler_params=_params("parallel"),
        name="even_sample",
    )(qa, ka, va, qb, kb, vb, c_ckv, c_kpe, c_bk, c_bv, wukv, place, bias)


def _odd_sample_kernel(sink_ref, q_ref, k_ref, v_ref, ck_ref, cv_ref, o_ref, *, layer):
    group = C_HEADS // C_KV_HEADS
    outs = []
    for hd in range(C_HEADS):
        kv = hd // group
        ksl = slice(kv * HEAD_DIM, (kv + 1) * HEAD_DIM)
        q = q_ref[:, hd * HEAD_DIM:(hd + 1) * HEAD_DIM]
        outs.append(_softmax_pv2(_dot_nt(q, ck_ref[:, ksl].astype(BF16)), _dot_nt(q, k_ref[:, ksl]),
                                 cv_ref[:, ksl].astype(BF16), v_ref[:, ksl], sink_ref[layer, hd] * LOG2E))
    o_ref[...] = jnp.concatenate(outs, axis=-1).astype(BF16)


def _odd_sample(sinks, layer, q, k, v, c_k, c_v, dec_batch, t):
    nc = c_k.shape[2]

    def new(w):
        return pl.BlockSpec((t, w), lambda b: (b, 0))

    def cache():
        return pl.BlockSpec((None, None, nc, C_KWIDTH), lambda b: (layer, b, 0, 0))

    return pl.pallas_call(
        functools.partial(_odd_sample_kernel, layer=layer),
        out_shape=jax.ShapeDtypeStruct((dec_batch * t, C_QWIDTH), BF16),
        grid=(dec_batch,),
        in_specs=[pl.BlockSpec(memory_space=pltpu.SMEM), new(C_QWIDTH), new(C_KWIDTH), new(C_KWIDTH),
                  cache(), cache()],
        out_specs=new(C_QWIDTH),
        compiler_params=_params("parallel"),
        name="odd_sample",
    )(sinks, q, k, v, c_k, c_v)


def _rope_parts(pos, n_rot):
    half = n_rot // 2
    inv = ROPE_THETA ** (-jnp.arange(half, dtype=F32) / half)
    ang = pos.astype(F32)[:, None] * inv[None, :]
    return jnp.cos(ang), jnp.sin(ang)


def _rope_tables(pos, n_rot, pre, width):
    cos, sin = _rope_parts(pos, n_rot)
    n = pos.shape[0]
    post = width - pre - n_rot

    def head(first, second, fill):
        return jnp.concatenate([jnp.full((n, pre), fill, F32), first, second, jnp.full((n, post), fill, F32)], axis=1)

    zero = jnp.zeros_like(sin)
    tabs = (head(cos, cos, 1.0), head(zero, sin, 0.0), head(-sin, zero, 0.0))
    return tuple(jnp.tile(t, (1, LANES // width)) for t in tabs)


def kernel(x_prompt, x_sample, cache_mla_ckv, cache_mla_kpe, cache_band_k, cache_band_v,
           cache_swa_k, cache_swa_v, norm_g, ffn_w_gate, ffn_w_up, ffn_w_down, even_w_in,
           mla_q_norm, mla_w_uq, mla_kv_norm, mla_w_ukv, band_rel_bias, even_w_out,
           odd_w_in, swa_sinks, odd_w_out):
    batch, seq, _ = x_prompt.shape
    dec_batch, t_new, _ = x_sample.shape
    depth = norm_g.shape[0]
    n_even = even_w_in.shape[0]
    n_odd = odd_w_in.shape[0]
    past = cache_mla_ckv.shape[2]
    n_p = batch * seq
    n_s = dec_batch * t_new
    assert seq % TOKEN_TILE == 0 and n_s % TOKEN_TILE == 0 and TOKEN_TILE % t_new == 0
    assert seq % MLA_TILE == 0 and min(B_REACH, seq) == TOKEN_TILE and t_new <= CHUNK
    assert cache_band_k.shape[2] == B_REACH and cache_swa_k.shape[2] == C_WINDOW and past >= B_REACH

    pos_p = jnp.arange(seq, dtype=jnp.int32)
    pos_s = past + (jnp.arange(TOKEN_TILE, dtype=jnp.int32) % t_new)
    mla_tabs = {True: _rope_tables(pos_p, A_ROPE, A_NOPE, LANES), False: _rope_tables(pos_s, A_ROPE, A_NOPE, LANES)}
    swa_tabs = {True: _rope_tables(pos_p, C_ROT, 0, HEAD_DIM), False: _rope_tables(pos_s, C_ROT, 0, HEAD_DIM)}

    wg = ffn_w_gate.astype(BF16)
    wu = ffn_w_up.astype(BF16)
    wd = ffn_w_down.astype(BF16)

    place = jnp.zeros((A_ROPE, LANES), F32).at[jnp.arange(A_ROPE), A_NOPE + jnp.arange(A_ROPE)].set(1.0).astype(BF16)
    c_bk = cache_band_k.reshape(n_even, dec_batch, B_REACH, B_WIDTH)
    c_bv = cache_band_v.reshape(n_even, dec_batch, B_REACH, B_WIDTH)
    c_sk = cache_swa_k.reshape(n_odd, dec_batch, C_WINDOW, C_KWIDTH)
    c_sv = cache_swa_v.reshape(n_odd, dec_batch, C_WINDOW, C_KWIDTH)

    c0 = A_Q_LORA + A_KV_LORA
    c1 = c0 + A_ROPE
    xs = {True: x_prompt.reshape(n_p, D_MODEL), False: x_sample.reshape(n_s, D_MODEL)}
    new = {True: [[] for _ in range(6)], False: [[] for _ in range(6)]}
    for l in range(depth):
        g = norm_g[l]
        i = l // 2
        if l % 2 == 0:
            w = even_w_in[i]
            zc = lambda n: jnp.zeros((D_MODEL, n), F32)
            win = jnp.concatenate([w[:, :c0], zc(A_NOPE), w[:, c0:c1], zc(LANES - A_NOPE - A_ROPE), w[:, c1:]],
                                  axis=1).astype(BF16)
            wuq = jnp.pad(mla_w_uq[i].reshape(A_Q_LORA, A_HEADS, A_QK),
                          ((0, 0), (0, 0), (0, LANES - A_QK))).reshape(A_Q_LORA, A_HEADS * LANES).astype(BF16)
            wkv = mla_w_ukv[i].reshape(A_KV_LORA, A_HEADS, A_NOPE + A_V)
            wk = jnp.pad(wkv[:, :, :A_NOPE], ((0, 0), (0, 0), (0, LANES - A_NOPE))).reshape(A_KV_LORA, A_HEADS * LANES)
            wv = wkv[:, :, A_NOPE:].reshape(A_KV_LORA, A_HEADS * A_V)
            wukv = jnp.concatenate([wk, wv], axis=1).astype(BF16)
            bias, bias_t = _band_bias(band_rel_bias[i])
            w_band = B_REACH + ATT_TILE
            bias_pairs = bias_t.reshape(B_HEADS // 2, 2, w_band, ATT_TILE).transpose(0, 2, 1, 3).reshape(
                B_HEADS // 2, w_band, 2 * ATT_TILE)
            wo = even_w_out[i].astype(BF16)
        else:
            win = odd_w_in[i].astype(BF16)
            wo = odd_w_out[i].astype(BF16)
        for prompt in (True, False):
            x = _ffn(xs[prompt], g[0:2], wg, wu, wd, l, 0)
            st = new[prompt]
            if l % 2 == 0:
                qa, ka, va, qb, kb, vb, ckv, kpe, kbf, vbf = _even_proj(
                    x, g[2:3], win, mla_q_norm[i][None], mla_kv_norm[i][None], wuq, wukv, mla_tabs[prompt],
                    prompt, batch, seq)
                if prompt:
                    mixes = [_mla_prompt(qa, ka, va, batch, seq), _band_prompt(qb, kb, vb, bias_pairs, batch, seq)]
                    lead = (batch, seq)
                else:
                    mixes = _even_sample(i, qa, ka, va, qb, kb, vb, cache_mla_ckv, cache_mla_kpe, c_bk, c_bv,
                                         wukv, place, bias, dec_batch, t_new)
                    lead = (dec_batch, t_new)
                st[0].append(ckv.reshape(*lead, A_KV_LORA))
                st[1].append(kpe.reshape(*lead, A_ROPE))
                st[2].append(kbf.reshape(lead[0], -1, B_HEADS, HEAD_DIM))
                st[3].append(vbf.reshape(lead[0], -1, B_HEADS, HEAD_DIM))
            else:
                q, k, v, kf, vf = _odd_proj(x, g[2:3], win, swa_tabs[prompt], prompt, batch, seq)
                if prompt:
                    mixes = [_swa_prompt(swa_sinks, i, q, k, v, batch, seq)]
                    lead = batch
                else:
                    mixes = [_odd_sample(swa_sinks, i, q, k, v, c_sk, c_sv, dec_batch, t_new)]
                    lead = dec_batch
                st[4].append(kf.reshape(lead, -1, C_KV_HEADS, HEAD_DIM))
                st[5].append(vf.reshape(lead, -1, C_KV_HEADS, HEAD_DIM))
            xs[prompt] = _mix_ffn(x, mixes, wo, g[3:4], g[4:6], wg, wu, wd, l, 1)

    y_prompt = xs[True].reshape(batch, seq, D_MODEL)
    y_sample = xs[False].reshape(dec_batch, t_new, D_MODEL)
    st_p = [jnp.stack(s, axis=0) for s in new[True]]
    st_s = [jnp.stack(s, axis=0) for s in new[False]]
    return (y_prompt, y_sample, *st_p, *st_s)
```

```python
import functools
import math

import jax
import jax.numpy as jnp
from jax import lax
from jax.experimental import pallas as pl
from jax.experimental.pallas import tpu as pltpu

F32 = jnp.float32
BF16 = jnp.bfloat16

D_MODEL = 1024
D_FF = 2816
CHUNK = 64
HEAD_DIM = 64
ROPE_THETA = 500000.0
NORM_EPS = 1e-6
NEG_INF = -1e30
A_HEADS = 8
A_Q_LORA = 256
A_KV_LORA = 128
A_NOPE = 64
A_ROPE = 32
A_QK = A_NOPE + A_ROPE
A_V = 64
A_SCALE = A_QK ** -0.5
B_HEADS = 8
B_REACH = 512
B_MAX_REL = 128
C_HEADS = 16
C_KV_HEADS = 2
C_WINDOW = 128
C_ROT = HEAD_DIM // 4
HEAD_SCALE = HEAD_DIM ** -0.5
LOG2E = math.log2(math.e)

LANES = 128
TOKEN_TILE = 512
FF_TILE = 256
PROJ_PARTS = 2
PROJ_ROWS = TOKEN_TILE // PROJ_PARTS
ATT_TILE = 128
BAND_TILE = 256
MLA_Q_TILE = 512
MLA_K_TILE = 256
MXU_LOOKAHEAD = 2
BAND_LOOKAHEAD = 1
ONES_ROWS = 16
VMEM_LIMIT = 56 * 1024 * 1024
B_WIDTH = B_HEADS * HEAD_DIM
C_QWIDTH = C_HEADS * HEAD_DIM
C_KWIDTH = C_KV_HEADS * HEAD_DIM


def _params(*sem):
    return pltpu.CompilerParams(dimension_semantics=sem, vmem_limit_bytes=VMEM_LIMIT)


def _const_spec(shape, index=None):
    index = index or (0,) * len(shape)
    return pl.BlockSpec(shape, lambda *_: index, pipeline_mode=pl.Buffered(1))


def _rms(x, g):
    ms = jnp.mean(x * x, axis=-1, keepdims=True)
    return x * lax.rsqrt(ms + NORM_EPS) * g


def _dot(a, b):
    return jnp.dot(a, b, preferred_element_type=F32)


def _dot_nt(a, b):
    return lax.dot_general(a, b, (((1,), (1,)), ((), ())), preferred_element_type=F32)


def _rope_group(x, c, s1, s2, half):
    return x * c + pltpu.roll(x, half, 1) * s1 + pltpu.roll(x, LANES - half, 1) * s2


def _lane_mask(lo):
    lane = lax.broadcasted_iota(jnp.int32, (1, LANES), 1)
    return (lane >= lo) & (lane < lo + HEAD_DIM)


def _swiglu_update(x, g_ref, wg_ref, wu_ref, wd_ref):
    xn = _rms(x, g_ref[0:1, :]).astype(BF16)
    acc = jnp.zeros(x.shape, F32)
    for c in range(D_FF // FF_TILE):
        sl = slice(c * FF_TILE, (c + 1) * FF_TILE)
        a = _dot(xn, wg_ref[:, sl])
        b = _dot(xn, wu_ref[:, sl])
        h = (a * jax.nn.sigmoid(a) * b).astype(BF16)
        acc = acc + _dot(h, wd_ref[sl, :])
    return x + 0.5 * _rms(acc, g_ref[1:2, :])


def _ffn_kernel(x_ref, g_ref, wg_ref, wu_ref, wd_ref, o_ref):
    o_ref[...] = _swiglu_update(x_ref[...], g_ref, wg_ref, wu_ref, wd_ref)


def _mix_ffn_kernel(*refs, n_mix):
    x_ref = refs[0]
    mix_refs = refs[1:1 + n_mix]
    wo_ref, gm_ref, g_ref, wg_ref, wu_ref, wd_ref, o_ref = refs[1 + n_mix:]
    y = None
    row = 0
    for m_ref in mix_refs:
        w = m_ref.shape[1]
        part = _dot(m_ref[...], wo_ref[row:row + w, :])
        y = part if y is None else y + part
        row += w
    x = x_ref[...] + _rms(y, gm_ref[...])
    o_ref[...] = _swiglu_update(x, g_ref, wg_ref, wu_ref, wd_ref)


def _ffn_weight_specs(layer, which):
    idx = (layer, which, 0, 0)
    return [_const_spec((None, None, D_MODEL, D_FF), idx), _const_spec((None, None, D_MODEL, D_FF), idx),
            _const_spec((None, None, D_FF, D_MODEL), idx)]


def _ffn(x, g, wg, wu, wd, layer, which):
    t = x.shape[0]
    tile = pl.BlockSpec((TOKEN_TILE, D_MODEL), lambda i: (i, 0))
    return pl.pallas_call(
        _ffn_kernel,
        out_shape=jax.ShapeDtypeStruct((t, D_MODEL), F32),
        grid=(t // TOKEN_TILE,),
        in_specs=[tile, _const_spec((2, D_MODEL))] + _ffn_weight_specs(layer, which),
        out_specs=tile,
        compiler_params=_params("parallel"),
        name="ffn",
    )(x, g, wg, wu, wd)


def _mix_ffn(x, mixes, wo, gm, g, wg, wu, wd, layer, which):
    t = x.shape[0]
    tile = pl.BlockSpec((TOKEN_TILE, D_MODEL), lambda i: (i, 0))
    mix_specs = [pl.BlockSpec((TOKEN_TILE, m.shape[1]), lambda i: (i, 0)) for m in mixes]
    return pl.pallas_call(
        functools.partial(_mix_ffn_kernel, n_mix=len(mixes)),
        out_shape=jax.ShapeDtypeStruct((t, D_MODEL), F32),
        grid=(t // TOKEN_TILE,),
        in_specs=[tile] + mix_specs + [_const_spec(wo.shape), _const_spec((1, D_MODEL)), _const_spec((2, D_MODEL))]
                 + _ffn_weight_specs(layer, which),
        out_specs=tile,
        compiler_params=_params("parallel"),
        name="mix_ffn",
    )(x, *mixes, wo, gm, g, wg, wu, wd)


def _even_proj_kernel(x_ref, g_ref, win_ref, qn_ref, kvn_ref, wuq_ref, wukv_ref,
                      c_ref, s1_ref, s2_ref,
                      qa_ref, ka_ref, va_ref, qb_ref, kb_ref, vb_ref,
                      ckv_ref, kpe_ref, kbf_ref, vbf_ref, *scratch, prompt, tiles_per_seq):
    half = A_ROPE // 2
    for part in range(PROJ_PARTS):
        rows = slice(part * PROJ_ROWS, (part + 1) * PROJ_ROWS)
        h = _rms(x_ref[rows, :], g_ref[...]).astype(BF16)
        proj = _dot(h, win_ref[...])
        cq = _rms(proj[:, 0:256], qn_ref[...]).astype(BF16)
        ckv = _rms(proj[:, 256:384], kvn_ref[...])
        c, s1, s2 = c_ref[rows, :], s1_ref[rows, :], s2_ref[rows, :]
        kpe = _rope_group(proj[:, 384:512], c, s1, s2, half)
        ckv_ref[rows, :] = ckv
        kpe_ref[rows, :] = pltpu.roll(kpe, LANES - A_NOPE, 1)[:, 0:A_ROPE]
        qa = _dot(cq, wuq_ref[...])
        kv = _dot(ckv.astype(BF16), wukv_ref[...])
        for hd in range(A_HEADS):
            sl = slice(hd * LANES, (hd + 1) * LANES)
            qa_ref[rows, sl] = (_rope_group(qa[:, sl], c, s1, s2, half) * (A_SCALE * LOG2E)).astype(BF16)
            ka_ref[rows, sl] = (kv[:, sl] + kpe).astype(BF16)
        for pair in range(B_HEADS // 2):
            grp = proj[:, 512 + pair * LANES:512 + (pair + 1) * LANES] * (HEAD_SCALE * LOG2E)
            for sub in range(2):
                qb_ref[2 * pair + sub, rows, :] = jnp.where(_lane_mask(sub * HEAD_DIM), grp, 0.0).astype(BF16)
        va = kv[:, A_HEADS * LANES:]
        kb = proj[:, 1024:1536]
        vb = proj[:, 1536:2048]
        kb_ref[rows, :] = kb.astype(BF16)
        if prompt:
            va_ref[:, rows] = va.T.astype(BF16)
            vb_ref[:, rows] = vb.T.astype(BF16)
            scratch[0][rows, :] = kb
            scratch[1][rows, :] = vb
        else:
            va_ref[rows, :] = va.astype(BF16)
            vb_ref[rows, :] = vb.astype(BF16)
            kbf_ref[rows, :] = kb
            vbf_ref[rows, :] = vb
    if prompt:
        @pl.when(pl.program_id(0) % tiles_per_seq == tiles_per_seq - 1)
        def _():
            kbf_ref[...] = scratch[0][...]
            vbf_ref[...] = scratch[1][...]


def _last_tile_index(tiles_per_seq):
    return lambda i: (jnp.maximum((i + 1) // tiles_per_seq - 1, 0), 0)


def _even_proj(x, g, win, qn, kvn, wuq, wukv, tabs, prompt, batch, seq):
    t = x.shape[0]
    tiles_per_seq = seq // TOKEN_TILE if prompt else 1

    def tile(w):
        return pl.BlockSpec((TOKEN_TILE, w), lambda i: (i, 0))

    def tile_t(w):
        return pl.BlockSpec((w, TOKEN_TILE), lambda i: (0, i))

    def sds(shape, dtype):
        return jax.ShapeDtypeStruct(shape, dtype)

    wa = A_HEADS * LANES
    if prompt:
        v_shapes = [sds((A_HEADS * A_V, t), BF16), sds((B_WIDTH, t), BF16)]
        v_specs = [tile_t(A_HEADS * A_V), tile_t(B_WIDTH)]
        keep_shape = sds((batch * TOKEN_TILE, B_WIDTH), F32)
        keep_spec = pl.BlockSpec((TOKEN_TILE, B_WIDTH), _last_tile_index(tiles_per_seq))
        tab_idx = lambda i: (i % tiles_per_seq, 0)
    else:
        v_shapes = [sds((t, A_HEADS * A_V), BF16), sds((t, B_WIDTH), BF16)]
        v_specs = [tile(A_HEADS * A_V), tile(B_WIDTH)]
        keep_shape = sds((t, B_WIDTH), F32)
        keep_spec = tile(B_WIDTH)
        tab_idx = lambda i: (0, 0)
    qb_spec = pl.BlockSpec((B_HEADS, TOKEN_TILE, LANES), lambda i: (0, i, 0))
    out_shape = [sds((t, wa), BF16), sds((t, wa), BF16), v_shapes[0], sds((B_HEADS, t, LANES), BF16),
                 sds((t, B_WIDTH), BF16), v_shapes[1], sds((t, A_KV_LORA), F32), sds((t, A_ROPE), F32),
                 keep_shape, keep_shape]
    out_specs = [tile(wa), tile(wa), v_specs[0], qb_spec, tile(B_WIDTH), v_specs[1],
                 tile(A_KV_LORA), tile(A_ROPE), keep_spec, keep_spec]
    return pl.pallas_call(
        functools.partial(_even_proj_kernel, prompt=prompt, tiles_per_seq=tiles_per_seq),
        out_shape=out_shape,
        grid=(t // TOKEN_TILE,),
        in_specs=[tile(D_MODEL), _const_spec((1, D_MODEL)), _const_spec(win.shape),
                  _const_spec((1, A_Q_LORA)), _const_spec((1, A_KV_LORA)),
                  _const_spec(wuq.shape), _const_spec(wukv.shape)]
                 + [pl.BlockSpec((TOKEN_TILE, LANES), tab_idx)] * 3,
        out_specs=out_specs,
        scratch_shapes=[pltpu.VMEM((TOKEN_TILE, B_WIDTH), F32)] * 2 if prompt else [],
        compiler_params=_params("arbitrary"),
        name="even_proj",
    )(x, g, win, qn, kvn, wuq, wukv, *tabs)


def _odd_proj_kernel(x_ref, g_ref, win_ref, c_ref, s1_ref, s2_ref,
                     q_ref, k_ref, v_ref, kf_ref, vf_ref, *scratch, prompt, tiles_per_seq):
    half = C_ROT // 2
    group = C_HEADS // C_KV_HEADS
    for part in range(PROJ_PARTS):
        rows = slice(part * PROJ_ROWS, (part + 1) * PROJ_ROWS)
        h = _rms(x_ref[rows, :], g_ref[...]).astype(BF16)
        proj = _dot(h, win_ref[...])
        c, s1, s2 = c_ref[rows, :], s1_ref[rows, :], s2_ref[rows, :]
        k = _rope_group(proj[:, C_QWIDTH:C_QWIDTH + LANES], c, s1, s2, half)
        v = proj[:, C_QWIDTH + LANES:C_QWIDTH + 2 * LANES]
        k_ref[rows, :] = k.astype(BF16)
        if prompt:
            for pair in range(C_HEADS // 2):
                grp = _rope_group(proj[:, pair * LANES:(pair + 1) * LANES], c, s1, s2, half) * (HEAD_SCALE * LOG2E)
                swapped = pltpu.roll(grp, HEAD_DIM, 1)
                for sub in range(2):
                    hd = 2 * pair + sub
                    kv = hd // group
                    src = grp if sub == kv else swapped
                    q_ref[hd, rows, :] = jnp.where(_lane_mask(kv * HEAD_DIM), src, 0.0).astype(BF16)
            v_ref[:, rows] = v.T.astype(BF16)
            if part == PROJ_PARTS - 1:
                scratch[0][...] = k[PROJ_ROWS - C_WINDOW:, :]
                scratch[1][...] = v[PROJ_ROWS - C_WINDOW:, :]
        else:
            for grp in range(C_QWIDTH // LANES):
                sl = slice(grp * LANES, (grp + 1) * LANES)
                q_ref[rows, sl] = (_rope_group(proj[:, sl], c, s1, s2, half) * (HEAD_SCALE * LOG2E)).astype(BF16)
            v_ref[rows, :] = v.astype(BF16)
            kf_ref[rows, :] = k
            vf_ref[rows, :] = v
    if prompt:
        @pl.when(pl.program_id(0) % tiles_per_seq == tiles_per_seq - 1)
        def _():
            kf_ref[...] = scratch[0][...]
            vf_ref[...] = scratch[1][...]


def _odd_proj(x, g, win, tabs, prompt, batch, seq):
    t = x.shape[0]
    tiles_per_seq = seq // TOKEN_TILE if prompt else 1

    def tile(w):
        return pl.BlockSpec((TOKEN_TILE, w), lambda i: (i, 0))

    def sds(shape, dtype):
        return jax.ShapeDtypeStruct(shape, dtype)

    if prompt:
        q_shape = sds((C_HEADS, t, LANES), BF16)
        q_spec = pl.BlockSpec((C_HEADS, TOKEN_TILE, LANES), lambda i: (0, i, 0))
        v_shape, v_spec = sds((LANES, t), BF16), pl.BlockSpec((LANES, TOKEN_TILE), lambda i: (0, i))
        keep_shape = sds((batch * C_WINDOW, LANES), F32)
        keep_spec = pl.BlockSpec((C_WINDOW, LANES), _last_tile_index(tiles_per_seq))
        tab_idx = lambda i: (i % tiles_per_seq, 0)
    else:
        q_shape, q_spec = sds((t, C_QWIDTH), BF16), tile(C_QWIDTH)
        v_shape, v_spec = sds((t, LANES), BF16), tile(LANES)
        keep_shape, keep_spec = sds((t, LANES), F32), tile(LANES)
        tab_idx = lambda i: (0, 0)
    return pl.pallas_call(
        functools.partial(_odd_proj_kernel, prompt=prompt, tiles_per_seq=tiles_per_seq),
        out_shape=[q_shape, sds((t, LANES), BF16), v_shape, keep_shape, keep_shape],
        grid=(t // TOKEN_TILE,),
        in_specs=[tile(D_MODEL), _const_spec((1, D_MODEL)), _const_spec(win.shape)]
                 + [pl.BlockSpec((TOKEN_TILE, LANES), tab_idx)] * 3,
        out_specs=[q_spec, tile(LANES), v_spec, keep_spec, keep_spec],
        scratch_shapes=[pltpu.VMEM((C_WINDOW, LANES), F32)] * 2 if prompt else [],
        compiler_params=_params("arbitrary"),
        name="odd_proj",
    )(x, g, win, *tabs)


def _band_bias_kernel(tab_ref, o_ref, ot_ref):
    hd = pl.program_id(0)
    w = B_REACH + BAND_TILE
    row = lax.broadcasted_iota(jnp.int32, (BAND_TILE, w), 0)
    col = lax.broadcasted_iota(jnp.int32, (BAND_TILE, w), 1)
    rel = jnp.clip(row - col + B_REACH, -B_MAX_REL, B_MAX_REL) + B_MAX_REL
    lo = (row // CHUNK) * CHUNK
    visible = (col >= lo) & (col < lo + B_REACH + CHUNK)

    def body(r, acc):
        return jnp.where(rel == r, tab_ref[hd, r], acc)

    bias = lax.fori_loop(0, 2 * B_MAX_REL + 1, body, jnp.zeros((BAND_TILE, w), F32))
    bias = jnp.where(visible, bias * LOG2E, NEG_INF)
    o_ref[...] = bias
    ot_ref[...] = bias.T


def _band_bias(table):
    w = B_REACH + BAND_TILE
    return pl.pallas_call(
        _band_bias_kernel,
        out_shape=[jax.ShapeDtypeStruct((B_HEADS, BAND_TILE, w), F32),
                   jax.ShapeDtypeStruct((B_HEADS, w, BAND_TILE), F32)],
        grid=(B_HEADS,),
        in_specs=[pl.BlockSpec(memory_space=pltpu.SMEM)],
        out_specs=[pl.BlockSpec((None, BAND_TILE, w), lambda h: (h, 0, 0)),
                   pl.BlockSpec((None, w, BAND_TILE), lambda h: (h, 0, 0))],
        compiler_params=_params("arbitrary"),
        name="band_bias",
    )(table)


def _ones_rows(n):
    return jnp.ones((ONES_ROWS, n), BF16)


def _mla_prompt_kernel(q_ref, k_ref, vt_ref, o_ref, m_ref, l_ref, acc_ref):
    i = pl.program_id(1)
    tq, tk = MLA_Q_TILE, MLA_K_TILE
    m_ref[...] = jnp.full(m_ref.shape, NEG_INF, F32)
    l_ref[...] = jnp.zeros(l_ref.shape, F32)
    acc_ref[...] = jnp.zeros(acc_ref.shape, F32)
    kc = lax.broadcasted_iota(jnp.int32, (tk, tq), 0) // CHUNK
    qc = lax.broadcasted_iota(jnp.int32, (tk, tq), 1) // CHUNK

    def block(j, diag):
        ks = pl.multiple_of(j * tk, tk)
        m_all, l_all, acc_all = m_ref[...], l_ref[...], acc_ref[...]
        ones = _ones_rows(tk)

        def scores(hd):
            sl = slice(hd * LANES, (hd + 1) * LANES)
            return _dot_nt(k_ref[pl.ds(ks, tk), sl], q_ref[:, sl])

        pending = {hd: scores(hd) for hd in range(MXU_LOOKAHEAD)}
        new = []
        for hd in range(A_HEADS):
            if hd + MXU_LOOKAHEAD < A_HEADS:
                pending[hd + MXU_LOOKAHEAD] = scores(hd + MXU_LOOKAHEAD)
            s = pending.pop(hd)
            if diag is not None:
                s = jnp.where(kc + diag * (tk // CHUNK) <= qc, s, NEG_INF)
            m_new = jnp.maximum(m_all[hd], jnp.max(s, axis=0, keepdims=True))
            alpha = jnp.exp2(m_all[hd] - m_new)
            p = jnp.exp2(s - m_new).astype(BF16)
            vt = jnp.concatenate([vt_ref[hd * A_V:(hd + 1) * A_V, pl.ds(ks, tk)], ones], axis=0)
            pv = _dot(vt, p)
            new.append((m_new, alpha * l_all[hd] + pv[A_V:A_V + 1], alpha * acc_all[hd] + pv[0:A_V]))
        for hd, (m_new, l_new, acc_new) in enumerate(new):
            m_ref[hd] = m_new
            l_ref[hd] = l_new
            acc_ref[hd] = acc_new

    def body(j, carry):
        block(j, None)
        return carry

    per_q = tq // tk
    lax.fori_loop(0, i * per_q, body, 0)
    for d in range(per_q):
        block(i * per_q + d, d)
    outs = [acc_ref[hd] / l_ref[hd] for hd in range(A_HEADS)]
    o_ref[...] = jnp.concatenate(outs, axis=0).T.astype(BF16)


def _mla_prompt(qa, ka, vat, batch, seq):
    nq = seq // MLA_Q_TILE
    wa = A_HEADS * LANES
    wv = A_HEADS * A_V
    return pl.pallas_call(
        _mla_prompt_kernel,
        out_shape=jax.ShapeDtypeStruct((batch * seq, wv), BF16),
        grid=(batch, nq),
        in_specs=[pl.BlockSpec((MLA_Q_TILE, wa), lambda b, i: (b * nq + i, 0)),
                  pl.BlockSpec((seq, wa), lambda b, i: (b, 0)),
                  pl.BlockSpec((wv, seq), lambda b, i: (0, b))],
        out_specs=pl.BlockSpec((MLA_Q_TILE, wv), lambda b, i: (b * nq + i, 0)),
        scratch_shapes=[pltpu.VMEM((A_HEADS, 1, MLA_Q_TILE), F32), pltpu.VMEM((A_HEADS, 1, MLA_Q_TILE), F32),
                        pltpu.VMEM((A_HEADS, A_V, MLA_Q_TILE), F32)],
        compiler_params=_params("parallel", "arbitrary"),
        name="mla_prompt",
    )(qa, ka, vat)


def _fill_padded(kpad, vtpad, k_ref, vt_ref, reach):
    kpad[0:reach, :] = jnp.zeros((reach, kpad.shape[1]), kpad.dtype)
    kpad[reach:, :] = k_ref[...]
    vtpad[:, 0:reach] = jnp.zeros((vtpad.shape[0], reach), vtpad.dtype)
    vtpad[:, reach:] = vt_ref[...]


def _window_attend(s, vt, sink=None):
    m = jnp.max(s, axis=0, keepdims=True)
    if sink is not None:
        m = jnp.maximum(m, sink)
    p = jnp.exp2(s - m).astype(BF16)
    d = vt.shape[0]
    pv = _dot(jnp.concatenate([vt, _ones_rows(vt.shape[1])], axis=0), p)
    l = pv[d:d + 1]
    if sink is not None:
        l = l + jnp.exp2(sink - m)
    return pv[0:d] * (1.0 / l)


def _band_prompt_kernel(q_ref, k_ref, vt_ref, bias_ref, o_ref, kpad, vtpad):
    i = pl.program_id(1)
    tq = BAND_TILE

    @pl.when(i == 0)
    def _():
        _fill_padded(kpad, vtpad, k_ref, vt_ref, B_REACH)

    w = B_REACH + tq
    start = pl.multiple_of(i * tq, tq)
    n_pairs = B_HEADS // 2

    def attend(mask_start):
        if mask_start:
            in_seq = lax.broadcasted_iota(jnp.int32, (w, 2 * tq), 0) >= B_REACH - i * tq

        def scores(pair):
            q2 = q_ref[2 * pair:2 * pair + 2].reshape(2 * tq, LANES)
            return _dot_nt(kpad[pl.ds(start, w), pair * LANES:(pair + 1) * LANES], q2)

        pending = {pair: scores(pair) for pair in range(BAND_LOOKAHEAD)}
        outs = []
        for pair in range(n_pairs):
            if pair + BAND_LOOKAHEAD < n_pairs:
                pending[pair + BAND_LOOKAHEAD] = scores(pair + BAND_LOOKAHEAD)
            s = pending.pop(pair) + bias_ref[pair]
            if mask_start:
                s = jnp.where(in_seq, s, NEG_INF)
            o2 = _window_attend(s, vtpad[pair * LANES:(pair + 1) * LANES, pl.ds(start, w)])
            outs.append(o2[0:HEAD_DIM, 0:tq])
            outs.append(o2[HEAD_DIM:, tq:])
        o_ref[...] = jnp.concatenate(outs, axis=0).T.astype(BF16)

    pl.when(i * tq < B_REACH)(functools.partial(attend, True))
    pl.when(i * tq >= B_REACH)(functools.partial(attend, False))


def _band_prompt(qb, kb, vbt, bias_pairs, batch, seq):
    nq = seq // BAND_TILE
    return pl.pallas_call(
        _band_prompt_kernel,
        out_shape=jax.ShapeDtypeStruct((batch * seq, B_WIDTH), BF16),
        grid=(batch, nq),
        in_specs=[pl.BlockSpec((B_HEADS, BAND_TILE, LANES), lambda b, i: (0, b * nq + i, 0)),
                  pl.BlockSpec((seq, B_WIDTH), lambda b, i: (b, 0)),
                  pl.BlockSpec((B_WIDTH, seq), lambda b, i: (0, b)),
                  _const_spec(bias_pairs.shape)],
        out_specs=pl.BlockSpec((BAND_TILE, B_WIDTH), lambda b, i: (b * nq + i, 0)),
        scratch_shapes=[pltpu.VMEM((seq + B_REACH, B_WIDTH), BF16), pltpu.VMEM((B_WIDTH, seq + B_REACH), BF16)],
        compiler_params=_params("parallel", "arbitrary"),
        name="band_prompt",
    )(qb, kb, vbt, bias_pairs)


def _swa_prompt_kernel(sink_ref, q_ref, k_ref, vt_ref, o_ref, kpad, vtpad, *, layer):
    i = pl.program_id(1)

    @pl.when(i == 0)
    def _():
        _fill_padded(kpad, vtpad, k_ref, vt_ref, C_WINDOW)

    w = C_WINDOW + ATT_TILE
    group = C_HEADS // C_KV_HEADS
    start = pl.multiple_of(i * ATT_TILE, ATT_TILE)
    key = lax.broadcasted_iota(jnp.int32, (w, ATT_TILE), 0)
    qry = lax.broadcasted_iota(jnp.int32, (w, ATT_TILE), 1)
    lo = (qry // CHUNK) * CHUNK
    visible = (key >= lo) & (key < lo + C_WINDOW + CHUNK) & (key >= C_WINDOW - i * ATT_TILE)
    mask = jnp.concatenate([jnp.where(visible, 0.0, NEG_INF)] * group, axis=1)
    k = kpad[pl.ds(start, w), :]
    scores = [_dot_nt(k, q_ref[kv * group:(kv + 1) * group].reshape(group * ATT_TILE, LANES))
              for kv in range(C_KV_HEADS)]
    outs = []
    for kv in range(C_KV_HEADS):
        sink = jnp.concatenate([jnp.full((1, ATT_TILE), sink_ref[layer, kv * group + g] * LOG2E, F32)
                                for g in range(group)], axis=1)
        og = _window_attend(scores[kv] + mask, vtpad[kv * HEAD_DIM:(kv + 1) * HEAD_DIM, pl.ds(start, w)], sink)
        outs.extend(og[:, g * ATT_TILE:(g + 1) * ATT_TILE] for g in range(group))
    o_ref[...] = jnp.concatenate(outs, axis=0).T.astype(BF16)


def _swa_prompt(sinks, layer, q, k, vt, batch, seq):
    nq = seq // ATT_TILE
    return pl.pallas_call(
        functools.partial(_swa_prompt_kernel, layer=layer),
        out_shape=jax.ShapeDtypeStruct((batch * seq, C_QWIDTH), BF16),
        grid=(batch, nq),
        in_specs=[pl.BlockSpec(memory_space=pltpu.SMEM),
                  pl.BlockSpec((C_HEADS, ATT_TILE, LANES), lambda b, i: (0, b * nq + i, 0)),
                  pl.BlockSpec((seq, C_KWIDTH), lambda b, i: (b, 0)),
                  pl.BlockSpec((C_KWIDTH, seq), lambda b, i: (0, b))],
        out_specs=pl.BlockSpec((ATT_TILE, C_QWIDTH), lambda b, i: (b * nq + i, 0)),
        scratch_shapes=[pltpu.VMEM((seq + C_WINDOW, C_KWIDTH), BF16), pltpu.VMEM((C_KWIDTH, seq + C_WINDOW), BF16)],
        compiler_params=_params("parallel", "arbitrary"),
        name="swa_prompt",
    )(sinks, q, k, vt)


def _softmax_pv2(s_c, s_n, v_c, v_n, sink=None):
    m = jnp.maximum(jnp.max(s_c, axis=-1, keepdims=True), jnp.max(s_n, axis=-1, keepdims=True))
    if sink is not None:
        m = jnp.maximum(m, sink)
    p_c = jnp.exp2(s_c - m)
    p_n = jnp.exp2(s_n - m)
    l = jnp.sum(p_c, axis=-1, keepdims=True) + jnp.sum(p_n, axis=-1, keepdims=True)
    if sink is not None:
        l = l + jnp.exp2(sink - m)
    return (_dot(p_c.astype(BF16), v_c) + _dot(p_n.astype(BF16), v_n)) / l


def _even_sample_kernel(qa_ref, ka_ref, va_ref, qb_ref, kb_ref, vb_ref,
                        cckv_ref, ckpe_ref, cbk_ref, cbv_ref, wukv_ref, place_ref, bias_ref,
                        oa_ref, ob_ref):
    t = qa_ref.shape[0]
    nb = cbk_ref.shape[0]
    kvc = _dot(cckv_ref[...].astype(BF16), wukv_ref[...])
    kpe = _dot(ckpe_ref[...].astype(BF16), place_ref[...])
    scores_a, scores_b = [], []
    for hd in range(A_HEADS):
        sl = slice(hd * LANES, (hd + 1) * LANES)
        k_c = (kvc[:, sl] + kpe).astype(BF16)
        q = qa_ref[:, sl]
        scores_a.append((_dot_nt(q, k_c), _dot_nt(q, ka_ref[:, sl])))
    for hd in range(B_HEADS):
        pair = slice((hd // 2) * LANES, (hd // 2 + 1) * LANES)
        q = qb_ref[hd]
        scores_b.append((_dot_nt(q, cbk_ref[:, pair].astype(BF16)) + bias_ref[hd, 0:t, 0:nb],
                         _dot_nt(q, kb_ref[:, pair]) + bias_ref[hd, 0:t, nb:nb + t]))
    outs_a, outs_b = [], []
    for hd in range(A_HEADS):
        vsl = slice(hd * A_V, (hd + 1) * A_V)
        v_c = kvc[:, A_HEADS * LANES + hd * A_V:A_HEADS * LANES + (hd + 1) * A_V].astype(BF16)
        outs_a.append(_softmax_pv2(*scores_a[hd], v_c, va_ref[:, vsl]))
    for hd in range(B_HEADS):
        sl = slice(hd * HEAD_DIM, (hd + 1) * HEAD_DIM)
        outs_b.append(_softmax_pv2(*scores_b[hd], cbv_ref[:, sl].astype(BF16), vb_ref[:, sl]))
    oa_ref[...] = jnp.concatenate(outs_a, axis=-1).astype(BF16)
    ob_ref[...] = jnp.concatenate(outs_b, axis=-1).astype(BF16)


def _even_sample(layer, qa, ka, va, qb, kb, vb, c_ckv, c_kpe, c_bk, c_bv, wukv, place, bias, dec_batch, t):
    past = c_ckv.shape[2]
    nb = c_bk.shape[2]

    def new(w):
        return pl.BlockSpec((t, w), lambda b: (b, 0))

    def cache(n, w):
        return pl.BlockSpec((None, None, n, w), lambda b: (layer, b, 0, 0))

    return pl.pallas_call(
        _even_sample_kernel,
        out_shape=[jax.ShapeDtypeStruct((dec_batch * t, B_WIDTH), BF16)] * 2,
        grid=(dec_batch,),
        in_specs=[new(A_HEADS * LANES), new(A_HEADS * LANES), new(A_HEADS * A_V),
                  pl.BlockSpec((B_HEADS, t, LANES), lambda b: (0, b, 0)), new(B_WIDTH), new(B_WIDTH),
                  cache(past, A_KV_LORA), cache(past, A_ROPE), cache(nb, B_WIDTH), cache(nb, B_WIDTH),
                  _const_spec(wukv.shape), _const_spec(place.shape), _const_spec(bias.shape)],
        out_specs=[new(B_WIDTH), new(B_WIDTH)],
        compiler_params=_params("parallel"),
        name="even_sample",
    )(qa, ka, va, qb, kb, vb, c_ckv, c_kpe, c_bk, c_bv, wukv, place, bias)


def _odd_sample_kernel(sink_ref, q_ref, k_ref, v_ref, ck_ref, cv_ref, o_ref, *, layer):
    group = C_HEADS // C_KV_HEADS
    scores = []
    for hd in range(C_HEADS):
        ksl = slice((hd // group) * HEAD_DIM, (hd // group + 1) * HEAD_DIM)
        q = q_ref[:, hd * HEAD_DIM:(hd + 1) * HEAD_DIM]
        scores.append((_dot_nt(q, ck_ref[:, ksl].astype(BF16)), _dot_nt(q, k_ref[:, ksl])))
    outs = []
    for hd in range(C_HEADS):
        ksl = slice((hd // group) * HEAD_DIM, (hd // group + 1) * HEAD_DIM)
        outs.append(_softmax_pv2(*scores[hd], cv_ref[:, ksl].astype(BF16), v_ref[:, ksl],
                                 sink_ref[layer, hd] * LOG2E))
    o_ref[...] = jnp.concatenate(outs, axis=-1).astype(BF16)


def _odd_sample(sinks, layer, q, k, v, c_k, c_v, dec_batch, t):
    nc = c_k.shape[2]

    def new(w):
        return pl.BlockSpec((t, w), lambda b: (b, 0))

    def cache():
        return pl.BlockSpec((None, None, nc, C_KWIDTH), lambda b: (layer, b, 0, 0))

    return pl.pallas_call(
        functools.partial(_odd_sample_kernel, layer=layer),
        out_shape=jax.ShapeDtypeStruct((dec_batch * t, C_QWIDTH), BF16),
        grid=(dec_batch,),
        in_specs=[pl.BlockSpec(memory_space=pltpu.SMEM), new(C_QWIDTH), new(C_KWIDTH), new(C_KWIDTH),
                  cache(), cache()],
        out_specs=new(C_QWIDTH),
        compiler_params=_params("parallel"),
        name="odd_sample",
    )(sinks, q, k, v, c_k, c_v)


def _rope_parts(pos, n_rot):
    half = n_rot // 2
    inv = ROPE_THETA ** (-jnp.arange(half, dtype=F32) / half)
    ang = pos.astype(F32)[:, None] * inv[None, :]
    return jnp.cos(ang), jnp.sin(ang)


def _rope_tables(pos, n_rot, pre, width):
    cos, sin = _rope_parts(pos, n_rot)
    n = pos.shape[0]
    post = width - pre - n_rot

    def head(first, second, fill):
        return jnp.concatenate([jnp.full((n, pre), fill, F32), first, second, jnp.full((n, post), fill, F32)], axis=1)

    zero = jnp.zeros_like(sin)
    tabs = (head(cos, cos, 1.0), head(zero, sin, 0.0), head(-sin, zero, 0.0))
    return tuple(jnp.tile(t, (1, LANES // width)) for t in tabs)


def kernel(x_prompt, x_sample, cache_mla_ckv, cache_mla_kpe, cache_band_k, cache_band_v,
           cache_swa_k, cache_swa_v, norm_g, ffn_w_gate, ffn_w_up, ffn_w_down, even_w_in,
           mla_q_norm, mla_w_uq, mla_kv_norm, mla_w_ukv, band_rel_bias, even_w_out,
           odd_w_in, swa_sinks, odd_w_out):
    batch, seq, _ = x_prompt.shape
    dec_batch, t_new, _ = x_sample.shape
    depth = norm_g.shape[0]
    n_even = even_w_in.shape[0]
    n_odd = odd_w_in.shape[0]
    past = cache_mla_ckv.shape[2]
    n_p = batch * seq
    n_s = dec_batch * t_new
    assert seq % TOKEN_TILE == 0 and n_s % TOKEN_TILE == 0 and TOKEN_TILE % t_new == 0
    assert seq % MLA_Q_TILE == 0 and seq % BAND_TILE == 0 and min(B_REACH, seq) == TOKEN_TILE and t_new <= CHUNK
    assert cache_band_k.shape[2] == B_REACH and cache_swa_k.shape[2] == C_WINDOW and past >= B_REACH

    pos_p = jnp.arange(seq, dtype=jnp.int32)
    pos_s = past + (jnp.arange(TOKEN_TILE, dtype=jnp.int32) % t_new)
    mla_tabs = {True: _rope_tables(pos_p, A_ROPE, A_NOPE, LANES), False: _rope_tables(pos_s, A_ROPE, A_NOPE, LANES)}
    swa_tabs = {True: _rope_tables(pos_p, C_ROT, 0, HEAD_DIM), False: _rope_tables(pos_s, C_ROT, 0, HEAD_DIM)}

    wg = ffn_w_gate.astype(BF16)
    wu = ffn_w_up.astype(BF16)
    wd = ffn_w_down.astype(BF16)

    place = jnp.zeros((A_ROPE, LANES), F32).at[jnp.arange(A_ROPE), A_NOPE + jnp.arange(A_ROPE)].set(1.0).astype(BF16)
    c_bk = cache_band_k.reshape(n_even, dec_batch, B_REACH, B_WIDTH)
    c_bv = cache_band_v.reshape(n_even, dec_batch, B_REACH, B_WIDTH)
    c_sk = cache_swa_k.reshape(n_odd, dec_batch, C_WINDOW, C_KWIDTH)
    c_sv = cache_swa_v.reshape(n_odd, dec_batch, C_WINDOW, C_KWIDTH)

    c0 = A_Q_LORA + A_KV_LORA
    c1 = c0 + A_ROPE
    xs = {True: x_prompt.reshape(n_p, D_MODEL), False: x_sample.reshape(n_s, D_MODEL)}
    new = {True: [[] for _ in range(6)], False: [[] for _ in range(6)]}
    for l in range(depth):
        g = norm_g[l]
        i = l // 2
        if l % 2 == 0:
            w = even_w_in[i]
            zc = lambda n: jnp.zeros((D_MODEL, n), F32)
            win = jnp.concatenate([w[:, :c0], zc(A_NOPE), w[:, c0:c1], zc(LANES - A_NOPE - A_ROPE), w[:, c1:]],
                                  axis=1).astype(BF16)
            wuq = jnp.pad(mla_w_uq[i].reshape(A_Q_LORA, A_HEADS, A_QK),
                          ((0, 0), (0, 0), (0, LANES - A_QK))).reshape(A_Q_LORA, A_HEADS * LANES).astype(BF16)
            wkv = mla_w_ukv[i].reshape(A_KV_LORA, A_HEADS, A_NOPE + A_V)
            wk = jnp.pad(wkv[:, :, :A_NOPE], ((0, 0), (0, 0), (0, LANES - A_NOPE))).reshape(A_KV_LORA, A_HEADS * LANES)
            wv = wkv[:, :, A_NOPE:].reshape(A_KV_LORA, A_HEADS * A_V)
            wukv = jnp.concatenate([wk, wv], axis=1).astype(BF16)
            bias, bias_t = _band_bias(band_rel_bias[i])
            w_band = B_REACH + BAND_TILE
            bias_pairs = bias_t.reshape(B_HEADS // 2, 2, w_band, BAND_TILE).transpose(0, 2, 1, 3).reshape(
                B_HEADS // 2, w_band, 2 * BAND_TILE)
            wo = even_w_out[i].astype(BF16)
        else:
            win = odd_w_in[i].astype(BF16)
            wo = odd_w_out[i].astype(BF16)
        for prompt in (True, False):
            x = _ffn(xs[prompt], g[0:2], wg, wu, wd, l, 0)
            st = new[prompt]
            if l % 2 == 0:
                qa, ka, va, qb, kb, vb, ckv, kpe, kbf, vbf = _even_proj(
                    x, g[2:3], win, mla_q_norm[i][None], mla_kv_norm[i][None], wuq, wukv, mla_tabs[prompt],
                    prompt, batch, seq)
                if prompt:
                    mixes = [_mla_prompt(qa, ka, va, batch, seq), _band_prompt(qb, kb, vb, bias_pairs, batch, seq)]
                    lead = (batch, seq)
                else:
                    mixes = _even_sample(i, qa, ka, va, qb, kb, vb, cache_mla_ckv, cache_mla_kpe, c_bk, c_bv,
                                         wukv, place, bias, dec_batch, t_new)
                    lead = (dec_batch, t_new)
                st[0].append(ckv.reshape(*lead, A_KV_LORA))
                st[1].append(kpe.reshape(*lead, A_ROPE))
                st[2].append(kbf.reshape(lead[0], -1, B_HEADS, HEAD_DIM))
                st[3].append(vbf.reshape(lead[0], -1, B_HEADS, HEAD_DIM))
            else:
                q, k, v, kf, vf = _odd_proj(x, g[2:3], win, swa_tabs[prompt], prompt, batch, seq)
                if prompt:
                    mixes = [_swa_prompt(swa_sinks, i, q, k, v, batch, seq)]
                    lead = batch
                else:
                    mixes = [_odd_sample(swa_sinks, i, q, k, v, c_sk, c_sv, dec_batch, t_new)]
                    lead = dec_batch
                st[4].append(kf.reshape(lead, -1, C_KV_HEADS, HEAD_DIM))
                st[5].append(vf.reshape(lead, -1, C_KV_HEADS, HEAD_DIM))
            xs[prompt] = _mix_ffn(x, mixes, wo, g[3:4], g[4:6], wg, wu, wd, l, 1)

    y_prompt = xs[True].reshape(batch, seq, D_MODEL)
    y_sample = xs[False].reshape(dec_batch, t_new, D_MODEL)
    st_p = [jnp.stack(s, axis=0) for s in new[True]]
    st_s = [jnp.stack(s, axis=0) for s in new[False]]
    return (y_prompt, y_sample, *st_p, *st_s)
```

```python
import functools
import math

import jax
import jax.numpy as jnp
from jax import lax
from jax.experimental import pallas as pl
from jax.experimental.pallas import tpu as pltpu

F32 = jnp.float32
BF16 = jnp.bfloat16

D_MODEL = 1024
D_FF = 2816
CHUNK = 64
HEAD_DIM = 64
ROPE_THETA = 500000.0
NORM_EPS = 1e-6
NEG_INF = -1e30
A_HEADS = 8
A_Q_LORA = 256
A_KV_LORA = 128
A_NOPE = 64
A_ROPE = 32
A_QK = A_NOPE + A_ROPE
A_V = 64
A_SCALE = A_QK ** -0.5
B_HEADS = 8
B_REACH = 512
B_MAX_REL = 128
C_HEADS = 16
C_KV_HEADS = 2
C_WINDOW = 128
C_ROT = HEAD_DIM // 4
HEAD_SCALE = HEAD_DIM ** -0.5
LOG2E = math.log2(math.e)

LANES = 128
TOKEN_TILE = 512
FFN_TILE = 1024
FFN_PARTS = 2
FFN_ROWS = FFN_TILE // FFN_PARTS
FF_TILE = 256
PROJ_PARTS = 2
PROJ_ROWS = TOKEN_TILE // PROJ_PARTS
ATT_TILE = 128
BAND_TILE = 256
BAND_BLOCKS = 2
MLA_Q_TILE = 512
MLA_K_TILE = 256
MXU_LOOKAHEAD = 2
BAND_LOOKAHEAD = 1
BIAS_DIAG_LANES = 1024
SWA_BLOCKS = 4
SWA_LOOKAHEAD = 2
ONES_ROWS = 16
VMEM_LIMIT = 56 * 1024 * 1024
B_WIDTH = B_HEADS * HEAD_DIM
C_QWIDTH = C_HEADS * HEAD_DIM
C_KWIDTH = C_KV_HEADS * HEAD_DIM


def _params(*sem):
    return pltpu.CompilerParams(dimension_semantics=sem, vmem_limit_bytes=VMEM_LIMIT)


def _const_spec(shape, index=None):
    index = index or (0,) * len(shape)
    return pl.BlockSpec(shape, lambda *_: index, pipeline_mode=pl.Buffered(1))


def _rms(x, g):
    ms = jnp.mean(x * x, axis=-1, keepdims=True)
    return x * lax.rsqrt(ms + NORM_EPS) * g


def _dot(a, b):
    return jnp.dot(a, b, preferred_element_type=F32)


def _dot_nt(a, b):
    return lax.dot_general(a, b, (((1,), (1,)), ((), ())), preferred_element_type=F32)


def _rope_group(x, c, s1, s2, half):
    return x * c + pltpu.roll(x, half, 1) * s1 + pltpu.roll(x, LANES - half, 1) * s2


def _lane_mask(lo):
    lane = lax.broadcasted_iota(jnp.int32, (1, LANES), 1)
    return (lane >= lo) & (lane < lo + HEAD_DIM)


def _swiglu_update(x, g_ref, wg_ref, wu_ref, wd_ref):
    xn = _rms(x, g_ref[0:1, :]).astype(BF16)
    acc = jnp.zeros(x.shape, F32)
    for c in range(D_FF // FF_TILE):
        sl = slice(c * FF_TILE, (c + 1) * FF_TILE)
        a = _dot(xn, wg_ref[:, sl])
        b = _dot(xn, wu_ref[:, sl])
        h = (a * jax.nn.sigmoid(a) * b).astype(BF16)
        acc = acc + _dot(h, wd_ref[sl, :])
    return x + 0.5 * _rms(acc, g_ref[1:2, :])


def _ffn_row_parts():
    return [slice(p * FFN_ROWS, (p + 1) * FFN_ROWS) for p in range(FFN_PARTS)]


def _ffn_kernel(x_ref, g_ref, wg_ref, wu_ref, wd_ref, o_ref):
    for rows in _ffn_row_parts():
        o_ref[rows, :] = _swiglu_update(x_ref[rows, :], g_ref, wg_ref, wu_ref, wd_ref)


def _mix_ffn_kernel(*refs, n_mix):
    x_ref = refs[0]
    mix_refs = refs[1:1 + n_mix]
    wo_ref, gm_ref, g_ref, wg_ref, wu_ref, wd_ref, o_ref = refs[1 + n_mix:]
    ys = []
    for rows in _ffn_row_parts():
        y = None
        row = 0
        for m_ref in mix_refs:
            w = m_ref.shape[1]
            part = _dot(m_ref[rows, :], wo_ref[row:row + w, :])
            y = part if y is None else y + part
            row += w
        ys.append(y)
    for rows, y in zip(_ffn_row_parts(), ys):
        x = x_ref[rows, :] + _rms(y, gm_ref[...])
        o_ref[rows, :] = _swiglu_update(x, g_ref, wg_ref, wu_ref, wd_ref)


def _ffn_weight_specs(layer, which):
    idx = (layer, which, 0, 0)
    return [_const_spec((None, None, D_MODEL, D_FF), idx), _const_spec((None, None, D_MODEL, D_FF), idx),
            _const_spec((None, None, D_FF, D_MODEL), idx)]


def _ffn(x, g, wg, wu, wd, layer, which):
    t = x.shape[0]
    tile = pl.BlockSpec((FFN_TILE, D_MODEL), lambda i: (i, 0))
    return pl.pallas_call(
        _ffn_kernel,
        out_shape=jax.ShapeDtypeStruct((t, D_MODEL), F32),
        grid=(t // FFN_TILE,),
        in_specs=[tile, _const_spec((2, D_MODEL))] + _ffn_weight_specs(layer, which),
        out_specs=tile,
        compiler_params=_params("parallel"),
        name="ffn",
    )(x, g, wg, wu, wd)


def _mix_ffn(x, mixes, wo, gm, g, wg, wu, wd, layer, which):
    t = x.shape[0]
    tile = pl.BlockSpec((FFN_TILE, D_MODEL), lambda i: (i, 0))
    mix_specs = [pl.BlockSpec((FFN_TILE, m.shape[1]), lambda i: (i, 0)) for m in mixes]
    return pl.pallas_call(
        functools.partial(_mix_ffn_kernel, n_mix=len(mixes)),
        out_shape=jax.ShapeDtypeStruct((t, D_MODEL), F32),
        grid=(t // FFN_TILE,),
        in_specs=[tile] + mix_specs + [_const_spec(wo.shape), _const_spec((1, D_MODEL)), _const_spec((2, D_MODEL))]
                 + _ffn_weight_specs(layer, which),
        out_specs=tile,
        compiler_params=_params("parallel"),
        name="mix_ffn",
    )(x, *mixes, wo, gm, g, wg, wu, wd)


def _even_proj_kernel(x_ref, g_ref, win_ref, qn_ref, kvn_ref, wuq_ref, wukv_ref,
                      c_ref, s1_ref, s2_ref,
                      qa_ref, ka_ref, va_ref, qb_ref, kb_ref, vb_ref,
                      ckv_ref, kpe_ref, kbf_ref, vbf_ref, *scratch, prompt, tiles_per_seq):
    half = A_ROPE // 2
    for part in range(PROJ_PARTS):
        rows = slice(part * PROJ_ROWS, (part + 1) * PROJ_ROWS)
        h = _rms(x_ref[rows, :], g_ref[...]).astype(BF16)
        proj = _dot(h, win_ref[...])
        cq = _rms(proj[:, 0:256], qn_ref[...]).astype(BF16)
        ckv = _rms(proj[:, 256:384], kvn_ref[...])
        c, s1, s2 = c_ref[rows, :], s1_ref[rows, :], s2_ref[rows, :]
        kpe = _rope_group(proj[:, 384:512], c, s1, s2, half)
        ckv_ref[rows, :] = ckv
        kpe_ref[rows, :] = pltpu.roll(kpe, LANES - A_NOPE, 1)[:, 0:A_ROPE]
        qa = _dot(cq, wuq_ref[...])
        kv = _dot(ckv.astype(BF16), wukv_ref[...])
        for hd in range(A_HEADS):
            sl = slice(hd * LANES, (hd + 1) * LANES)
            qa_ref[rows, sl] = (_rope_group(qa[:, sl], c, s1, s2, half) * (A_SCALE * LOG2E)).astype(BF16)
            ka_ref[rows, sl] = (kv[:, sl] + kpe).astype(BF16)
        for pair in range(B_HEADS // 2):
            grp = proj[:, 512 + pair * LANES:512 + (pair + 1) * LANES] * (HEAD_SCALE * LOG2E)
            for sub in range(2):
                qb_ref[2 * pair + sub, rows, :] = jnp.where(_lane_mask(sub * HEAD_DIM), grp, 0.0).astype(BF16)
        va = kv[:, A_HEADS * LANES:]
        kb = proj[:, 1024:1536]
        vb = proj[:, 1536:2048]
        kb_ref[rows, :] = kb.astype(BF16)
        if prompt:
            va_ref[:, rows] = va.T.astype(BF16)
            vb_ref[:, rows] = vb.T.astype(BF16)
            scratch[0][rows, :] = kb
            scratch[1][rows, :] = vb
        else:
            va_ref[rows, :] = va.astype(BF16)
            vb_ref[rows, :] = vb.astype(BF16)
            kbf_ref[rows, :] = kb
            vbf_ref[rows, :] = vb
    if prompt:
        @pl.when(pl.program_id(0) % tiles_per_seq == tiles_per_seq - 1)
        def _():
            kbf_ref[...] = scratch[0][...]
            vbf_ref[...] = scratch[1][...]


def _last_tile_index(tiles_per_seq):
    return lambda i: (jnp.maximum((i + 1) // tiles_per_seq - 1, 0), 0)


def _even_proj(x, g, win, qn, kvn, wuq, wukv, tabs, prompt, batch, seq):
    t = x.shape[0]
    tiles_per_seq = seq // TOKEN_TILE if prompt else 1

    def tile(w):
        return pl.BlockSpec((TOKEN_TILE, w), lambda i: (i, 0))

    def tile_t(w):
        return pl.BlockSpec((w, TOKEN_TILE), lambda i: (0, i))

    def sds(shape, dtype):
        return jax.ShapeDtypeStruct(shape, dtype)

    wa = A_HEADS * LANES
    if prompt:
        v_shapes = [sds((A_HEADS * A_V, t), BF16), sds((B_WIDTH, t), BF16)]
        v_specs = [tile_t(A_HEADS * A_V), tile_t(B_WIDTH)]
        keep_shape = sds((batch * TOKEN_TILE, B_WIDTH), F32)
        keep_spec = pl.BlockSpec((TOKEN_TILE, B_WIDTH), _last_tile_index(tiles_per_seq))
        tab_idx = lambda i: (i % tiles_per_seq, 0)
    else:
        v_shapes = [sds((t, A_HEADS * A_V), BF16), sds((t, B_WIDTH), BF16)]
        v_specs = [tile(A_HEADS * A_V), tile(B_WIDTH)]
        keep_shape = sds((t, B_WIDTH), F32)
        keep_spec = tile(B_WIDTH)
        tab_idx = lambda i: (0, 0)
    qb_spec = pl.BlockSpec((B_HEADS, TOKEN_TILE, LANES), lambda i: (0, i, 0))
    out_shape = [sds((t, wa), BF16), sds((t, wa), BF16), v_shapes[0], sds((B_HEADS, t, LANES), BF16),
                 sds((t, B_WIDTH), BF16), v_shapes[1], sds((t, A_KV_LORA), F32), sds((t, A_ROPE), F32),
                 keep_shape, keep_shape]
    out_specs = [tile(wa), tile(wa), v_specs[0], qb_spec, tile(B_WIDTH), v_specs[1],
                 tile(A_KV_LORA), tile(A_ROPE), keep_spec, keep_spec]
    return pl.pallas_call(
        functools.partial(_even_proj_kernel, prompt=prompt, tiles_per_seq=tiles_per_seq),
        out_shape=out_shape,
        grid=(t // TOKEN_TILE,),
        in_specs=[tile(D_MODEL), _const_spec((1, D_MODEL)), _const_spec(win.shape),
                  _const_spec((1, A_Q_LORA)), _const_spec((1, A_KV_LORA)),
                  _const_spec(wuq.shape), _const_spec(wukv.shape)]
                 + [pl.BlockSpec((TOKEN_TILE, LANES), tab_idx)] * 3,
        out_specs=out_specs,
        scratch_shapes=[pltpu.VMEM((TOKEN_TILE, B_WIDTH), F32)] * 2 if prompt else [],
        compiler_params=_params("arbitrary"),
        name="even_proj",
    )(x, g, win, qn, kvn, wuq, wukv, *tabs)


def _odd_proj_kernel(x_ref, g_ref, win_ref, c_ref, s1_ref, s2_ref,
                     q_ref, k_ref, v_ref, kf_ref, vf_ref, *scratch, prompt, tiles_per_seq):
    half = C_ROT // 2
    group = C_HEADS // C_KV_HEADS
    for part in range(PROJ_PARTS):
        rows = slice(part * PROJ_ROWS, (part + 1) * PROJ_ROWS)
        h = _rms(x_ref[rows, :], g_ref[...]).astype(BF16)
        proj = _dot(h, win_ref[...])
        c, s1, s2 = c_ref[rows, :], s1_ref[rows, :], s2_ref[rows, :]
        k = _rope_group(proj[:, C_QWIDTH:C_QWIDTH + LANES], c, s1, s2, half)
        v = proj[:, C_QWIDTH + LANES:C_QWIDTH + 2 * LANES]
        k_ref[rows, :] = k.astype(BF16)
        if prompt:
            for pair in range(C_HEADS // 2):
                grp = _rope_group(proj[:, pair * LANES:(pair + 1) * LANES], c, s1, s2, half) * (HEAD_SCALE * LOG2E)
                swapped = pltpu.roll(grp, HEAD_DIM, 1)
                for sub in range(2):
                    hd = 2 * pair + sub
                    kv = hd // group
                    src = grp if sub == kv else swapped
                    q_ref[hd, rows, :] = jnp.where(_lane_mask(kv * HEAD_DIM), src, 0.0).astype(BF16)
            v_ref[:, rows] = v.T.astype(BF16)
            if part == PROJ_PARTS - 1:
                scratch[0][...] = k[PROJ_ROWS - C_WINDOW:, :]
                scratch[1][...] = v[PROJ_ROWS - C_WINDOW:, :]
        else:
            for grp in range(C_QWIDTH // LANES):
                sl = slice(grp * LANES, (grp + 1) * LANES)
                q_ref[rows, sl] = (_rope_group(proj[:, sl], c, s1, s2, half) * (HEAD_SCALE * LOG2E)).astype(BF16)
            v_ref[rows, :] = v.astype(BF16)
            kf_ref[rows, :] = k
            vf_ref[rows, :] = v
    if prompt:
        @pl.when(pl.program_id(0) % tiles_per_seq == tiles_per_seq - 1)
        def _():
            kf_ref[...] = scratch[0][...]
            vf_ref[...] = scratch[1][...]


def _odd_proj(x, g, win, tabs, prompt, batch, seq):
    t = x.shape[0]
    tiles_per_seq = seq // TOKEN_TILE if prompt else 1

    def tile(w):
        return pl.BlockSpec((TOKEN_TILE, w), lambda i: (i, 0))

    def sds(shape, dtype):
        return jax.ShapeDtypeStruct(shape, dtype)

    if prompt:
        q_shape = sds((C_HEADS, t, LANES), BF16)
        q_spec = pl.BlockSpec((C_HEADS, TOKEN_TILE, LANES), lambda i: (0, i, 0))
        v_shape, v_spec = sds((LANES, t), BF16), pl.BlockSpec((LANES, TOKEN_TILE), lambda i: (0, i))
        keep_shape = sds((batch * C_WINDOW, LANES), F32)
        keep_spec = pl.BlockSpec((C_WINDOW, LANES), _last_tile_index(tiles_per_seq))
        tab_idx = lambda i: (i % tiles_per_seq, 0)
    else:
        q_shape, q_spec = sds((t, C_QWIDTH), BF16), tile(C_QWIDTH)
        v_shape, v_spec = sds((t, LANES), BF16), tile(LANES)
        keep_shape, keep_spec = sds((t, LANES), F32), tile(LANES)
        tab_idx = lambda i: (0, 0)
    return pl.pallas_call(
        functools.partial(_odd_proj_kernel, prompt=prompt, tiles_per_seq=tiles_per_seq),
        out_shape=[q_shape, sds((t, LANES), BF16), v_shape, keep_shape, keep_shape],
        grid=(t // TOKEN_TILE,),
        in_specs=[tile(D_MODEL), _const_spec((1, D_MODEL)), _const_spec(win.shape)]
                 + [pl.BlockSpec((TOKEN_TILE, LANES), tab_idx)] * 3,
        out_specs=[q_spec, tile(LANES), v_spec, keep_spec, keep_spec],
        scratch_shapes=[pltpu.VMEM((C_WINDOW, LANES), F32)] * 2 if prompt else [],
        compiler_params=_params("arbitrary"),
        name="odd_proj",
    )(x, g, win, *tabs)


def _band_bias_kernel(tab_ref, o_ref, ot_ref):
    hd = pl.program_id(0)
    w = B_REACH + BAND_TILE
    row = lax.broadcasted_iota(jnp.int32, (BAND_TILE, w), 0)
    col = lax.broadcasted_iota(jnp.int32, (BAND_TILE, w), 1)
    lo = (row // CHUNK) * CHUNK
    visible = (col >= lo) & (col < lo + B_REACH + CHUNK)
    n = BIAS_DIAG_LANES
    e = lax.broadcasted_iota(jnp.int32, (1, n), 1)
    rel = jnp.clip(B_REACH + BAND_TILE - 1 - e, -B_MAX_REL, B_MAX_REL) + B_MAX_REL

    def body(r, acc):
        return jnp.where(rel == r, tab_ref[hd, r], acc)

    diag = lax.fori_loop(0, 2 * B_MAX_REL + 1, body, jnp.zeros((1, n), F32))
    shifted = pltpu.roll(jnp.broadcast_to(diag, (BAND_TILE, n)), n - (BAND_TILE - 1), 1, stride=1, stride_axis=0)
    bias = jnp.where(visible, shifted[:, 0:w] * LOG2E, NEG_INF)
    o_ref[...] = bias
    ot_ref[...] = bias.T


def _band_bias(table):
    w = B_REACH + BAND_TILE
    return pl.pallas_call(
        _band_bias_kernel,
        out_shape=[jax.ShapeDtypeStruct((B_HEADS, BAND_TILE, w), F32),
                   jax.ShapeDtypeStruct((B_HEADS, w, BAND_TILE), F32)],
        grid=(B_HEADS,),
        in_specs=[pl.BlockSpec(memory_space=pltpu.SMEM)],
        out_specs=[pl.BlockSpec((None, BAND_TILE, w), lambda h: (h, 0, 0)),
                   pl.BlockSpec((None, w, BAND_TILE), lambda h: (h, 0, 0))],
        compiler_params=_params("arbitrary"),
        name="band_bias",
    )(table)


def _ones_rows(n):
    return jnp.ones((ONES_ROWS, n), BF16)


def _mla_prompt_kernel(q_ref, k_ref, vt_ref, o_ref, m_ref, l_ref, acc_ref):
    i = pl.program_id(1)
    tq, tk = MLA_Q_TILE, MLA_K_TILE
    m_ref[...] = jnp.full(m_ref.shape, NEG_INF, F32)
    l_ref[...] = jnp.zeros(l_ref.shape, F32)
    acc_ref[...] = jnp.zeros(acc_ref.shape, F32)
    kc = lax.broadcasted_iota(jnp.int32, (tk, tq), 0) // CHUNK
    qc = lax.broadcasted_iota(jnp.int32, (tk, tq), 1) // CHUNK

    def block(j, diag):
        ks = pl.multiple_of(j * tk, tk)
        m_all, l_all, acc_all = m_ref[...], l_ref[...], acc_ref[...]
        ones = _ones_rows(tk)

        def scores(hd):
            sl = slice(hd * LANES, (hd + 1) * LANES)
            return _dot_nt(k_ref[pl.ds(ks, tk), sl], q_ref[:, sl])

        pending = {hd: scores(hd) for hd in range(MXU_LOOKAHEAD)}
        new = []
        for hd in range(A_HEADS):
            if hd + MXU_LOOKAHEAD < A_HEADS:
                pending[hd + MXU_LOOKAHEAD] = scores(hd + MXU_LOOKAHEAD)
            s = pending.pop(hd)
            if diag is not None:
                s = jnp.where(kc + diag * (tk // CHUNK) <= qc, s, NEG_INF)
            m_new = jnp.maximum(m_all[hd], jnp.max(s, axis=0, keepdims=True))
            alpha = jnp.exp2(m_all[hd] - m_new)
            p = jnp.exp2(s - m_new).astype(BF16)
            vt = jnp.concatenate([vt_ref[hd * A_V:(hd + 1) * A_V, pl.ds(ks, tk)], ones], axis=0)
            pv = _dot(vt, p)
            new.append((m_new, alpha * l_all[hd] + pv[A_V:A_V + 1], alpha * acc_all[hd] + pv[0:A_V]))
        for hd, (m_new, l_new, acc_new) in enumerate(new):
            m_ref[hd] = m_new
            l_ref[hd] = l_new
            acc_ref[hd] = acc_new

    def body(j, carry):
        block(j, None)
        return carry

    per_q = tq // tk
    lax.fori_loop(0, i * per_q, body, 0)
    for d in range(per_q):
        block(i * per_q + d, d)
    outs = [acc_ref[hd] / l_ref[hd] for hd in range(A_HEADS)]
    o_ref[...] = jnp.concatenate(outs, axis=0).T.astype(BF16)


def _mla_prompt(qa, ka, vat, batch, seq):
    nq = seq // MLA_Q_TILE
    wa = A_HEADS * LANES
    wv = A_HEADS * A_V
    return pl.pallas_call(
        _mla_prompt_kernel,
        out_shape=jax.ShapeDtypeStruct((batch * seq, wv), BF16),
        grid=(batch, nq),
        in_specs=[pl.BlockSpec((MLA_Q_TILE, wa), lambda b, i: (b * nq + i, 0)),
                  pl.BlockSpec((seq, wa), lambda b, i: (b, 0)),
                  pl.BlockSpec((wv, seq), lambda b, i: (0, b))],
        out_specs=pl.BlockSpec((MLA_Q_TILE, wv), lambda b, i: (b * nq + i, 0)),
        scratch_shapes=[pltpu.VMEM((A_HEADS, 1, MLA_Q_TILE), F32), pltpu.VMEM((A_HEADS, 1, MLA_Q_TILE), F32),
                        pltpu.VMEM((A_HEADS, A_V, MLA_Q_TILE), F32)],
        compiler_params=_params("parallel", "arbitrary"),
        name="mla_prompt",
    )(qa, ka, vat)


def _fill_padded(kpad, vtpad, k_ref, vt_ref, reach):
    kpad[0:reach, :] = jnp.zeros((reach, kpad.shape[1]), kpad.dtype)
    kpad[reach:, :] = k_ref[...]
    vtpad[:, 0:reach] = jnp.zeros((vtpad.shape[0], reach), vtpad.dtype)
    vtpad[:, reach:] = vt_ref[...]


def _window_attend(s, vt, sink=None):
    m = jnp.max(s, axis=0, keepdims=True)
    if sink is not None:
        m = jnp.maximum(m, sink)
    p = jnp.exp2(s - m).astype(BF16)
    d = vt.shape[0]
    pv = _dot(jnp.concatenate([vt, _ones_rows(vt.shape[1])], axis=0), p)
    l = pv[d:d + 1]
    if sink is not None:
        l = l + jnp.exp2(sink - m)
    return pv[0:d] * (1.0 / l)


def _band_prompt_kernel(q_ref, k_ref, vt_ref, bias_ref, o_ref, kpad, vtpad):
    i = pl.program_id(1)
    tq = BAND_TILE

    @pl.when(i == 0)
    def _():
        _fill_padded(kpad, vtpad, k_ref, vt_ref, B_REACH)

    w = B_REACH + tq
    n_pairs = B_HEADS // 2
    step = BAND_BLOCKS * tq

    def attend(mask_start):
        key = lax.broadcasted_iota(jnp.int32, (w, 2 * tq), 0)

        def first_row(blk):
            return pl.multiple_of((i * BAND_BLOCKS + blk) * tq, tq)

        def scores(unit):
            blk, pair = divmod(unit, n_pairs)
            q2 = q_ref[2 * pair:2 * pair + 2, blk * tq:(blk + 1) * tq, :].reshape(2 * tq, LANES)
            return _dot_nt(kpad[pl.ds(first_row(blk), w), pair * LANES:(pair + 1) * LANES], q2)

        n_units = BAND_BLOCKS * n_pairs
        pending = {unit: scores(unit) for unit in range(BAND_LOOKAHEAD)}
        for unit in range(n_units):
            if unit + BAND_LOOKAHEAD < n_units:
                pending[unit + BAND_LOOKAHEAD] = scores(unit + BAND_LOOKAHEAD)
            blk, pair = divmod(unit, n_pairs)
            if pair == 0:
                outs = []
            s = pending.pop(unit) + bias_ref[pair]
            if mask_start:
                s = jnp.where(key >= B_REACH - first_row(blk), s, NEG_INF)
            o2 = _window_attend(s, vtpad[pair * LANES:(pair + 1) * LANES, pl.ds(first_row(blk), w)])
            outs.append(o2[0:HEAD_DIM, 0:tq])
            outs.append(o2[HEAD_DIM:, tq:])
            if pair == n_pairs - 1:
                o_ref[blk * tq:(blk + 1) * tq, :] = jnp.concatenate(outs, axis=0).T.astype(BF16)

    pl.when(i * step < B_REACH)(functools.partial(attend, True))
    pl.when(i * step >= B_REACH)(functools.partial(attend, False))


def _band_prompt(qb, kb, vbt, bias_pairs, batch, seq):
    step = BAND_BLOCKS * BAND_TILE
    nq = seq // step
    return pl.pallas_call(
        _band_prompt_kernel,
        out_shape=jax.ShapeDtypeStruct((batch * seq, B_WIDTH), BF16),
        grid=(batch, nq),
        in_specs=[pl.BlockSpec((B_HEADS, step, LANES), lambda b, i: (0, b * nq + i, 0)),
                  pl.BlockSpec((seq, B_WIDTH), lambda b, i: (b, 0)),
                  pl.BlockSpec((B_WIDTH, seq), lambda b, i: (0, b)),
                  _const_spec(bias_pairs.shape)],
        out_specs=pl.BlockSpec((step, B_WIDTH), lambda b, i: (b * nq + i, 0)),
        scratch_shapes=[pltpu.VMEM((seq + B_REACH, B_WIDTH), BF16), pltpu.VMEM((B_WIDTH, seq + B_REACH), BF16)],
        compiler_params=_params("parallel", "arbitrary"),
        name="band_prompt",
    )(qb, kb, vbt, bias_pairs)


def _swa_prompt_kernel(sink_ref, q_ref, k_ref, vt_ref, o_ref, kpad, vtpad, *, layer):
    i = pl.program_id(1)

    @pl.when(i == 0)
    def _():
        _fill_padded(kpad, vtpad, k_ref, vt_ref, C_WINDOW)

    w = C_WINDOW + ATT_TILE
    group = C_HEADS // C_KV_HEADS
    key = lax.broadcasted_iota(jnp.int32, (w, ATT_TILE), 0)
    qry = lax.broadcasted_iota(jnp.int32, (w, ATT_TILE), 1)
    lo = (qry // CHUNK) * CHUNK
    in_window = (key >= lo) & (key < lo + C_WINDOW + CHUNK)

    def scores(unit):
        blk, kv = divmod(unit, C_KV_HEADS)
        start = pl.multiple_of((i * SWA_BLOCKS + blk) * ATT_TILE, ATT_TILE)
        q = q_ref[kv * group:(kv + 1) * group, blk * ATT_TILE:(blk + 1) * ATT_TILE, :]
        return _dot_nt(kpad[pl.ds(start, w), :], q.reshape(group * ATT_TILE, LANES))

    n_units = SWA_BLOCKS * C_KV_HEADS
    pending = {unit: scores(unit) for unit in range(SWA_LOOKAHEAD)}
    for unit in range(n_units):
        if unit + SWA_LOOKAHEAD < n_units:
            pending[unit + SWA_LOOKAHEAD] = scores(unit + SWA_LOOKAHEAD)
        blk, kv = divmod(unit, C_KV_HEADS)
        if kv == 0:
            first = (i * SWA_BLOCKS + blk) * ATT_TILE
            start = pl.multiple_of(first, ATT_TILE)
            visible = in_window & (key >= C_WINDOW - first)
            mask = jnp.concatenate([jnp.where(visible, 0.0, NEG_INF)] * group, axis=1)
            outs = []
        sink = jnp.concatenate([jnp.full((1, ATT_TILE), sink_ref[layer, kv * group + g] * LOG2E, F32)
                                for g in range(group)], axis=1)
        og = _window_attend(pending.pop(unit) + mask, vtpad[kv * HEAD_DIM:(kv + 1) * HEAD_DIM, pl.ds(start, w)], sink)
        outs.extend(og[:, g * ATT_TILE:(g + 1) * ATT_TILE] for g in range(group))
        if kv == C_KV_HEADS - 1:
            o_ref[blk * ATT_TILE:(blk + 1) * ATT_TILE, :] = jnp.concatenate(outs, axis=0).T.astype(BF16)


def _swa_prompt(sinks, layer, q, k, vt, batch, seq):
    step = SWA_BLOCKS * ATT_TILE
    nq = seq // step
    return pl.pallas_call(
        functools.partial(_swa_prompt_kernel, layer=layer),
        out_shape=jax.ShapeDtypeStruct((batch * seq, C_QWIDTH), BF16),
        grid=(batch, nq),
        in_specs=[pl.BlockSpec(memory_space=pltpu.SMEM),
                  pl.BlockSpec((C_HEADS, step, LANES), lambda b, i: (0, b * nq + i, 0)),
                  pl.BlockSpec((seq, C_KWIDTH), lambda b, i: (b, 0)),
                  pl.BlockSpec((C_KWIDTH, seq), lambda b, i: (0, b))],
        out_specs=pl.BlockSpec((step, C_QWIDTH), lambda b, i: (b * nq + i, 0)),
        scratch_shapes=[pltpu.VMEM((seq + C_WINDOW, C_KWIDTH), BF16), pltpu.VMEM((C_KWIDTH, seq + C_WINDOW), BF16)],
        compiler_params=_params("parallel", "arbitrary"),
        name="swa_prompt",
    )(sinks, q, k, vt)


def _softmax_pv2(s_c, s_n, v_c, v_n, sink=None):
    m = jnp.maximum(jnp.max(s_c, axis=-1, keepdims=True), jnp.max(s_n, axis=-1, keepdims=True))
    if sink is not None:
        m = jnp.maximum(m, sink)
    p_c = jnp.exp2(s_c - m)
    p_n = jnp.exp2(s_n - m)
    l = jnp.sum(p_c, axis=-1, keepdims=True) + jnp.sum(p_n, axis=-1, keepdims=True)
    if sink is not None:
        l = l + jnp.exp2(sink - m)
    return (_dot(p_c.astype(BF16), v_c) + _dot(p_n.astype(BF16), v_n)) / l


def _even_sample_kernel(qa_ref, ka_ref, va_ref, qb_ref, kb_ref, vb_ref,
                        cckv_ref, ckpe_ref, cbk_ref, cbv_ref, wukv_ref, place_ref, bias_ref,
                        oa_ref, ob_ref):
    t = qa_ref.shape[0]
    nb = cbk_ref.shape[0]
    kvc = _dot(cckv_ref[...].astype(BF16), wukv_ref[...])
    kpe = _dot(ckpe_ref[...].astype(BF16), place_ref[...])
    scores_a, scores_b = [], []
    for hd in range(A_HEADS):
        sl = slice(hd * LANES, (hd + 1) * LANES)
        k_c = (kvc[:, sl] + kpe).astype(BF16)
        q = qa_ref[:, sl]
        scores_a.append((_dot_nt(q, k_c), _dot_nt(q, ka_ref[:, sl])))
    for hd in range(B_HEADS):
        pair = slice((hd // 2) * LANES, (hd // 2 + 1) * LANES)
        q = qb_ref[hd]
        scores_b.append((_dot_nt(q, cbk_ref[:, pair].astype(BF16)) + bias_ref[hd, 0:t, 0:nb],
                         _dot_nt(q, kb_ref[:, pair]) + bias_ref[hd, 0:t, nb:nb + t]))
    outs_a, outs_b = [], []
    for hd in range(A_HEADS):
        vsl = slice(hd * A_V, (hd + 1) * A_V)
        v_c = kvc[:, A_HEADS * LANES + hd * A_V:A_HEADS * LANES + (hd + 1) * A_V].astype(BF16)
        outs_a.append(_softmax_pv2(*scores_a[hd], v_c, va_ref[:, vsl]))
    for hd in range(B_HEADS):
        sl = slice(hd * HEAD_DIM, (hd + 1) * HEAD_DIM)
        outs_b.append(_softmax_pv2(*scores_b[hd], cbv_ref[:, sl].astype(BF16), vb_ref[:, sl]))
    oa_ref[...] = jnp.concatenate(outs_a, axis=-1).astype(BF16)
    ob_ref[...] = jnp.concatenate(outs_b, axis=-1).astype(BF16)


def _even_sample(layer, qa, ka, va, qb, kb, vb, c_ckv, c_kpe, c_bk, c_bv, wukv, place, bias, dec_batch, t):
    past = c_ckv.shape[2]
    nb = c_bk.shape[2]

    def new(w):
        return pl.BlockSpec((t, w), lambda b: (b, 0))

    def cache(n, w):
        return pl.BlockSpec((None, None, n, w), lambda b: (layer, b, 0, 0))

    return pl.pallas_call(
        _even_sample_kernel,
        out_shape=[jax.ShapeDtypeStruct((dec_batch * t, B_WIDTH), BF16)] * 2,
        grid=(dec_batch,),
        in_specs=[new(A_HEADS * LANES), new(A_HEADS * LANES), new(A_HEADS * A_V),
                  pl.BlockSpec((B_HEADS, t, LANES), lambda b: (0, b, 0)), new(B_WIDTH), new(B_WIDTH),
                  cache(past, A_KV_LORA), cache(past, A_ROPE), cache(nb, B_WIDTH), cache(nb, B_WIDTH),
                  _const_spec(wukv.shape), _const_spec(place.shape), _const_spec(bias.shape)],
        out_specs=[new(B_WIDTH), new(B_WIDTH)],
        compiler_params=_params("parallel"),
        name="even_sample",
    )(qa, ka, va, qb, kb, vb, c_ckv, c_kpe, c_bk, c_bv, wukv, place, bias)


def _odd_sample_kernel(sink_ref, q_ref, k_ref, v_ref, ck_ref, cv_ref, o_ref, *, layer):
    group = C_HEADS // C_KV_HEADS
    scores = []
    for hd in range(C_HEADS):
        ksl = slice((hd // group) * HEAD_DIM, (hd // group + 1) * HEAD_DIM)
        q = q_ref[:, hd * HEAD_DIM:(hd + 1) * HEAD_DIM]
        scores.append((_dot_nt(q, ck_ref[:, ksl].astype(BF16)), _dot_nt(q, k_ref[:, ksl])))
    outs = []
    for hd in range(C_HEADS):
        ksl = slice((hd // group) * HEAD_DIM, (hd // group + 1) * HEAD_DIM)
        outs.append(_softmax_pv2(*scores[hd], cv_ref[:, ksl].astype(BF16), v_ref[:, ksl],
                                 sink_ref[layer, hd] * LOG2E))
    o_ref[...] = jnp.concatenate(outs, axis=-1).astype(BF16)


def _odd_sample(sinks, layer, q, k, v, c_k, c_v, dec_batch, t):
    nc = c_k.shape[2]

    def new(w):
        return pl.BlockSpec((t, w), lambda b: (b, 0))

    def cache():
        return pl.BlockSpec((None, None, nc, C_KWIDTH), lambda b: (layer, b, 0, 0))

    return pl.pallas_call(
        functools.partial(_odd_sample_kernel, layer=layer),
        out_shape=jax.ShapeDtypeStruct((dec_batch * t, C_QWIDTH), BF16),
        grid=(dec_batch,),
        in_specs=[pl.BlockSpec(memory_space=pltpu.SMEM), new(C_QWIDTH), new(C_KWIDTH), new(C_KWIDTH),
                  cache(), cache()],
        out_specs=new(C_QWIDTH),
        compiler_params=_params("parallel"),
        name="odd_sample",
    )(sinks, q, k, v, c_k, c_v)


def _rope_parts(pos, n_rot):
    half = n_rot // 2
    inv = ROPE_THETA ** (-jnp.arange(half, dtype=F32) / half)
    ang = pos.astype(F32)[:, None] * inv[None, :]
    return jnp.cos(ang), jnp.sin(ang)


def _rope_tables(pos, n_rot, pre, width):
    cos, sin = _rope_parts(pos, n_rot)
    n = pos.shape[0]
    post = width - pre - n_rot

    def head(first, second, fill):
        return jnp.concatenate([jnp.full((n, pre), fill, F32), first, second, jnp.full((n, post), fill, F32)], axis=1)

    zero = jnp.zeros_like(sin)
    tabs = (head(cos, cos, 1.0), head(zero, sin, 0.0), head(-sin, zero, 0.0))
    return tuple(jnp.tile(t, (1, LANES // width)) for t in tabs)


def kernel(x_prompt, x_sample, cache_mla_ckv, cache_mla_kpe, cache_band_k, cache_band_v,
           cache_swa_k, cache_swa_v, norm_g, ffn_w_gate, ffn_w_up, ffn_w_down, even_w_in,
           mla_q_norm, mla_w_uq, mla_kv_norm, mla_w_ukv, band_rel_bias, even_w_out,
           odd_w_in, swa_sinks, odd_w_out):
    batch, seq, _ = x_prompt.shape
    dec_batch, t_new, _ = x_sample.shape
    depth = norm_g.shape[0]
    n_even = even_w_in.shape[0]
    n_odd = odd_w_in.shape[0]
    past = cache_mla_ckv.shape[2]
    n_p = batch * seq
    n_s = dec_batch * t_new
    assert seq % TOKEN_TILE == 0 and n_s % TOKEN_TILE == 0 and TOKEN_TILE % t_new == 0
    assert n_p % FFN_TILE == 0 and n_s % FFN_TILE == 0
    assert seq % MLA_Q_TILE == 0 and seq % (BAND_BLOCKS * BAND_TILE) == 0 and seq % (SWA_BLOCKS * ATT_TILE) == 0 and min(B_REACH, seq) == TOKEN_TILE and t_new <= CHUNK
    assert cache_band_k.shape[2] == B_REACH and cache_swa_k.shape[2] == C_WINDOW and past >= B_REACH

    pos_p = jnp.arange(seq, dtype=jnp.int32)
    pos_s = past + (jnp.arange(TOKEN_TILE, dtype=jnp.int32) % t_new)
    mla_tabs = {True: _rope_tables(pos_p, A_ROPE, A_NOPE, LANES), False: _rope_tables(pos_s, A_ROPE, A_NOPE, LANES)}
    swa_tabs = {True: _rope_tables(pos_p, C_ROT, 0, HEAD_DIM), False: _rope_tables(pos_s, C_ROT, 0, HEAD_DIM)}

    wg = ffn_w_gate.astype(BF16)
    wu = ffn_w_up.astype(BF16)
    wd = ffn_w_down.astype(BF16)

    place = jnp.zeros((A_ROPE, LANES), F32).at[jnp.arange(A_ROPE), A_NOPE + jnp.arange(A_ROPE)].set(1.0).astype(BF16)
    c_bk = cache_band_k.reshape(n_even, dec_batch, B_REACH, B_WIDTH)
    c_bv = cache_band_v.reshape(n_even, dec_batch, B_REACH, B_WIDTH)
    c_sk = cache_swa_k.reshape(n_odd, dec_batch, C_WINDOW, C_KWIDTH)
    c_sv = cache_swa_v.reshape(n_odd, dec_batch, C_WINDOW, C_KWIDTH)

    c0 = A_Q_LORA + A_KV_LORA
    c1 = c0 + A_ROPE
    xs = {True: x_prompt.reshape(n_p, D_MODEL), False: x_sample.reshape(n_s, D_MODEL)}
    new = {True: [[] for _ in range(6)], False: [[] for _ in range(6)]}
    for l in range(depth):
        g = norm_g[l]
        i = l // 2
        if l % 2 == 0:
            w = even_w_in[i]
            zc = lambda n: jnp.zeros((D_MODEL, n), F32)
            win = jnp.concatenate([w[:, :c0], zc(A_NOPE), w[:, c0:c1], zc(LANES - A_NOPE - A_ROPE), w[:, c1:]],
                                  axis=1).astype(BF16)
            wuq = jnp.pad(mla_w_uq[i].reshape(A_Q_LORA, A_HEADS, A_QK),
                          ((0, 0), (0, 0), (0, LANES - A_QK))).reshape(A_Q_LORA, A_HEADS * LANES).astype(BF16)
            wkv = mla_w_ukv[i].reshape(A_KV_LORA, A_HEADS, A_NOPE + A_V)
            wk = jnp.pad(wkv[:, :, :A_NOPE], ((0, 0), (0, 0), (0, LANES - A_NOPE))).reshape(A_KV_LORA, A_HEADS * LANES)
            wv = wkv[:, :, A_NOPE:].reshape(A_KV_LORA, A_HEADS * A_V)
            wukv = jnp.concatenate([wk, wv], axis=1).astype(BF16)
            bias, bias_t = _band_bias(band_rel_bias[i])
            w_band = B_REACH + BAND_TILE
            bias_pairs = bias_t.reshape(B_HEADS // 2, 2, w_band, BAND_TILE).transpose(0, 2, 1, 3).reshape(
                B_HEADS // 2, w_band, 2 * BAND_TILE)
            wo = even_w_out[i].astype(BF16)
        else:
            win = odd_w_in[i].astype(BF16)
            wo = odd_w_out[i].astype(BF16)
        for prompt in (True, False):
            x = _ffn(xs[prompt], g[0:2], wg, wu, wd, l, 0)
            st = new[prompt]
            if l % 2 == 0:
                qa, ka, va, qb, kb, vb, ckv, kpe, kbf, vbf = _even_proj(
                    x, g[2:3], win, mla_q_norm[i][None], mla_kv_norm[i][None], wuq, wukv, mla_tabs[prompt],
                    prompt, batch, seq)
                if prompt:
                    mixes = [_mla_prompt(qa, ka, va, batch, seq), _band_prompt(qb, kb, vb, bias_pairs, batch, seq)]
                    lead = (batch, seq)
                else:
                    mixes = _even_sample(i, qa, ka, va, qb, kb, vb, cache_mla_ckv, cache_mla_kpe, c_bk, c_bv,
                                         wukv, place, bias, dec_batch, t_new)
                    lead = (dec_batch, t_new)
                st[0].append(ckv.reshape(*lead, A_KV_LORA))
                st[1].append(kpe.reshape(*lead, A_ROPE))
                st[2].append(kbf.reshape(lead[0], -1, B_HEADS, HEAD_DIM))
                st[3].append(vbf.reshape(lead[0], -1, B_HEADS, HEAD_DIM))
            else:
                q, k, v, kf, vf = _odd_proj(x, g[2:3], win, swa_tabs[prompt], prompt, batch, seq)
                if prompt:
                    mixes = [_swa_prompt(swa_sinks, i, q, k, v, batch, seq)]
                    lead = batch
                else:
                    mixes = [_odd_sample(swa_sinks, i, q, k, v, c_sk, c_sv, dec_batch, t_new)]
                    lead = dec_batch
                st[4].append(kf.reshape(lead, -1, C_KV_HEADS, HEAD_DIM))
                st[5].append(vf.reshape(lead, -1, C_KV_HEADS, HEAD_DIM))
            xs[prompt] = _mix_ffn(x, mixes, wo, g[3:4], g[4:6], wg, wu, wd, l, 1)

    y_prompt = xs[True].reshape(batch, seq, D_MODEL)
    y_sample = xs[False].reshape(dec_batch, t_new, D_MODEL)
    st_p = [jnp.stack(s, axis=0) for s in new[True]]
    st_s = [jnp.stack(s, axis=0) for s in new[False]]
    return (y_prompt, y_sample, *st_p, *st_s)
```

```python
import functools
import math

import jax
import jax.numpy as jnp
from jax import lax
from jax.experimental import pallas as pl
from jax.experimental.pallas import tpu as pltpu

F32 = jnp.float32
BF16 = jnp.bfloat16

D_MODEL = 1024
D_FF = 2816
CHUNK = 64
HEAD_DIM = 64
ROPE_THETA = 500000.0
NORM_EPS = 1e-6
NEG_INF = -1e30
A_HEADS = 8
A_Q_LORA = 256
A_KV_LORA = 128
A_NOPE = 64
A_ROPE = 32
A_QK = A_NOPE + A_ROPE
A_V = 64
A_SCALE = A_QK ** -0.5
B_HEADS = 8
B_REACH = 512
B_MAX_REL = 128
C_HEADS = 16
C_KV_HEADS = 2
C_WINDOW = 128
C_ROT = HEAD_DIM // 4
HEAD_SCALE = HEAD_DIM ** -0.5
LOG2E = math.log2(math.e)

LANES = 128
TOKEN_TILE = 512
FFN_TILE = 1024
FFN_PARTS = 2
FFN_ROWS = FFN_TILE // FFN_PARTS
FF_TILE = 256
PROJ_PARTS = 2
PROJ_ROWS = TOKEN_TILE // PROJ_PARTS
ATT_TILE = 128
BAND_TILE = 256
BAND_BLOCKS = 2
MLA_Q_TILE = 512
MLA_K_TILE = 256
BIAS_DIAG_LANES = 1024
SWA_BLOCKS = 4
ONES_ROWS = 16
VMEM_LIMIT = 56 * 1024 * 1024
B_WIDTH = B_HEADS * HEAD_DIM
C_QWIDTH = C_HEADS * HEAD_DIM
C_KWIDTH = C_KV_HEADS * HEAD_DIM


def _params(*sem):
    return pltpu.CompilerParams(dimension_semantics=sem, vmem_limit_bytes=VMEM_LIMIT)


def _const_spec(shape, index=None):
    index = index or (0,) * len(shape)
    return pl.BlockSpec(shape, lambda *_: index, pipeline_mode=pl.Buffered(1))


def _rms(x, g):
    ms = jnp.mean(x * x, axis=-1, keepdims=True)
    return x * lax.rsqrt(ms + NORM_EPS) * g


def _dot(a, b):
    return jnp.dot(a, b, preferred_element_type=F32)


def _dot_nt(a, b):
    return lax.dot_general(a, b, (((1,), (1,)), ((), ())), preferred_element_type=F32)


def _rope_group(x, c, s1, s2, half):
    return x * c + pltpu.roll(x, half, 1) * s1 + pltpu.roll(x, LANES - half, 1) * s2


def _lane_mask(lo):
    lane = lax.broadcasted_iota(jnp.int32, (1, LANES), 1)
    return (lane >= lo) & (lane < lo + HEAD_DIM)


def _swiglu_update(x, g_ref, wg_ref, wu_ref, wd_ref):
    xn = _rms(x, g_ref[0:1, :]).astype(BF16)
    acc = jnp.zeros(x.shape, F32)
    for c in range(D_FF // FF_TILE):
        sl = slice(c * FF_TILE, (c + 1) * FF_TILE)
        a = _dot(xn, wg_ref[:, sl])
        b = _dot(xn, wu_ref[:, sl])
        h = (a * jax.nn.sigmoid(a) * b).astype(BF16)
        acc = acc + _dot(h, wd_ref[sl, :])
    return x + 0.5 * _rms(acc, g_ref[1:2, :])


def _ffn_row_parts():
    return [slice(p * FFN_ROWS, (p + 1) * FFN_ROWS) for p in range(FFN_PARTS)]


def _ffn_kernel(x_ref, g_ref, wg_ref, wu_ref, wd_ref, o_ref):
    for rows in _ffn_row_parts():
        o_ref[rows, :] = _swiglu_update(x_ref[rows, :], g_ref, wg_ref, wu_ref, wd_ref)


def _mix_ffn_kernel(*refs, n_mix):
    x_ref = refs[0]
    mix_refs = refs[1:1 + n_mix]
    wo_ref, gm_ref, g_ref, wg_ref, wu_ref, wd_ref, o_ref = refs[1 + n_mix:]
    ys = []
    for rows in _ffn_row_parts():
        y = None
        row = 0
        for m_ref in mix_refs:
            w = m_ref.shape[1]
            part = _dot(m_ref[rows, :], wo_ref[row:row + w, :])
            y = part if y is None else y + part
            row += w
        ys.append(y)
    for rows, y in zip(_ffn_row_parts(), ys):
        x = x_ref[rows, :] + _rms(y, gm_ref[...])
        o_ref[rows, :] = _swiglu_update(x, g_ref, wg_ref, wu_ref, wd_ref)


def _ffn_weight_specs(layer, which):
    idx = (layer, which, 0, 0)
    return [_const_spec((None, None, D_MODEL, D_FF), idx), _const_spec((None, None, D_MODEL, D_FF), idx),
            _const_spec((None, None, D_FF, D_MODEL), idx)]


def _ffn(x, g, wg, wu, wd, layer, which):
    t = x.shape[0]
    tile = pl.BlockSpec((FFN_TILE, D_MODEL), lambda i: (i, 0))
    return pl.pallas_call(
        _ffn_kernel,
        out_shape=jax.ShapeDtypeStruct((t, D_MODEL), F32),
        grid=(t // FFN_TILE,),
        in_specs=[tile, _const_spec((2, D_MODEL))] + _ffn_weight_specs(layer, which),
        out_specs=tile,
        compiler_params=_params("parallel"),
        name="ffn",
    )(x, g, wg, wu, wd)


def _mix_ffn(x, mixes, wo, gm, g, wg, wu, wd, layer, which):
    t = x.shape[0]
    tile = pl.BlockSpec((FFN_TILE, D_MODEL), lambda i: (i, 0))
    mix_specs = [pl.BlockSpec((FFN_TILE, m.shape[1]), lambda i: (i, 0)) for m in mixes]
    return pl.pallas_call(
        functools.partial(_mix_ffn_kernel, n_mix=len(mixes)),
        out_shape=jax.ShapeDtypeStruct((t, D_MODEL), F32),
        grid=(t // FFN_TILE,),
        in_specs=[tile] + mix_specs + [_const_spec(wo.shape), _const_spec((1, D_MODEL)), _const_spec((2, D_MODEL))]
                 + _ffn_weight_specs(layer, which),
        out_specs=tile,
        compiler_params=_params("parallel"),
        name="mix_ffn",
    )(x, *mixes, wo, gm, g, wg, wu, wd)


def _even_proj_kernel(x_ref, g_ref, win_ref, qn_ref, kvn_ref, wuq_ref, wukv_ref,
                      c_ref, s1_ref, s2_ref,
                      qa_ref, ka_ref, va_ref, qb_ref, kb_ref, vb_ref,
                      ckv_ref, kpe_ref, kbf_ref, vbf_ref, *scratch, prompt, tiles_per_seq):
    half = A_ROPE // 2
    for part in range(PROJ_PARTS):
        rows = slice(part * PROJ_ROWS, (part + 1) * PROJ_ROWS)
        h = _rms(x_ref[rows, :], g_ref[...]).astype(BF16)
        proj = _dot(h, win_ref[...])
        cq = _rms(proj[:, 0:256], qn_ref[...]).astype(BF16)
        ckv = _rms(proj[:, 256:384], kvn_ref[...])
        c, s1, s2 = c_ref[rows, :], s1_ref[rows, :], s2_ref[rows, :]
        kpe = _rope_group(proj[:, 384:512], c, s1, s2, half)
        ckv_ref[rows, :] = ckv
        kpe_ref[rows, :] = pltpu.roll(kpe, LANES - A_NOPE, 1)[:, 0:A_ROPE]
        qa = _dot(cq, wuq_ref[...])
        kv = _dot(ckv.astype(BF16), wukv_ref[...])
        for hd in range(A_HEADS):
            sl = slice(hd * LANES, (hd + 1) * LANES)
            qa_ref[rows, sl] = (_rope_group(qa[:, sl], c, s1, s2, half) * (A_SCALE * LOG2E)).astype(BF16)
            ka_ref[rows, sl] = (kv[:, sl] + kpe).astype(BF16)
        for pair in range(B_HEADS // 2):
            grp = proj[:, 512 + pair * LANES:512 + (pair + 1) * LANES] * (HEAD_SCALE * LOG2E)
            for sub in range(2):
                qb_ref[2 * pair + sub, rows, :] = jnp.where(_lane_mask(sub * HEAD_DIM), grp, 0.0).astype(BF16)
        va = kv[:, A_HEADS * LANES:]
        kb = proj[:, 1024:1536]
        vb = proj[:, 1536:2048]
        kb_ref[rows, :] = kb.astype(BF16)
        if prompt:
            va_ref[:, rows] = va.T.astype(BF16)
            vb_ref[:, rows] = vb.T.astype(BF16)
            scratch[0][rows, :] = kb
            scratch[1][rows, :] = vb
        else:
            va_ref[rows, :] = va.astype(BF16)
            vb_ref[rows, :] = vb.astype(BF16)
            kbf_ref[rows, :] = kb
            vbf_ref[rows, :] = vb
    if prompt:
        @pl.when(pl.program_id(0) % tiles_per_seq == tiles_per_seq - 1)
        def _():
            kbf_ref[...] = scratch[0][...]
            vbf_ref[...] = scratch[1][...]


def _last_tile_index(tiles_per_seq):
    return lambda i: (jnp.maximum((i + 1) // tiles_per_seq - 1, 0), 0)


def _even_proj(x, g, win, qn, kvn, wuq, wukv, tabs, prompt, batch, seq):
    t = x.shape[0]
    tiles_per_seq = seq // TOKEN_TILE if prompt else 1

    def tile(w):
        return pl.BlockSpec((TOKEN_TILE, w), lambda i: (i, 0))

    def tile_t(w):
        return pl.BlockSpec((w, TOKEN_TILE), lambda i: (0, i))

    def sds(shape, dtype):
        return jax.ShapeDtypeStruct(shape, dtype)

    wa = A_HEADS * LANES
    if prompt:
        v_shapes = [sds((A_HEADS * A_V, t), BF16), sds((B_WIDTH, t), BF16)]
        v_specs = [tile_t(A_HEADS * A_V), tile_t(B_WIDTH)]
        keep_shape = sds((batch * TOKEN_TILE, B_WIDTH), F32)
        keep_spec = pl.BlockSpec((TOKEN_TILE, B_WIDTH), _last_tile_index(tiles_per_seq))
        tab_idx = lambda i: (i % tiles_per_seq, 0)
    else:
        v_shapes = [sds((t, A_HEADS * A_V), BF16), sds((t, B_WIDTH), BF16)]
        v_specs = [tile(A_HEADS * A_V), tile(B_WIDTH)]
        keep_shape = sds((t, B_WIDTH), F32)
        keep_spec = tile(B_WIDTH)
        tab_idx = lambda i: (0, 0)
    qb_spec = pl.BlockSpec((B_HEADS, TOKEN_TILE, LANES), lambda i: (0, i, 0))
    out_shape = [sds((t, wa), BF16), sds((t, wa), BF16), v_shapes[0], sds((B_HEADS, t, LANES), BF16),
                 sds((t, B_WIDTH), BF16), v_shapes[1], sds((t, A_KV_LORA), F32), sds((t, A_ROPE), F32),
                 keep_shape, keep_shape]
    out_specs = [tile(wa), tile(wa), v_specs[0], qb_spec, tile(B_WIDTH), v_specs[1],
                 tile(A_KV_LORA), tile(A_ROPE), keep_spec, keep_spec]
    return pl.pallas_call(
        functools.partial(_even_proj_kernel, prompt=prompt, tiles_per_seq=tiles_per_seq),
        out_shape=out_shape,
        grid=(t // TOKEN_TILE,),
        in_specs=[tile(D_MODEL), _const_spec((1, D_MODEL)), _const_spec(win.shape),
                  _const_spec((1, A_Q_LORA)), _const_spec((1, A_KV_LORA)),
                  _const_spec(wuq.shape), _const_spec(wukv.shape)]
                 + [pl.BlockSpec((TOKEN_TILE, LANES), tab_idx)] * 3,
        out_specs=out_specs,
        scratch_shapes=[pltpu.VMEM((TOKEN_TILE, B_WIDTH), F32)] * 2 if prompt else [],
        compiler_params=_params("arbitrary"),
        name="even_proj",
    )(x, g, win, qn, kvn, wuq, wukv, *tabs)


def _odd_proj_kernel(x_ref, g_ref, win_ref, c_ref, s1_ref, s2_ref,
                     q_ref, k_ref, v_ref, kf_ref, vf_ref, *scratch, prompt, tiles_per_seq):
    half = C_ROT // 2
    group = C_HEADS // C_KV_HEADS
    for part in range(PROJ_PARTS):
        rows = slice(part * PROJ_ROWS, (part + 1) * PROJ_ROWS)
        h = _rms(x_ref[rows, :], g_ref[...]).astype(BF16)
        proj = _dot(h, win_ref[...])
        c, s1, s2 = c_ref[rows, :], s1_ref[rows, :], s2_ref[rows, :]
        k = _rope_group(proj[:, C_QWIDTH:C_QWIDTH + LANES], c, s1, s2, half)
        v = proj[:, C_QWIDTH + LANES:C_QWIDTH + 2 * LANES]
        k_ref[rows, :] = k.astype(BF16)
        if prompt:
            for pair in range(C_HEADS // 2):
                grp = _rope_group(proj[:, pair * LANES:(pair + 1) * LANES], c, s1, s2, half) * (HEAD_SCALE * LOG2E)
                swapped = pltpu.roll(grp, HEAD_DIM, 1)
                for sub in range(2):
                    hd = 2 * pair + sub
                    kv = hd // group
                    src = grp if sub == kv else swapped
                    q_ref[hd, rows, :] = jnp.where(_lane_mask(kv * HEAD_DIM), src, 0.0).astype(BF16)
            v_ref[:, rows] = v.T.astype(BF16)
            if part == PROJ_PARTS - 1:
                scratch[0][...] = k[PROJ_ROWS - C_WINDOW:, :]
                scratch[1][...] = v[PROJ_ROWS - C_WINDOW:, :]
        else:
            for grp in range(C_QWIDTH // LANES):
                sl = slice(grp * LANES, (grp + 1) * LANES)
                q_ref[rows, sl] = (_rope_group(proj[:, sl], c, s1, s2, half) * (HEAD_SCALE * LOG2E)).astype(BF16)
            v_ref[rows, :] = v.astype(BF16)
            kf_ref[rows, :] = k
            vf_ref[rows, :] = v
    if prompt:
        @pl.when(pl.program_id(0) % tiles_per_seq == tiles_per_seq - 1)
        def _():
            kf_ref[...] = scratch[0][...]
            vf_ref[...] = scratch[1][...]


def _odd_proj(x, g, win, tabs, prompt, batch, seq):
    t = x.shape[0]
    tiles_per_seq = seq // TOKEN_TILE if prompt else 1

    def tile(w):
        return pl.BlockSpec((TOKEN_TILE, w), lambda i: (i, 0))

    def sds(shape, dtype):
        return jax.ShapeDtypeStruct(shape, dtype)

    if prompt:
        q_shape = sds((C_HEADS, t, LANES), BF16)
        q_spec = pl.BlockSpec((C_HEADS, TOKEN_TILE, LANES), lambda i: (0, i, 0))
        v_shape, v_spec = sds((LANES, t), BF16), pl.BlockSpec((LANES, TOKEN_TILE), lambda i: (0, i))
        keep_shape = sds((batch * C_WINDOW, LANES), F32)
        keep_spec = pl.BlockSpec((C_WINDOW, LANES), _last_tile_index(tiles_per_seq))
        tab_idx = lambda i: (i % tiles_per_seq, 0)
    else:
        q_shape, q_spec = sds((t, C_QWIDTH), BF16), tile(C_QWIDTH)
        v_shape, v_spec = sds((t, LANES), BF16), tile(LANES)
        keep_shape, keep_spec = sds((t, LANES), F32), tile(LANES)
        tab_idx = lambda i: (0, 0)
    return pl.pallas_call(
        functools.partial(_odd_proj_kernel, prompt=prompt, tiles_per_seq=tiles_per_seq),
        out_shape=[q_shape, sds((t, LANES), BF16), v_shape, keep_shape, keep_shape],
        grid=(t // TOKEN_TILE,),
        in_specs=[tile(D_MODEL), _const_spec((1, D_MODEL)), _const_spec(win.shape)]
                 + [pl.BlockSpec((TOKEN_TILE, LANES), tab_idx)] * 3,
        out_specs=[q_spec, tile(LANES), v_spec, keep_spec, keep_spec],
        scratch_shapes=[pltpu.VMEM((C_WINDOW, LANES), F32)] * 2 if prompt else [],
        compiler_params=_params("arbitrary"),
        name="odd_proj",
    )(x, g, win, *tabs)


def _band_bias_kernel(tab_ref, o_ref, ot_ref):
    hd = pl.program_id(0)
    w = B_REACH + BAND_TILE
    row = lax.broadcasted_iota(jnp.int32, (BAND_TILE, w), 0)
    col = lax.broadcasted_iota(jnp.int32, (BAND_TILE, w), 1)
    lo = (row // CHUNK) * CHUNK
    visible = (col >= lo) & (col < lo + B_REACH + CHUNK)
    n = BIAS_DIAG_LANES
    e = lax.broadcasted_iota(jnp.int32, (1, n), 1)
    rel = jnp.clip(B_REACH + BAND_TILE - 1 - e, -B_MAX_REL, B_MAX_REL) + B_MAX_REL

    def body(r, acc):
        return jnp.where(rel == r, tab_ref[hd, r], acc)

    diag = lax.fori_loop(0, 2 * B_MAX_REL + 1, body, jnp.zeros((1, n), F32))
    shifted = pltpu.roll(jnp.broadcast_to(diag, (BAND_TILE, n)), n - (BAND_TILE - 1), 1, stride=1, stride_axis=0)
    bias = jnp.where(visible, shifted[:, 0:w] * LOG2E, NEG_INF)
    o_ref[...] = bias
    ot_ref[...] = bias.T


def _band_bias(table):
    w = B_REACH + BAND_TILE
    return pl.pallas_call(
        _band_bias_kernel,
        out_shape=[jax.ShapeDtypeStruct((B_HEADS, BAND_TILE, w), F32),
                   jax.ShapeDtypeStruct((B_HEADS, w, BAND_TILE), F32)],
        grid=(B_HEADS,),
        in_specs=[pl.BlockSpec(memory_space=pltpu.SMEM)],
        out_specs=[pl.BlockSpec((None, BAND_TILE, w), lambda h: (h, 0, 0)),
                   pl.BlockSpec((None, w, BAND_TILE), lambda h: (h, 0, 0))],
        compiler_params=_params("arbitrary"),
        name="band_bias",
    )(table)


def _ones_rows(n):
    return jnp.ones((ONES_ROWS, n), BF16)


def _mla_prompt_kernel(q_ref, k_ref, vt_ref, o_ref, m_ref, l_ref, acc_ref, s0_ref, s1_ref):
    i = pl.program_id(1)
    tq, tk = MLA_Q_TILE, MLA_K_TILE
    m_ref[...] = jnp.full(m_ref.shape, NEG_INF, F32)
    l_ref[...] = jnp.zeros(l_ref.shape, F32)
    acc_ref[...] = jnp.zeros(acc_ref.shape, F32)
    kc = lax.broadcasted_iota(jnp.int32, (tk, tq), 0) // CHUNK
    qc = lax.broadcasted_iota(jnp.int32, (tk, tq), 1) // CHUNK

    per_q = tq // tk
    assert per_q == 2
    s_refs = (s0_ref, s1_ref)

    def score(j, slot, hd):
        ks = pl.multiple_of(j * tk, tk)
        sl = slice(hd * LANES, (hd + 1) * LANES)
        s_refs[slot][hd] = _dot_nt(k_ref[pl.ds(ks, tk), sl], q_ref[:, sl])

    def block(j, slot, diag, stage_next):
        ks = pl.multiple_of(j * tk, tk)
        m_all, l_all, acc_all = m_ref[...], l_ref[...], acc_ref[...]
        ones = _ones_rows(tk)
        new = []
        for hd in range(A_HEADS):
            if stage_next:
                score(j + 1, 1 - slot, hd)
            s = s_refs[slot][hd]
            if diag is not None:
                s = jnp.where(kc + diag * (tk // CHUNK) <= qc, s, NEG_INF)
            m_new = jnp.maximum(m_all[hd], jnp.max(s, axis=0, keepdims=True))
            alpha = jnp.exp2(m_all[hd] - m_new)
            p = jnp.exp2(s - m_new).astype(BF16)
            vt = jnp.concatenate([vt_ref[hd * A_V:(hd + 1) * A_V, pl.ds(ks, tk)], ones], axis=0)
            pv = _dot(vt, p)
            new.append((m_new, alpha * l_all[hd] + pv[A_V:A_V + 1], alpha * acc_all[hd] + pv[0:A_V]))
        for hd, (m_new, l_new, acc_new) in enumerate(new):
            m_ref[hd] = m_new
            l_ref[hd] = l_new
            acc_ref[hd] = acc_new

    for hd in range(A_HEADS):
        score(0, 0, hd)

    def body(pair, carry):
        for d in range(per_q):
            block(pair * per_q + d, d, None, True)
        return carry

    lax.fori_loop(0, i, body, 0)
    for d in range(per_q):
        block(i * per_q + d, d, d, d + 1 < per_q)
    outs = [acc_ref[hd] / l_ref[hd] for hd in range(A_HEADS)]
    o_ref[...] = jnp.concatenate(outs, axis=0).T.astype(BF16)


def _mla_prompt(qa, ka, vat, batch, seq):
    nq = seq // MLA_Q_TILE
    wa = A_HEADS * LANES
    wv = A_HEADS * A_V
    return pl.pallas_call(
        _mla_prompt_kernel,
        out_shape=jax.ShapeDtypeStruct((batch * seq, wv), BF16),
        grid=(batch, nq),
        in_specs=[pl.BlockSpec((MLA_Q_TILE, wa), lambda b, i: (b * nq + i, 0)),
                  pl.BlockSpec((seq, wa), lambda b, i: (b, 0)),
                  pl.BlockSpec((wv, seq), lambda b, i: (0, b))],
        out_specs=pl.BlockSpec((MLA_Q_TILE, wv), lambda b, i: (b * nq + i, 0)),
        scratch_shapes=[pltpu.VMEM((A_HEADS, 1, MLA_Q_TILE), F32), pltpu.VMEM((A_HEADS, 1, MLA_Q_TILE), F32),
                        pltpu.VMEM((A_HEADS, A_V, MLA_Q_TILE), F32),
                        pltpu.VMEM((A_HEADS, MLA_K_TILE, MLA_Q_TILE), F32),
                        pltpu.VMEM((A_HEADS, MLA_K_TILE, MLA_Q_TILE), F32)],
        compiler_params=_params("parallel", "arbitrary"),
        name="mla_prompt",
    )(qa, ka, vat)


def _fill_padded(kpad, vtpad, k_ref, vt_ref, reach):
    kpad[0:reach, :] = jnp.zeros((reach, kpad.shape[1]), kpad.dtype)
    kpad[reach:, :] = k_ref[...]
    vtpad[:, 0:reach] = jnp.zeros((vtpad.shape[0], reach), vtpad.dtype)
    vtpad[:, reach:] = vt_ref[...]


def _window_attend(s, vt, sink=None):
    m = jnp.max(s, axis=0, keepdims=True)
    if sink is not None:
        m = jnp.maximum(m, sink)
    p = jnp.exp2(s - m).astype(BF16)
    d = vt.shape[0]
    pv = _dot(jnp.concatenate([vt, _ones_rows(vt.shape[1])], axis=0), p)
    l = pv[d:d + 1]
    if sink is not None:
        l = l + jnp.exp2(sink - m)
    return pv[0:d] * (1.0 / l)


def _band_prompt_kernel(q_ref, k_ref, vt_ref, bias_ref, o_ref, kpad, vtpad, s0_ref, s1_ref):
    i = pl.program_id(1)
    tq = BAND_TILE
    s_refs = (s0_ref, s1_ref)

    @pl.when(i == 0)
    def _():
        _fill_padded(kpad, vtpad, k_ref, vt_ref, B_REACH)

    w = B_REACH + tq
    n_pairs = B_HEADS // 2
    step = BAND_BLOCKS * tq

    def attend(mask_start):
        key = lax.broadcasted_iota(jnp.int32, (w, 2 * tq), 0)

        def first_row(blk):
            return pl.multiple_of((i * BAND_BLOCKS + blk) * tq, tq)

        def score(unit):
            blk, pair = divmod(unit, n_pairs)
            q2 = q_ref[2 * pair:2 * pair + 2, blk * tq:(blk + 1) * tq, :].reshape(2 * tq, LANES)
            s_refs[unit % 2][...] = _dot_nt(kpad[pl.ds(first_row(blk), w), pair * LANES:(pair + 1) * LANES], q2)

        n_units = BAND_BLOCKS * n_pairs
        score(0)
        for unit in range(n_units):
            if unit + 1 < n_units:
                score(unit + 1)
            blk, pair = divmod(unit, n_pairs)
            if pair == 0:
                outs = []
            s = s_refs[unit % 2][...] + bias_ref[pair]
            if mask_start:
                s = jnp.where(key >= B_REACH - first_row(blk), s, NEG_INF)
            o2 = _window_attend(s, vtpad[pair * LANES:(pair + 1) * LANES, pl.ds(first_row(blk), w)])
            outs.append(o2[0:HEAD_DIM, 0:tq])
            outs.append(o2[HEAD_DIM:, tq:])
            if pair == n_pairs - 1:
                o_ref[blk * tq:(blk + 1) * tq, :] = jnp.concatenate(outs, axis=0).T.astype(BF16)

    pl.when(i * step < B_REACH)(functools.partial(attend, True))
    pl.when(i * step >= B_REACH)(functools.partial(attend, False))


def _band_prompt(qb, kb, vbt, bias_pairs, batch, seq):
    step = BAND_BLOCKS * BAND_TILE
    nq = seq // step
    return pl.pallas_call(
        _band_prompt_kernel,
        out_shape=jax.ShapeDtypeStruct((batch * seq, B_WIDTH), BF16),
        grid=(batch, nq),
        in_specs=[pl.BlockSpec((B_HEADS, step, LANES), lambda b, i: (0, b * nq + i, 0)),
                  pl.BlockSpec((seq, B_WIDTH), lambda b, i: (b, 0)),
                  pl.BlockSpec((B_WIDTH, seq), lambda b, i: (0, b)),
                  _const_spec(bias_pairs.shape)],
        out_specs=pl.BlockSpec((step, B_WIDTH), lambda b, i: (b * nq + i, 0)),
        scratch_shapes=[pltpu.VMEM((seq + B_REACH, B_WIDTH), BF16), pltpu.VMEM((B_WIDTH, seq + B_REACH), BF16)]
                       + [pltpu.VMEM((B_REACH + BAND_TILE, 2 * BAND_TILE), F32)] * 2,
        compiler_params=_params("parallel", "arbitrary"),
        name="band_prompt",
    )(qb, kb, vbt, bias_pairs)


def _swa_prompt_kernel(sink_ref, q_ref, k_ref, vt_ref, o_ref, kpad, vtpad, s0_ref, s1_ref, *, layer):
    i = pl.program_id(1)
    s_refs = (s0_ref, s1_ref)

    @pl.when(i == 0)
    def _():
        _fill_padded(kpad, vtpad, k_ref, vt_ref, C_WINDOW)

    w = C_WINDOW + ATT_TILE
    group = C_HEADS // C_KV_HEADS
    key = lax.broadcasted_iota(jnp.int32, (w, ATT_TILE), 0)
    qry = lax.broadcasted_iota(jnp.int32, (w, ATT_TILE), 1)
    lo = (qry // CHUNK) * CHUNK
    in_window = (key >= lo) & (key < lo + C_WINDOW + CHUNK)

    def score(unit):
        blk, kv = divmod(unit, C_KV_HEADS)
        start = pl.multiple_of((i * SWA_BLOCKS + blk) * ATT_TILE, ATT_TILE)
        q = q_ref[kv * group:(kv + 1) * group, blk * ATT_TILE:(blk + 1) * ATT_TILE, :]
        s_refs[unit % 2][...] = _dot_nt(kpad[pl.ds(start, w), :], q.reshape(group * ATT_TILE, LANES))

    n_units = SWA_BLOCKS * C_KV_HEADS
    score(0)
    for unit in range(n_units):
        if unit + 1 < n_units:
            score(unit + 1)
        blk, kv = divmod(unit, C_KV_HEADS)
        if kv == 0:
            first = (i * SWA_BLOCKS + blk) * ATT_TILE
            start = pl.multiple_of(first, ATT_TILE)
            visible = in_window & (key >= C_WINDOW - first)
            mask = jnp.concatenate([jnp.where(visible, 0.0, NEG_INF)] * group, axis=1)
            outs = []
        sink = jnp.concatenate([jnp.full((1, ATT_TILE), sink_ref[layer, kv * group + g] * LOG2E, F32)
                                for g in range(group)], axis=1)
        og = _window_attend(s_refs[unit % 2][...] + mask, vtpad[kv * HEAD_DIM:(kv + 1) * HEAD_DIM, pl.ds(start, w)],
                            sink)
        outs.extend(og[:, g * ATT_TILE:(g + 1) * ATT_TILE] for g in range(group))
        if kv == C_KV_HEADS - 1:
            o_ref[blk * ATT_TILE:(blk + 1) * ATT_TILE, :] = jnp.concatenate(outs, axis=0).T.astype(BF16)


def _swa_prompt(sinks, layer, q, k, vt, batch, seq):
    step = SWA_BLOCKS * ATT_TILE
    nq = seq // step
    return pl.pallas_call(
        functools.partial(_swa_prompt_kernel, layer=layer),
        out_shape=jax.ShapeDtypeStruct((batch * seq, C_QWIDTH), BF16),
        grid=(batch, nq),
        in_specs=[pl.BlockSpec(memory_space=pltpu.SMEM),
                  pl.BlockSpec((C_HEADS, step, LANES), lambda b, i: (0, b * nq + i, 0)),
                  pl.BlockSpec((seq, C_KWIDTH), lambda b, i: (b, 0)),
                  pl.BlockSpec((C_KWIDTH, seq), lambda b, i: (0, b))],
        out_specs=pl.BlockSpec((step, C_QWIDTH), lambda b, i: (b * nq + i, 0)),
        scratch_shapes=[pltpu.VMEM((seq + C_WINDOW, C_KWIDTH), BF16), pltpu.VMEM((C_KWIDTH, seq + C_WINDOW), BF16)]
                       + [pltpu.VMEM((C_WINDOW + ATT_TILE, C_HEADS // C_KV_HEADS * ATT_TILE), F32)] * 2,
        compiler_params=_params("parallel", "arbitrary"),
        name="swa_prompt",
    )(sinks, q, k, vt)


def _softmax_pv2(s_c, s_n, v_c, v_n, sink=None):
    m = jnp.maximum(jnp.max(s_c, axis=-1, keepdims=True), jnp.max(s_n, axis=-1, keepdims=True))
    if sink is not None:
        m = jnp.maximum(m, sink)
    p_c = jnp.exp2(s_c - m)
    p_n = jnp.exp2(s_n - m)
    l = jnp.sum(p_c, axis=-1, keepdims=True) + jnp.sum(p_n, axis=-1, keepdims=True)
    if sink is not None:
        l = l + jnp.exp2(sink - m)
    return (_dot(p_c.astype(BF16), v_c) + _dot(p_n.astype(BF16), v_n)) / l


def _even_sample_kernel(qa_ref, ka_ref, va_ref, qb_ref, kb_ref, vb_ref,
                        cckv_ref, ckpe_ref, cbk_ref, cbv_ref, wukv_ref, place_ref, bias_ref,
                        oa_ref, ob_ref):
    t = qa_ref.shape[0]
    nb = cbk_ref.shape[0]
    kvc = _dot(cckv_ref[...].astype(BF16), wukv_ref[...])
    kpe = _dot(ckpe_ref[...].astype(BF16), place_ref[...])
    scores_a, scores_b = [], []
    for hd in range(A_HEADS):
        sl = slice(hd * LANES, (hd + 1) * LANES)
        k_c = (kvc[:, sl] + kpe).astype(BF16)
        q = qa_ref[:, sl]
        scores_a.append((_dot_nt(q, k_c), _dot_nt(q, ka_ref[:, sl])))
    for hd in range(B_HEADS):
        pair = slice((hd // 2) * LANES, (hd // 2 + 1) * LANES)
        q = qb_ref[hd]
        scores_b.append((_dot_nt(q, cbk_ref[:, pair].astype(BF16)) + bias_ref[hd, 0:t, 0:nb],
                         _dot_nt(q, kb_ref[:, pair]) + bias_ref[hd, 0:t, nb:nb + t]))
    outs_a, outs_b = [], []
    for hd in range(A_HEADS):
        vsl = slice(hd * A_V, (hd + 1) * A_V)
        v_c = kvc[:, A_HEADS * LANES + hd * A_V:A_HEADS * LANES + (hd + 1) * A_V].astype(BF16)
        outs_a.append(_softmax_pv2(*scores_a[hd], v_c, va_ref[:, vsl]))
    for hd in range(B_HEADS):
        sl = slice(hd * HEAD_DIM, (hd + 1) * HEAD_DIM)
        outs_b.append(_softmax_pv2(*scores_b[hd], cbv_ref[:, sl].astype(BF16), vb_ref[:, sl]))
    oa_ref[...] = jnp.concatenate(outs_a, axis=-1).astype(BF16)
    ob_ref[...] = jnp.concatenate(outs_b, axis=-1).astype(BF16)


def _even_sample(layer, qa, ka, va, qb, kb, vb, c_ckv, c_kpe, c_bk, c_bv, wukv, place, bias, dec_batch, t):
    past = c_ckv.shape[2]
    nb = c_bk.shape[2]

    def new(w):
        return pl.BlockSpec((t, w), lambda b: (b, 0))

    def cache(n, w):
        return pl.BlockSpec((None, None, n, w), lambda b: (layer, b, 0, 0))

    return pl.pallas_call(
        _even_sample_kernel,
        out_shape=[jax.ShapeDtypeStruct((dec_batch * t, B_WIDTH), BF16)] * 2,
        grid=(dec_batch,),
        in_specs=[new(A_HEADS * LANES), new(A_HEADS * LANES), new(A_HEADS * A_V),
                  pl.BlockSpec((B_HEADS, t, LANES), lambda b: (0, b, 0)), new(B_WIDTH), new(B_WIDTH),
                  cache(past, A_KV_LORA), cache(past, A_ROPE), cache(nb, B_WIDTH), cache(nb, B_WIDTH),
                  _const_spec(wukv.shape), _const_spec(place.shape), _const_spec(bias.shape)],
        out_specs=[new(B_WIDTH), new(B_WIDTH)],
        compiler_params=_params("parallel"),
        name="even_sample",
    )(qa, ka, va, qb, kb, vb, c_ckv, c_kpe, c_bk, c_bv, wukv, place, bias)


def _odd_sample_kernel(sink_ref, q_ref, k_ref, v_ref, ck_ref, cv_ref, o_ref, *, layer):
    group = C_HEADS // C_KV_HEADS
    scores = []
    for hd in range(C_HEADS):
        ksl = slice((hd // group) * HEAD_DIM, (hd // group + 1) * HEAD_DIM)
        q = q_ref[:, hd * HEAD_DIM:(hd + 1) * HEAD_DIM]
        scores.append((_dot_nt(q, ck_ref[:, ksl].astype(BF16)), _dot_nt(q, k_ref[:, ksl])))
    outs = []
    for hd in range(C_HEADS):
        ksl = slice((hd // group) * HEAD_DIM, (hd // group + 1) * HEAD_DIM)
        outs.append(_softmax_pv2(*scores[hd], cv_ref[:, ksl].astype(BF16), v_ref[:, ksl],
                                 sink_ref[layer, hd] * LOG2E))
    o_ref[...] = jnp.concatenate(outs, axis=-1).astype(BF16)


def _odd_sample(sinks, layer, q, k, v, c_k, c_v, dec_batch, t):
    nc = c_k.shape[2]

    def new(w):
        return pl.BlockSpec((t, w), lambda b: (b, 0))

    def cache():
        return pl.BlockSpec((None, None, nc, C_KWIDTH), lambda b: (layer, b, 0, 0))

    return pl.pallas_call(
        functools.partial(_odd_sample_kernel, layer=layer),
        out_shape=jax.ShapeDtypeStruct((dec_batch * t, C_QWIDTH), BF16),
        grid=(dec_batch,),
        in_specs=[pl.BlockSpec(memory_space=pltpu.SMEM), new(C_QWIDTH), new(C_KWIDTH), new(C_KWIDTH),
                  cache(), cache()],
        out_specs=new(C_QWIDTH),
        compiler_params=_params("parallel"),
        name="odd_sample",
    )(sinks, q, k, v, c_k, c_v)


def _rope_parts(pos, n_rot):
    half = n_rot // 2
    inv = ROPE_THETA ** (-jnp.arange(half, dtype=F32) / half)
    ang = pos.astype(F32)[:, None] * inv[None, :]
    return jnp.cos(ang), jnp.sin(ang)


def _rope_tables(pos, n_rot, pre, width):
    cos, sin = _rope_parts(pos, n_rot)
    n = pos.shape[0]
    post = width - pre - n_rot

    def head(first, second, fill):
        return jnp.concatenate([jnp.full((n, pre), fill, F32), first, second, jnp.full((n, post), fill, F32)], axis=1)

    zero = jnp.zeros_like(sin)
    tabs = (head(cos, cos, 1.0), head(zero, sin, 0.0), head(-sin, zero, 0.0))
    return tuple(jnp.tile(t, (1, LANES // width)) for t in tabs)


def kernel(x_prompt, x_sample, cache_mla_ckv, cache_mla_kpe, cache_band_k, cache_band_v,
           cache_swa_k, cache_swa_v, norm_g, ffn_w_gate, ffn_w_up, ffn_w_down, even_w_in,
           mla_q_norm, mla_w_uq, mla_kv_norm, mla_w_ukv, band_rel_bias, even_w_out,
           odd_w_in, swa_sinks, odd_w_out):
    batch, seq, _ = x_prompt.shape
    dec_batch, t_new, _ = x_sample.shape
    depth = norm_g.shape[0]
    n_even = even_w_in.shape[0]
    n_odd = odd_w_in.shape[0]
    past = cache_mla_ckv.shape[2]
    n_p = batch * seq
    n_s = dec_batch * t_new
    assert seq % TOKEN_TILE == 0 and n_s % TOKEN_TILE == 0 and TOKEN_TILE % t_new == 0
    assert n_p % FFN_TILE == 0 and n_s % FFN_TILE == 0
    assert seq % MLA_Q_TILE == 0 and seq % (BAND_BLOCKS * BAND_TILE) == 0 and seq % (SWA_BLOCKS * ATT_TILE) == 0 and min(B_REACH, seq) == TOKEN_TILE and t_new <= CHUNK
    assert cache_band_k.shape[2] == B_REACH and cache_swa_k.shape[2] == C_WINDOW and past >= B_REACH

    pos_p = jnp.arange(seq, dtype=jnp.int32)
    pos_s = past + (jnp.arange(TOKEN_TILE, dtype=jnp.int32) % t_new)
    mla_tabs = {True: _rope_tables(pos_p, A_ROPE, A_NOPE, LANES), False: _rope_tables(pos_s, A_ROPE, A_NOPE, LANES)}
    swa_tabs = {True: _rope_tables(pos_p, C_ROT, 0, HEAD_DIM), False: _rope_tables(pos_s, C_ROT, 0, HEAD_DIM)}

    wg = ffn_w_gate.astype(BF16)
    wu = ffn_w_up.astype(BF16)
    wd = ffn_w_down.astype(BF16)

    place = jnp.zeros((A_ROPE, LANES), F32).at[jnp.arange(A_ROPE), A_NOPE + jnp.arange(A_ROPE)].set(1.0).astype(BF16)
    c_bk = cache_band_k.reshape(n_even, dec_batch, B_REACH, B_WIDTH)
    c_bv = cache_band_v.reshape(n_even, dec_batch, B_REACH, B_WIDTH)
    c_sk = cache_swa_k.reshape(n_odd, dec_batch, C_WINDOW, C_KWIDTH)
    c_sv = cache_swa_v.reshape(n_odd, dec_batch, C_WINDOW, C_KWIDTH)

    c0 = A_Q_LORA + A_KV_LORA
    c1 = c0 + A_ROPE
    xs = {True: x_prompt.reshape(n_p, D_MODEL), False: x_sample.reshape(n_s, D_MODEL)}
    new = {True: [[] for _ in range(6)], False: [[] for _ in range(6)]}
    for l in range(depth):
        g = norm_g[l]
        i = l // 2
        if l % 2 == 0:
            w = even_w_in[i].astype(BF16)
            zc = lambda n: jnp.zeros((D_MODEL, n), BF16)
            win = jnp.concatenate([w[:, :c0], zc(A_NOPE), w[:, c0:c1], zc(LANES - A_NOPE - A_ROPE), w[:, c1:]], axis=1)
            wuq = jnp.pad(mla_w_uq[i].reshape(A_Q_LORA, A_HEADS, A_QK),
                          ((0, 0), (0, 0), (0, LANES - A_QK))).reshape(A_Q_LORA, A_HEADS * LANES).astype(BF16)
            wkv = mla_w_ukv[i].reshape(A_KV_LORA, A_HEADS, A_NOPE + A_V)
            wk = jnp.pad(wkv[:, :, :A_NOPE], ((0, 0), (0, 0), (0, LANES - A_NOPE))).reshape(A_KV_LORA, A_HEADS * LANES)
            wv = wkv[:, :, A_NOPE:].reshape(A_KV_LORA, A_HEADS * A_V)
            wukv = jnp.concatenate([wk, wv], axis=1).astype(BF16)
            bias, bias_t = _band_bias(band_rel_bias[i])
            w_band = B_REACH + BAND_TILE
            bias_pairs = bias_t.reshape(B_HEADS // 2, 2, w_band, BAND_TILE).transpose(0, 2, 1, 3).reshape(
                B_HEADS // 2, w_band, 2 * BAND_TILE)
            wo = even_w_out[i].astype(BF16)
        else:
            win = odd_w_in[i].astype(BF16)
            wo = odd_w_out[i].astype(BF16)
        for prompt in (True, False):
            x = _ffn(xs[prompt], g[0:2], wg, wu, wd, l, 0)
            st = new[prompt]
            if l % 2 == 0:
                qa, ka, va, qb, kb, vb, ckv, kpe, kbf, vbf = _even_proj(
                    x, g[2:3], win, mla_q_norm[i][None], mla_kv_norm[i][None], wuq, wukv, mla_tabs[prompt],
                    prompt, batch, seq)
                if prompt:
                    mixes = [_mla_prompt(qa, ka, va, batch, seq), _band_prompt(qb, kb, vb, bias_pairs, batch, seq)]
                    lead = (batch, seq)
                else:
                    mixes = _even_sample(i, qa, ka, va, qb, kb, vb, cache_mla_ckv, cache_mla_kpe, c_bk, c_bv,
                                         wukv, place, bias, dec_batch, t_new)
                    lead = (dec_batch, t_new)
                st[0].append(ckv.reshape(*lead, A_KV_LORA))
                st[1].append(kpe.reshape(*lead, A_ROPE))
                st[2].append(kbf.reshape(lead[0], -1, B_HEADS, HEAD_DIM))
                st[3].append(vbf.reshape(lead[0], -1, B_HEADS, HEAD_DIM))
            else:
                q, k, v, kf, vf = _odd_proj(x, g[2:3], win, swa_tabs[prompt], prompt, batch, seq)
                if prompt:
                    mixes = [_swa_prompt(swa_sinks, i, q, k, v, batch, seq)]
                    lead = batch
                else:
                    mixes = [_odd_sample(swa_sinks, i, q, k, v, c_sk, c_sv, dec_batch, t_new)]
                    lead = dec_batch
                st[4].append(kf.reshape(lead, -1, C_KV_HEADS, HEAD_DIM))
                st[5].append(vf.reshape(lead, -1, C_KV_HEADS, HEAD_DIM))
            xs[prompt] = _mix_ffn(x, mixes, wo, g[3:4], g[4:6], wg, wu, wd, l, 1)

    y_prompt = xs[True].reshape(batch, seq, D_MODEL)
    y_sample = xs[False].reshape(dec_batch, t_new, D_MODEL)
    st_p = [jnp.stack(s, axis=0) for s in new[True]]
    st_s = [jnp.stack(s, axis=0) for s in new[False]]
    return (y_prompt, y_sample, *st_p, *st_s)
```

```python
import functools
import math

import jax
import jax.numpy as jnp
from jax import lax
from jax.experimental import pallas as pl
from jax.experimental.pallas import tpu as pltpu

F32 = jnp.float32
BF16 = jnp.bfloat16

D_MODEL = 1024
D_FF = 2816
CHUNK = 64
HEAD_DIM = 64
ROPE_THETA = 500000.0
NORM_EPS = 1e-6
NEG_INF = -1e30
A_HEADS = 8
A_Q_LORA = 256
A_KV_LORA = 128
A_NOPE = 64
A_ROPE = 32
A_QK = A_NOPE + A_ROPE
A_V = 64
A_SCALE = A_QK ** -0.5
B_HEADS = 8
B_REACH = 512
B_MAX_REL = 128
C_HEADS = 16
C_KV_HEADS = 2
C_WINDOW = 128
C_ROT = HEAD_DIM // 4
HEAD_SCALE = HEAD_DIM ** -0.5
LOG2E = math.log2(math.e)

LANES = 128
TOKEN_TILE = 512
FFN_TILE = 1024
FFN_PARTS = 2
FFN_ROWS = FFN_TILE // FFN_PARTS
FF_TILE = 256
PROJ_PARTS = 2
PROJ_ROWS = TOKEN_TILE // PROJ_PARTS
ATT_TILE = 128
BAND_TILE = 256
BAND_BLOCKS = 2
MLA_Q_TILE = 512
MLA_K_TILE = 256
BIAS_DIAG_LANES = 1024
SWA_BLOCKS = 4
ONES_ROWS = 16
VMEM_LIMIT = 56 * 1024 * 1024
B_WIDTH = B_HEADS * HEAD_DIM
C_QWIDTH = C_HEADS * HEAD_DIM
C_KWIDTH = C_KV_HEADS * HEAD_DIM


def _params(*sem):
    return pltpu.CompilerParams(dimension_semantics=sem, vmem_limit_bytes=VMEM_LIMIT)


def _const_spec(shape, index=None):
    index = index or (0,) * len(shape)
    return pl.BlockSpec(shape, lambda *_: index, pipeline_mode=pl.Buffered(1))


def _rms(x, g):
    ms = jnp.mean(x * x, axis=-1, keepdims=True)
    return x * lax.rsqrt(ms + NORM_EPS) * g


def _dot(a, b):
    return jnp.dot(a, b, preferred_element_type=F32)


def _dot_nt(a, b):
    return lax.dot_general(a, b, (((1,), (1,)), ((), ())), preferred_element_type=F32)


def _rope_group(x, c, s1, s2, half):
    return x * c + pltpu.roll(x, half, 1) * s1 + pltpu.roll(x, LANES - half, 1) * s2


def _lane_mask(lo):
    lane = lax.broadcasted_iota(jnp.int32, (1, LANES), 1)
    return (lane >= lo) & (lane < lo + HEAD_DIM)


def _swiglu_update(x, g_ref, wg_ref, wu_ref, wd_ref):
    xn = _rms(x, g_ref[0:1, :]).astype(BF16)
    acc = jnp.zeros(x.shape, F32)
    for c in range(D_FF // FF_TILE):
        sl = slice(c * FF_TILE, (c + 1) * FF_TILE)
        a = _dot(xn, wg_ref[:, sl])
        b = _dot(xn, wu_ref[:, sl])
        h = (a * jax.nn.sigmoid(a) * b).astype(BF16)
        acc = acc + _dot(h, wd_ref[sl, :])
    return x + 0.5 * _rms(acc, g_ref[1:2, :])


def _ffn_row_parts():
    return [slice(p * FFN_ROWS, (p + 1) * FFN_ROWS) for p in range(FFN_PARTS)]


def _ffn_kernel(x_ref, g_ref, wg_ref, wu_ref, wd_ref, o_ref):
    for rows in _ffn_row_parts():
        o_ref[rows, :] = _swiglu_update(x_ref[rows, :], g_ref, wg_ref, wu_ref, wd_ref)


def _mix_ffn_kernel(*refs, n_mix):
    x_ref = refs[0]
    mix_refs = refs[1:1 + n_mix]
    wo_ref, gm_ref, g_ref, wg_ref, wu_ref, wd_ref, o_ref = refs[1 + n_mix:]
    ys = []
    for rows in _ffn_row_parts():
        y = None
        row = 0
        for m_ref in mix_refs:
            w = m_ref.shape[1]
            part = _dot(m_ref[rows, :], wo_ref[row:row + w, :])
            y = part if y is None else y + part
            row += w
        ys.append(y)
    for rows, y in zip(_ffn_row_parts(), ys):
        x = x_ref[rows, :] + _rms(y, gm_ref[...])
        o_ref[rows, :] = _swiglu_update(x, g_ref, wg_ref, wu_ref, wd_ref)


def _ffn_weight_specs(layer, which):
    idx = (layer, which, 0, 0)
    return [_const_spec((None, None, D_MODEL, D_FF), idx), _const_spec((None, None, D_MODEL, D_FF), idx),
            _const_spec((None, None, D_FF, D_MODEL), idx)]


def _ffn(x, g, wg, wu, wd, layer, which):
    t = x.shape[0]
    tile = pl.BlockSpec((FFN_TILE, D_MODEL), lambda i: (i, 0))
    return pl.pallas_call(
        _ffn_kernel,
        out_shape=jax.ShapeDtypeStruct((t, D_MODEL), F32),
        grid=(t // FFN_TILE,),
        in_specs=[tile, _const_spec((2, D_MODEL))] + _ffn_weight_specs(layer, which),
        out_specs=tile,
        compiler_params=_params("parallel"),
        name="ffn",
    )(x, g, wg, wu, wd)


def _mix_ffn(x, mixes, wo, gm, g, wg, wu, wd, layer, which):
    t = x.shape[0]
    tile = pl.BlockSpec((FFN_TILE, D_MODEL), lambda i: (i, 0))
    mix_specs = [pl.BlockSpec((FFN_TILE, m.shape[1]), lambda i: (i, 0)) for m in mixes]
    return pl.pallas_call(
        functools.partial(_mix_ffn_kernel, n_mix=len(mixes)),
        out_shape=jax.ShapeDtypeStruct((t, D_MODEL), F32),
        grid=(t // FFN_TILE,),
        in_specs=[tile] + mix_specs + [_const_spec(wo.shape), _const_spec((1, D_MODEL)), _const_spec((2, D_MODEL))]
                 + _ffn_weight_specs(layer, which),
        out_specs=tile,
        compiler_params=_params("parallel"),
        name="mix_ffn",
    )(x, *mixes, wo, gm, g, wg, wu, wd)


def _even_proj_kernel(x_ref, g_ref, win_ref, qn_ref, kvn_ref, wuq_ref, wukv_ref,
                      c_ref, s1_ref, s2_ref,
                      qa_ref, ka_ref, va_ref, qb_ref, kb_ref, vb_ref,
                      ckv_ref, kpe_ref, kbf_ref, vbf_ref, *scratch, prompt, tiles_per_seq):
    half = A_ROPE // 2
    for part in range(PROJ_PARTS):
        rows = slice(part * PROJ_ROWS, (part + 1) * PROJ_ROWS)
        h = _rms(x_ref[rows, :], g_ref[...]).astype(BF16)
        proj = _dot(h, win_ref[...])
        cq = _rms(proj[:, 0:256], qn_ref[...]).astype(BF16)
        ckv = _rms(proj[:, 256:384], kvn_ref[...])
        c, s1, s2 = c_ref[rows, :], s1_ref[rows, :], s2_ref[rows, :]
        kpe = _rope_group(proj[:, 384:512], c, s1, s2, half)
        ckv_ref[rows, :] = ckv
        kpe_ref[rows, :] = pltpu.roll(kpe, LANES - A_NOPE, 1)[:, 0:A_ROPE]
        qa = _dot(cq, wuq_ref[...])
        kv = _dot(ckv.astype(BF16), wukv_ref[...])
        for hd in range(A_HEADS):
            sl = slice(hd * LANES, (hd + 1) * LANES)
            qa_ref[rows, sl] = (_rope_group(qa[:, sl], c, s1, s2, half) * (A_SCALE * LOG2E)).astype(BF16)
            ka_ref[rows, sl] = (kv[:, sl] + kpe).astype(BF16)
        for pair in range(B_HEADS // 2):
            grp = proj[:, 512 + pair * LANES:512 + (pair + 1) * LANES] * (HEAD_SCALE * LOG2E)
            for sub in range(2):
                qb_ref[2 * pair + sub, rows, :] = jnp.where(_lane_mask(sub * HEAD_DIM), grp, 0.0).astype(BF16)
        va = kv[:, A_HEADS * LANES:]
        kb = proj[:, 1024:1536]
        vb = proj[:, 1536:2048]
        kb_ref[rows, :] = kb.astype(BF16)
        if prompt:
            va_ref[:, rows] = va.T.astype(BF16)
            vb_ref[:, rows] = vb.T.astype(BF16)
            scratch[0][rows, :] = kb
            scratch[1][rows, :] = vb
        else:
            va_ref[rows, :] = va.astype(BF16)
            vb_ref[rows, :] = vb.astype(BF16)
            kbf_ref[rows, :] = kb
            vbf_ref[rows, :] = vb
    if prompt:
        @pl.when(pl.program_id(0) % tiles_per_seq == tiles_per_seq - 1)
        def _():
            kbf_ref[...] = scratch[0][...]
            vbf_ref[...] = scratch[1][...]


def _last_tile_index(tiles_per_seq):
    return lambda i: (jnp.maximum((i + 1) // tiles_per_seq - 1, 0), 0)


def _even_proj(x, g, win, qn, kvn, wuq, wukv, tabs, prompt, batch, seq):
    t = x.shape[0]
    tiles_per_seq = seq // TOKEN_TILE if prompt else 1

    def tile(w):
        return pl.BlockSpec((TOKEN_TILE, w), lambda i: (i, 0))

    def tile_t(w):
        return pl.BlockSpec((w, TOKEN_TILE), lambda i: (0, i))

    def sds(shape, dtype):
        return jax.ShapeDtypeStruct(shape, dtype)

    wa = A_HEADS * LANES
    if prompt:
        v_shapes = [sds((A_HEADS * A_V, t), BF16), sds((B_WIDTH, t), BF16)]
        v_specs = [tile_t(A_HEADS * A_V), tile_t(B_WIDTH)]
        keep_shape = sds((batch * TOKEN_TILE, B_WIDTH), F32)
        keep_spec = pl.BlockSpec((TOKEN_TILE, B_WIDTH), _last_tile_index(tiles_per_seq))
        tab_idx = lambda i: (i % tiles_per_seq, 0)
    else:
        v_shapes = [sds((t, A_HEADS * A_V), BF16), sds((t, B_WIDTH), BF16)]
        v_specs = [tile(A_HEADS * A_V), tile(B_WIDTH)]
        keep_shape = sds((t, B_WIDTH), F32)
        keep_spec = tile(B_WIDTH)
        tab_idx = lambda i: (0, 0)
    qb_spec = pl.BlockSpec((B_HEADS, TOKEN_TILE, LANES), lambda i: (0, i, 0))
    out_shape = [sds((t, wa), BF16), sds((t, wa), BF16), v_shapes[0], sds((B_HEADS, t, LANES), BF16),
                 sds((t, B_WIDTH), BF16), v_shapes[1], sds((t, A_KV_LORA), F32), sds((t, A_ROPE), F32),
                 keep_shape, keep_shape]
    out_specs = [tile(wa), tile(wa), v_specs[0], qb_spec, tile(B_WIDTH), v_specs[1],
                 tile(A_KV_LORA), tile(A_ROPE), keep_spec, keep_spec]
    return pl.pallas_call(
        functools.partial(_even_proj_kernel, prompt=prompt, tiles_per_seq=tiles_per_seq),
        out_shape=out_shape,
        grid=(t // TOKEN_TILE,),
        in_specs=[tile(D_MODEL), _const_spec((1, D_MODEL)), _const_spec(win.shape),
                  _const_spec((1, A_Q_LORA)), _const_spec((1, A_KV_LORA)),
                  _const_spec(wuq.shape), _const_spec(wukv.shape)]
                 + [pl.BlockSpec((TOKEN_TILE, LANES), tab_idx)] * 3,
        out_specs=out_specs,
        scratch_shapes=[pltpu.VMEM((TOKEN_TILE, B_WIDTH), F32)] * 2 if prompt else [],
        compiler_params=_params("arbitrary"),
        name="even_proj",
    )(x, g, win, qn, kvn, wuq, wukv, *tabs)


def _odd_proj_kernel(x_ref, g_ref, win_ref, c_ref, s1_ref, s2_ref,
                     q_ref, k_ref, v_ref, kf_ref, vf_ref, *scratch, prompt, tiles_per_seq):
    half = C_ROT // 2
    group = C_HEADS // C_KV_HEADS
    for part in range(PROJ_PARTS):
        rows = slice(part * PROJ_ROWS, (part + 1) * PROJ_ROWS)
        h = _rms(x_ref[rows, :], g_ref[...]).astype(BF16)
        proj = _dot(h, win_ref[...])
        c, s1, s2 = c_ref[rows, :], s1_ref[rows, :], s2_ref[rows, :]
        k = _rope_group(proj[:, C_QWIDTH:C_QWIDTH + LANES], c, s1, s2, half)
        v = proj[:, C_QWIDTH + LANES:C_QWIDTH + 2 * LANES]
        k_ref[rows, :] = k.astype(BF16)
        if prompt:
            for pair in range(C_HEADS // 2):
                grp = _rope_group(proj[:, pair * LANES:(pair + 1) * LANES], c, s1, s2, half) * (HEAD_SCALE * LOG2E)
                swapped = pltpu.roll(grp, HEAD_DIM, 1)
                for sub in range(2):
                    hd = 2 * pair + sub
                    kv = hd // group
                    src = grp if sub == kv else swapped
                    q_ref[hd, rows, :] = jnp.where(_lane_mask(kv * HEAD_DIM), src, 0.0).astype(BF16)
            v_ref[:, rows] = v.T.astype(BF16)
            if part == PROJ_PARTS - 1:
                scratch[0][...] = k[PROJ_ROWS - C_WINDOW:, :]
                scratch[1][...] = v[PROJ_ROWS - C_WINDOW:, :]
        else:
            for grp in range(C_QWIDTH // LANES):
                sl = slice(grp * LANES, (grp + 1) * LANES)
                q_ref[rows, sl] = (_rope_group(proj[:, sl], c, s1, s2, half) * (HEAD_SCALE * LOG2E)).astype(BF16)
            v_ref[rows, :] = v.astype(BF16)
            kf_ref[rows, :] = k
            vf_ref[rows, :] = v
    if prompt:
        @pl.when(pl.program_id(0) % tiles_per_seq == tiles_per_seq - 1)
        def _():
            kf_ref[...] = scratch[0][...]
            vf_ref[...] = scratch[1][...]


def _odd_proj(x, g, win, tabs, prompt, batch, seq):
    t = x.shape[0]
    tiles_per_seq = seq // TOKEN_TILE if prompt else 1

    def tile(w):
        return pl.BlockSpec((TOKEN_TILE, w), lambda i: (i, 0))

    def sds(shape, dtype):
        return jax.ShapeDtypeStruct(shape, dtype)

    if prompt:
        q_shape = sds((C_HEADS, t, LANES), BF16)
        q_spec = pl.BlockSpec((C_HEADS, TOKEN_TILE, LANES), lambda i: (0, i, 0))
        v_shape, v_spec = sds((LANES, t), BF16), pl.BlockSpec((LANES, TOKEN_TILE), lambda i: (0, i))
        keep_shape = sds((batch * C_WINDOW, LANES), F32)
        keep_spec = pl.BlockSpec((C_WINDOW, LANES), _last_tile_index(tiles_per_seq))
        tab_idx = lambda i: (i % tiles_per_seq, 0)
    else:
        q_shape, q_spec = sds((t, C_QWIDTH), BF16), tile(C_QWIDTH)
        v_shape, v_spec = sds((t, LANES), BF16), tile(LANES)
        keep_shape, keep_spec = sds((t, LANES), F32), tile(LANES)
        tab_idx = lambda i: (0, 0)
    return pl.pallas_call(
        functools.partial(_odd_proj_kernel, prompt=prompt, tiles_per_seq=tiles_per_seq),
        out_shape=[q_shape, sds((t, LANES), BF16), v_shape, keep_shape, keep_shape],
        grid=(t // TOKEN_TILE,),
        in_specs=[tile(D_MODEL), _const_spec((1, D_MODEL)), _const_spec(win.shape)]
                 + [pl.BlockSpec((TOKEN_TILE, LANES), tab_idx)] * 3,
        out_specs=[q_spec, tile(LANES), v_spec, keep_spec, keep_spec],
        scratch_shapes=[pltpu.VMEM((C_WINDOW, LANES), F32)] * 2 if prompt else [],
        compiler_params=_params("arbitrary"),
        name="odd_proj",
    )(x, g, win, *tabs)


def _band_bias_kernel(tab_ref, o_ref, ot_ref):
    hd = pl.program_id(0)
    w = B_REACH + BAND_TILE
    row = lax.broadcasted_iota(jnp.int32, (BAND_TILE, w), 0)
    col = lax.broadcasted_iota(jnp.int32, (BAND_TILE, w), 1)
    lo = (row // CHUNK) * CHUNK
    visible = (col >= lo) & (col < lo + B_REACH + CHUNK)
    n = BIAS_DIAG_LANES
    e = lax.broadcasted_iota(jnp.int32, (1, n), 1)
    rel = jnp.clip(B_REACH + BAND_TILE - 1 - e, -B_MAX_REL, B_MAX_REL) + B_MAX_REL

    def body(r, acc):
        return jnp.where(rel == r, tab_ref[hd, r], acc)

    diag = lax.fori_loop(0, 2 * B_MAX_REL + 1, body, jnp.zeros((1, n), F32))
    shifted = pltpu.roll(jnp.broadcast_to(diag, (BAND_TILE, n)), n - (BAND_TILE - 1), 1, stride=1, stride_axis=0)
    bias = jnp.where(visible, shifted[:, 0:w] * LOG2E, NEG_INF)
    o_ref[...] = bias
    ot_ref[...] = bias.T


def _band_bias(table):
    w = B_REACH + BAND_TILE
    return pl.pallas_call(
        _band_bias_kernel,
        out_shape=[jax.ShapeDtypeStruct((B_HEADS, BAND_TILE, w), F32),
                   jax.ShapeDtypeStruct((B_HEADS, w, BAND_TILE), F32)],
        grid=(B_HEADS,),
        in_specs=[pl.BlockSpec(memory_space=pltpu.SMEM)],
        out_specs=[pl.BlockSpec((None, BAND_TILE, w), lambda h: (h, 0, 0)),
                   pl.BlockSpec((None, w, BAND_TILE), lambda h: (h, 0, 0))],
        compiler_params=_params("arbitrary"),
        name="band_bias",
    )(table)


def _ones_rows(n):
    return jnp.ones((ONES_ROWS, n), BF16)


def _mla_prompt_kernel(q_ref, k_ref, vt_ref, o_ref, m_ref, l_ref, acc_ref, s0_ref, s1_ref, smax0_ref, smax1_ref):
    i = pl.program_id(1)
    tq, tk = MLA_Q_TILE, MLA_K_TILE
    m_ref[...] = jnp.full(m_ref.shape, NEG_INF, F32)
    l_ref[...] = jnp.zeros(l_ref.shape, F32)
    acc_ref[...] = jnp.zeros(acc_ref.shape, F32)
    kc = lax.broadcasted_iota(jnp.int32, (tk, tq), 0) // CHUNK
    qc = lax.broadcasted_iota(jnp.int32, (tk, tq), 1) // CHUNK

    per_q = tq // tk
    assert per_q == 2
    s_refs = (s0_ref, s1_ref)
    smax_refs = (smax0_ref, smax1_ref)

    def score(j, slot, hd):
        ks = pl.multiple_of(j * tk, tk)
        sl = slice(hd * LANES, (hd + 1) * LANES)
        s = _dot_nt(k_ref[pl.ds(ks, tk), sl], q_ref[:, sl])
        s_refs[slot][hd] = s
        smax_refs[slot][hd] = jnp.max(s, axis=0, keepdims=True)

    def block(j, slot, diag, stage_next):
        ks = pl.multiple_of(j * tk, tk)
        m_all, l_all, acc_all = m_ref[...], l_ref[...], acc_ref[...]
        ones = _ones_rows(tk)
        new = []
        for hd in range(A_HEADS):
            if stage_next:
                score(j + 1, 1 - slot, hd)
            s = s_refs[slot][hd]
            if diag is None:
                s_max = smax_refs[slot][hd]
            else:
                s = jnp.where(kc + diag * (tk // CHUNK) <= qc, s, NEG_INF)
                s_max = jnp.max(s, axis=0, keepdims=True)
            m_new = jnp.maximum(m_all[hd], s_max)
            alpha = jnp.exp2(m_all[hd] - m_new)
            p = jnp.exp2(s - m_new).astype(BF16)
            vt = jnp.concatenate([vt_ref[hd * A_V:(hd + 1) * A_V, pl.ds(ks, tk)], ones], axis=0)
            pv = _dot(vt, p)
            new.append((m_new, alpha * l_all[hd] + pv[A_V:A_V + 1], alpha * acc_all[hd] + pv[0:A_V]))
        for hd, (m_new, l_new, acc_new) in enumerate(new):
            m_ref[hd] = m_new
            l_ref[hd] = l_new
            acc_ref[hd] = acc_new

    for hd in range(A_HEADS):
        score(0, 0, hd)

    def body(pair, carry):
        for d in range(per_q):
            block(pair * per_q + d, d, None, True)
        return carry

    lax.fori_loop(0, i, body, 0)
    for d in range(per_q):
        block(i * per_q + d, d, d, d + 1 < per_q)
    outs = [acc_ref[hd] / l_ref[hd] for hd in range(A_HEADS)]
    o_ref[...] = jnp.concatenate(outs, axis=0).T.astype(BF16)


def _mla_prompt(qa, ka, vat, batch, seq):
    nq = seq // MLA_Q_TILE
    wa = A_HEADS * LANES
    wv = A_HEADS * A_V
    return pl.pallas_call(
        _mla_prompt_kernel,
        out_shape=jax.ShapeDtypeStruct((batch * seq, wv), BF16),
        grid=(batch, nq),
        in_specs=[pl.BlockSpec((MLA_Q_TILE, wa), lambda b, i: (b * nq + i, 0)),
                  pl.BlockSpec((seq, wa), lambda b, i: (b, 0)),
                  pl.BlockSpec((wv, seq), lambda b, i: (0, b))],
        out_specs=pl.BlockSpec((MLA_Q_TILE, wv), lambda b, i: (b * nq + i, 0)),
        scratch_shapes=[pltpu.VMEM((A_HEADS, 1, MLA_Q_TILE), F32), pltpu.VMEM((A_HEADS, 1, MLA_Q_TILE), F32),
                        pltpu.VMEM((A_HEADS, A_V, MLA_Q_TILE), F32),
                        pltpu.VMEM((A_HEADS, MLA_K_TILE, MLA_Q_TILE), F32),
                        pltpu.VMEM((A_HEADS, MLA_K_TILE, MLA_Q_TILE), F32),
                        pltpu.VMEM((A_HEADS, 1, MLA_Q_TILE), F32), pltpu.VMEM((A_HEADS, 1, MLA_Q_TILE), F32)],
        compiler_params=_params("parallel", "arbitrary"),
        name="mla_prompt",
    )(qa, ka, vat)


def _fill_padded(kpad, vtpad, k_ref, vt_ref, reach):
    kpad[0:reach, :] = jnp.zeros((reach, kpad.shape[1]), kpad.dtype)
    kpad[reach:, :] = k_ref[...]
    vtpad[:, 0:reach] = jnp.zeros((vtpad.shape[0], reach), vtpad.dtype)
    vtpad[:, reach:] = vt_ref[...]


def _window_attend(s, vt, sink=None):
    m = jnp.max(s, axis=0, keepdims=True)
    if sink is not None:
        m = jnp.maximum(m, sink)
    p = jnp.exp2(s - m).astype(BF16)
    d = vt.shape[0]
    pv = _dot(jnp.concatenate([vt, _ones_rows(vt.shape[1])], axis=0), p)
    l = pv[d:d + 1]
    if sink is not None:
        l = l + jnp.exp2(sink - m)
    return pv[0:d] * (1.0 / l)


def _band_prompt_kernel(q_ref, k_ref, vt_ref, bias_ref, o_ref, kpad, vtpad, s0_ref, s1_ref):
    i = pl.program_id(1)
    tq = BAND_TILE
    s_refs = (s0_ref, s1_ref)

    @pl.when(i == 0)
    def _():
        _fill_padded(kpad, vtpad, k_ref, vt_ref, B_REACH)

    w = B_REACH + tq
    n_pairs = B_HEADS // 2
    step = BAND_BLOCKS * tq

    def attend(mask_start):
        key = lax.broadcasted_iota(jnp.int32, (w, 2 * tq), 0)

        def first_row(blk):
            return pl.multiple_of((i * BAND_BLOCKS + blk) * tq, tq)

        def score(unit):
            blk, pair = divmod(unit, n_pairs)
            q2 = q_ref[2 * pair:2 * pair + 2, blk * tq:(blk + 1) * tq, :].reshape(2 * tq, LANES)
            s_refs[unit % 2][...] = _dot_nt(kpad[pl.ds(first_row(blk), w), pair * LANES:(pair + 1) * LANES], q2)

        n_units = BAND_BLOCKS * n_pairs
        score(0)
        for unit in range(n_units):
            if unit + 1 < n_units:
                score(unit + 1)
            blk, pair = divmod(unit, n_pairs)
            if pair == 0:
                outs = []
            s = s_refs[unit % 2][...] + bias_ref[pair]
            if mask_start:
                s = jnp.where(key >= B_REACH - first_row(blk), s, NEG_INF)
            o2 = _window_attend(s, vtpad[pair * LANES:(pair + 1) * LANES, pl.ds(first_row(blk), w)])
            outs.append(o2[0:HEAD_DIM, 0:tq])
            outs.append(o2[HEAD_DIM:, tq:])
            if pair == n_pairs - 1:
                o_ref[blk * tq:(blk + 1) * tq, :] = jnp.concatenate(outs, axis=0).T.astype(BF16)

    pl.when(i * step < B_REACH)(functools.partial(attend, True))
    pl.when(i * step >= B_REACH)(functools.partial(attend, False))


def _band_prompt(qb, kb, vbt, bias_pairs, batch, seq):
    step = BAND_BLOCKS * BAND_TILE
    nq = seq // step
    return pl.pallas_call(
        _band_prompt_kernel,
        out_shape=jax.ShapeDtypeStruct((batch * seq, B_WIDTH), BF16),
        grid=(batch, nq),
        in_specs=[pl.BlockSpec((B_HEADS, step, LANES), lambda b, i: (0, b * nq + i, 0)),
                  pl.BlockSpec((seq, B_WIDTH), lambda b, i: (b, 0)),
                  pl.BlockSpec((B_WIDTH, seq), lambda b, i: (0, b)),
                  _const_spec(bias_pairs.shape)],
        out_specs=pl.BlockSpec((step, B_WIDTH), lambda b, i: (b * nq + i, 0)),
        scratch_shapes=[pltpu.VMEM((seq + B_REACH, B_WIDTH), BF16), pltpu.VMEM((B_WIDTH, seq + B_REACH), BF16)]
                       + [pltpu.VMEM((B_REACH + BAND_TILE, 2 * BAND_TILE), F32)] * 2,
        compiler_params=_params("parallel", "arbitrary"),
        name="band_prompt",
    )(qb, kb, vbt, bias_pairs)


def _swa_prompt_kernel(sink_ref, q_ref, k_ref, vt_ref, o_ref, kpad, vtpad, s0_ref, s1_ref, *, layer):
    i = pl.program_id(1)
    s_refs = (s0_ref, s1_ref)

    @pl.when(i == 0)
    def _():
        _fill_padded(kpad, vtpad, k_ref, vt_ref, C_WINDOW)

    w = C_WINDOW + ATT_TILE
    group = C_HEADS // C_KV_HEADS
    key = lax.broadcasted_iota(jnp.int32, (w, ATT_TILE), 0)
    qry = lax.broadcasted_iota(jnp.int32, (w, ATT_TILE), 1)
    lo = (qry // CHUNK) * CHUNK
    in_window = (key >= lo) & (key < lo + C_WINDOW + CHUNK)

    def score(unit):
        blk, kv = divmod(unit, C_KV_HEADS)
        start = pl.multiple_of((i * SWA_BLOCKS + blk) * ATT_TILE, ATT_TILE)
        q = q_ref[kv * group:(kv + 1) * group, blk * ATT_TILE:(blk + 1) * ATT_TILE, :]
        s_refs[unit % 2][...] = _dot_nt(kpad[pl.ds(start, w), :], q.reshape(group * ATT_TILE, LANES))

    n_units = SWA_BLOCKS * C_KV_HEADS
    score(0)
    for unit in range(n_units):
        if unit + 1 < n_units:
            score(unit + 1)
        blk, kv = divmod(unit, C_KV_HEADS)
        if kv == 0:
            first = (i * SWA_BLOCKS + blk) * ATT_TILE
            start = pl.multiple_of(first, ATT_TILE)
            visible = in_window & (key >= C_WINDOW - first)
            mask = jnp.concatenate([jnp.where(visible, 0.0, NEG_INF)] * group, axis=1)
            outs = []
        sink = jnp.concatenate([jnp.full((1, ATT_TILE), sink_ref[layer, kv * group + g] * LOG2E, F32)
                                for g in range(group)], axis=1)
        og = _window_attend(s_refs[unit % 2][...] + mask, vtpad[kv * HEAD_DIM:(kv + 1) * HEAD_DIM, pl.ds(start, w)],
                            sink)
        outs.extend(og[:, g * ATT_TILE:(g + 1) * ATT_TILE] for g in range(group))
        if kv == C_KV_HEADS - 1:
            o_ref[blk * ATT_TILE:(blk + 1) * ATT_TILE, :] = jnp.concatenate(outs, axis=0).T.astype(BF16)


def _swa_prompt(sinks, layer, q, k, vt, batch, seq):
    step = SWA_BLOCKS * ATT_TILE
    nq = seq // step
    return pl.pallas_call(
        functools.partial(_swa_prompt_kernel, layer=layer),
        out_shape=jax.ShapeDtypeStruct((batch * seq, C_QWIDTH), BF16),
        grid=(batch, nq),
        in_specs=[pl.BlockSpec(memory_space=pltpu.SMEM),
                  pl.BlockSpec((C_HEADS, step, LANES), lambda b, i: (0, b * nq + i, 0)),
                  pl.BlockSpec((seq, C_KWIDTH), lambda b, i: (b, 0)),
                  pl.BlockSpec((C_KWIDTH, seq), lambda b, i: (0, b))],
        out_specs=pl.BlockSpec((step, C_QWIDTH), lambda b, i: (b * nq + i, 0)),
        scratch_shapes=[pltpu.VMEM((seq + C_WINDOW, C_KWIDTH), BF16), pltpu.VMEM((C_KWIDTH, seq + C_WINDOW), BF16)]
                       + [pltpu.VMEM((C_WINDOW + ATT_TILE, C_HEADS // C_KV_HEADS * ATT_TILE), F32)] * 2,
        compiler_params=_params("parallel", "arbitrary"),
        name="swa_prompt",
    )(sinks, q, k, vt)


def _softmax_pv2(s_c, s_n, v_c, v_n, sink=None, v_c_transposed=False):
    m = jnp.maximum(jnp.max(s_c, axis=-1, keepdims=True), jnp.max(s_n, axis=-1, keepdims=True))
    if sink is not None:
        m = jnp.maximum(m, sink)
    p_c = jnp.exp2(s_c - m)
    p_n = jnp.exp2(s_n - m)
    l = jnp.sum(p_c, axis=-1, keepdims=True) + jnp.sum(p_n, axis=-1, keepdims=True)
    if sink is not None:
        l = l + jnp.exp2(sink - m)
    pv_c = _dot_nt(p_c.astype(BF16), v_c) if v_c_transposed else _dot(p_c.astype(BF16), v_c)
    return (pv_c + _dot(p_n.astype(BF16), v_n)) / l


def _even_sample_kernel(qa_ref, ka_ref, va_ref, qb_ref, kb_ref, vb_ref,
                        cckv_ref, ckpe_ref, cbk_ref, cbv_ref, wukv_ref, pick_ref, bias_ref,
                        oa_ref, ob_ref):
    t = qa_ref.shape[0]
    nb = cbk_ref.shape[1]
    kvc = _dot(cckv_ref[...].astype(BF16), wukv_ref[...])
    kpe_t = ckpe_ref[...].astype(BF16)
    scores_a, scores_b = [], []
    for hd in range(A_HEADS):
        sl = slice(hd * LANES, (hd + 1) * LANES)
        q = qa_ref[:, sl]
        q_rope = _dot(q, pick_ref[...]).astype(BF16)
        s_c = _dot_nt(q, kvc[:, sl].astype(BF16)) + _dot(q_rope, kpe_t)
        scores_a.append((s_c, _dot_nt(q, ka_ref[:, sl])))
    for hd in range(B_HEADS):
        pair = slice((hd // 2) * LANES, (hd // 2 + 1) * LANES)
        q = qb_ref[hd]
        scores_b.append((_dot(q, cbk_ref[pair, :].astype(BF16)) + bias_ref[hd, 0:t, 0:nb],
                         _dot_nt(q, kb_ref[:, pair]) + bias_ref[hd, 0:t, nb:nb + t]))
    outs_a, outs_b = [], []
    for hd in range(A_HEADS):
        vsl = slice(hd * A_V, (hd + 1) * A_V)
        v_c = kvc[:, A_HEADS * LANES + hd * A_V:A_HEADS * LANES + (hd + 1) * A_V].astype(BF16)
        outs_a.append(_softmax_pv2(*scores_a[hd], v_c, va_ref[:, vsl]))
    for hd in range(B_HEADS):
        sl = slice(hd * HEAD_DIM, (hd + 1) * HEAD_DIM)
        outs_b.append(_softmax_pv2(*scores_b[hd], cbv_ref[sl, :].astype(BF16), vb_ref[:, sl], v_c_transposed=True))
    oa_ref[...] = jnp.concatenate(outs_a, axis=-1).astype(BF16)
    ob_ref[...] = jnp.concatenate(outs_b, axis=-1).astype(BF16)


def _even_sample(layer, qa, ka, va, qb, kb, vb, c_ckv, c_kpe_t, c_bk_t, c_bv_t, wukv, pick, bias, dec_batch, t):
    past = c_ckv.shape[2]
    nb = c_bk_t.shape[3]

    def new(w):
        return pl.BlockSpec((t, w), lambda b: (b, 0))

    def cache(n, w):
        return pl.BlockSpec((None, None, n, w), lambda b: (layer, b, 0, 0))

    return pl.pallas_call(
        _even_sample_kernel,
        out_shape=[jax.ShapeDtypeStruct((dec_batch * t, B_WIDTH), BF16)] * 2,
        grid=(dec_batch,),
        in_specs=[new(A_HEADS * LANES), new(A_HEADS * LANES), new(A_HEADS * A_V),
                  pl.BlockSpec((B_HEADS, t, LANES), lambda b: (0, b, 0)), new(B_WIDTH), new(B_WIDTH),
                  cache(past, A_KV_LORA), cache(A_ROPE, past), cache(B_WIDTH, nb), cache(B_WIDTH, nb),
                  _const_spec(wukv.shape), _const_spec(pick.shape), _const_spec(bias.shape)],
        out_specs=[new(B_WIDTH), new(B_WIDTH)],
        compiler_params=_params("parallel"),
        name="even_sample",
    )(qa, ka, va, qb, kb, vb, c_ckv, c_kpe_t, c_bk_t, c_bv_t, wukv, pick, bias)


def _odd_sample_kernel(sink_ref, q_ref, k_ref, v_ref, ck_ref, cv_ref, o_ref, *, layer):
    group = C_HEADS // C_KV_HEADS
    scores = []
    for hd in range(C_HEADS):
        ksl = slice((hd // group) * HEAD_DIM, (hd // group + 1) * HEAD_DIM)
        q = q_ref[:, hd * HEAD_DIM:(hd + 1) * HEAD_DIM]
        scores.append((_dot_nt(q, ck_ref[:, ksl].astype(BF16)), _dot_nt(q, k_ref[:, ksl])))
    outs = []
    for hd in range(C_HEADS):
        ksl = slice((hd // group) * HEAD_DIM, (hd // group + 1) * HEAD_DIM)
        outs.append(_softmax_pv2(*scores[hd], cv_ref[:, ksl].astype(BF16), v_ref[:, ksl],
                                 sink_ref[layer, hd] * LOG2E))
    o_ref[...] = jnp.concatenate(outs, axis=-1).astype(BF16)


def _odd_sample(sinks, layer, q, k, v, c_k, c_v, dec_batch, t):
    nc = c_k.shape[2]

    def new(w):
        return pl.BlockSpec((t, w), lambda b: (b, 0))

    def cache():
        return pl.BlockSpec((None, None, nc, C_KWIDTH), lambda b: (layer, b, 0, 0))

    return pl.pallas_call(
        functools.partial(_odd_sample_kernel, layer=layer),
        out_shape=jax.ShapeDtypeStruct((dec_batch * t, C_QWIDTH), BF16),
        grid=(dec_batch,),
        in_specs=[pl.BlockSpec(memory_space=pltpu.SMEM), new(C_QWIDTH), new(C_KWIDTH), new(C_KWIDTH),
                  cache(), cache()],
        out_specs=new(C_QWIDTH),
        compiler_params=_params("parallel"),
        name="odd_sample",
    )(sinks, q, k, v, c_k, c_v)


def _rope_parts(pos, n_rot):
    half = n_rot // 2
    inv = ROPE_THETA ** (-jnp.arange(half, dtype=F32) / half)
    ang = pos.astype(F32)[:, None] * inv[None, :]
    return jnp.cos(ang), jnp.sin(ang)


def _rope_tables(pos, n_rot, pre, width):
    cos, sin = _rope_parts(pos, n_rot)
    n = pos.shape[0]
    post = width - pre - n_rot

    def head(first, second, fill):
        return jnp.concatenate([jnp.full((n, pre), fill, F32), first, second, jnp.full((n, post), fill, F32)], axis=1)

    zero = jnp.zeros_like(sin)
    tabs = (head(cos, cos, 1.0), head(zero, sin, 0.0), head(-sin, zero, 0.0))
    return tuple(jnp.tile(t, (1, LANES // width)) for t in tabs)


def kernel(x_prompt, x_sample, cache_mla_ckv, cache_mla_kpe, cache_band_k, cache_band_v,
           cache_swa_k, cache_swa_v, norm_g, ffn_w_gate, ffn_w_up, ffn_w_down, even_w_in,
           mla_q_norm, mla_w_uq, mla_kv_norm, mla_w_ukv, band_rel_bias, even_w_out,
           odd_w_in, swa_sinks, odd_w_out):
    batch, seq, _ = x_prompt.shape
    dec_batch, t_new, _ = x_sample.shape
    depth = norm_g.shape[0]
    n_even = even_w_in.shape[0]
    n_odd = odd_w_in.shape[0]
    past = cache_mla_ckv.shape[2]
    n_p = batch * seq
    n_s = dec_batch * t_new
    assert seq % TOKEN_TILE == 0 and n_s % TOKEN_TILE == 0 and TOKEN_TILE % t_new == 0
    assert n_p % FFN_TILE == 0 and n_s % FFN_TILE == 0
    assert seq % MLA_Q_TILE == 0 and seq % (BAND_BLOCKS * BAND_TILE) == 0 and seq % (SWA_BLOCKS * ATT_TILE) == 0 and min(B_REACH, seq) == TOKEN_TILE and t_new <= CHUNK
    assert cache_band_k.shape[2] == B_REACH and cache_swa_k.shape[2] == C_WINDOW and past >= B_REACH

    pos_p = jnp.arange(seq, dtype=jnp.int32)
    pos_s = past + (jnp.arange(TOKEN_TILE, dtype=jnp.int32) % t_new)
    mla_tabs = {True: _rope_tables(pos_p, A_ROPE, A_NOPE, LANES), False: _rope_tables(pos_s, A_ROPE, A_NOPE, LANES)}
    swa_tabs = {True: _rope_tables(pos_p, C_ROT, 0, HEAD_DIM), False: _rope_tables(pos_s, C_ROT, 0, HEAD_DIM)}

    wg = ffn_w_gate.astype(BF16)
    wu = ffn_w_up.astype(BF16)
    wd = ffn_w_down.astype(BF16)

    pick = jnp.zeros((LANES, A_ROPE), F32).at[A_NOPE + jnp.arange(A_ROPE), jnp.arange(A_ROPE)].set(1.0).astype(BF16)
    c_kpe_t = cache_mla_kpe.transpose(0, 1, 3, 2)
    c_bk_t = cache_band_k.transpose(0, 1, 3, 4, 2).reshape(n_even, dec_batch, B_WIDTH, B_REACH)
    c_bv_t = cache_band_v.transpose(0, 1, 3, 4, 2).reshape(n_even, dec_batch, B_WIDTH, B_REACH)
    c_sk = cache_swa_k.reshape(n_odd, dec_batch, C_WINDOW, C_KWIDTH)
    c_sv = cache_swa_v.reshape(n_odd, dec_batch, C_WINDOW, C_KWIDTH)

    c0 = A_Q_LORA + A_KV_LORA
    c1 = c0 + A_ROPE
    xs = {True: x_prompt.reshape(n_p, D_MODEL), False: x_sample.reshape(n_s, D_MODEL)}
    new = {True: [[] for _ in range(6)], False: [[] for _ in range(6)]}
    for l in range(depth):
        g = norm_g[l]
        i = l // 2
        if l % 2 == 0:
            w = even_w_in[i].astype(BF16)
            zc = lambda n: jnp.zeros((D_MODEL, n), BF16)
            win = jnp.concatenate([w[:, :c0], zc(A_NOPE), w[:, c0:c1], zc(LANES - A_NOPE - A_ROPE), w[:, c1:]], axis=1)
            wuq = jnp.pad(mla_w_uq[i].reshape(A_Q_LORA, A_HEADS, A_QK),
                          ((0, 0), (0, 0), (0, LANES - A_QK))).reshape(A_Q_LORA, A_HEADS * LANES).astype(BF16)
            wkv = mla_w_ukv[i].reshape(A_KV_LORA, A_HEADS, A_NOPE + A_V)
            wk = jnp.pad(wkv[:, :, :A_NOPE], ((0, 0), (0, 0), (0, LANES - A_NOPE))).reshape(A_KV_LORA, A_HEADS * LANES)
            wv = wkv[:, :, A_NOPE:].reshape(A_KV_LORA, A_HEADS * A_V)
            wukv = jnp.concatenate([wk, wv], axis=1).astype(BF16)
            bias, bias_t = _band_bias(band_rel_bias[i])
            w_band = B_REACH + BAND_TILE
            bias_pairs = bias_t.reshape(B_HEADS // 2, 2, w_band, BAND_TILE).transpose(0, 2, 1, 3).reshape(
                B_HEADS // 2, w_band, 2 * BAND_TILE)
            wo = even_w_out[i].astype(BF16)
        else:
            win = odd_w_in[i].astype(BF16)
            wo = odd_w_out[i].astype(BF16)
        for prompt in (True, False):
            x = _ffn(xs[prompt], g[0:2], wg, wu, wd, l, 0)
            st = new[prompt]
            if l % 2 == 0:
                qa, ka, va, qb, kb, vb, ckv, kpe, kbf, vbf = _even_proj(
                    x, g[2:3], win, mla_q_norm[i][None], mla_kv_norm[i][None], wuq, wukv, mla_tabs[prompt],
                    prompt, batch, seq)
                if prompt:
                    mixes = [_mla_prompt(qa, ka, va, batch, seq), _band_prompt(qb, kb, vb, bias_pairs, batch, seq)]
                    lead = (batch, seq)
                else:
                    mixes = _even_sample(i, qa, ka, va, qb, kb, vb, cache_mla_ckv, c_kpe_t, c_bk_t, c_bv_t,
                                         wukv, pick, bias, dec_batch, t_new)
                    lead = (dec_batch, t_new)
                st[0].append(ckv.reshape(*lead, A_KV_LORA))
                st[1].append(kpe.reshape(*lead, A_ROPE))
                st[2].append(kbf.reshape(lead[0], -1, B_HEADS, HEAD_DIM))
                st[3].append(vbf.reshape(lead[0], -1, B_HEADS, HEAD_DIM))
            else:
                q, k, v, kf, vf = _odd_proj(x, g[2:3], win, swa_tabs[prompt], prompt, batch, seq)
                if prompt:
                    mixes = [_swa_prompt(swa_sinks, i, q, k, v, batch, seq)]
                    lead = batch
                else:
                    mixes = [_odd_sample(swa_sinks, i, q, k, v, c_sk, c_sv, dec_batch, t_new)]
                    lead = dec_batch
                st[4].append(kf.reshape(lead, -1, C_KV_HEADS, HEAD_DIM))
                st[5].append(vf.reshape(lead, -1, C_KV_HEADS, HEAD_DIM))
            xs[prompt] = _mix_ffn(x, mixes, wo, g[3:4], g[4:6], wg, wu, wd, l, 1)

    y_prompt = xs[True].reshape(batch, seq, D_MODEL)
    y_sample = xs[False].reshape(dec_batch, t_new, D_MODEL)
    st_p = [jnp.stack(s, axis=0) for s in new[True]]
    st_s = [jnp.stack(s, axis=0) for s in new[False]]
    return (y_prompt, y_sample, *st_p, *st_s)
```

```python
import functools
import math

import jax
import jax.numpy as jnp
from jax import lax
from jax.experimental import pallas as pl
from jax.experimental.pallas import tpu as pltpu

F32 = jnp.float32
BF16 = jnp.bfloat16

D_MODEL = 1024
D_FF = 2816
CHUNK = 64
HEAD_DIM = 64
ROPE_THETA = 500000.0
NORM_EPS = 1e-6
NEG_INF = -1e30
A_HEADS = 8
A_Q_LORA = 256
A_KV_LORA = 128
A_NOPE = 64
A_ROPE = 32
A_QK = A_NOPE + A_ROPE
A_V = 64
A_SCALE = A_QK ** -0.5
B_HEADS = 8
B_REACH = 512
B_MAX_REL = 128
C_HEADS = 16
C_KV_HEADS = 2
C_WINDOW = 128
C_ROT = HEAD_DIM // 4
HEAD_SCALE = HEAD_DIM ** -0.5
LOG2E = math.log2(math.e)

LANES = 128
TOKEN_TILE = 512
FFN_TILE = 1024
FFN_PARTS = 2
FFN_ROWS = FFN_TILE // FFN_PARTS
FF_TILE = 256
PROJ_PARTS = 2
PROJ_ROWS = TOKEN_TILE // PROJ_PARTS
ATT_TILE = 128
BAND_TILE = 256
BAND_BLOCKS = 2
MLA_Q_TILE = 512
MLA_K_TILE = 256
BIAS_DIAG_LANES = 1024
SWA_BLOCKS = 4
ONES_ROWS = 16
VMEM_LIMIT = 56 * 1024 * 1024
B_WIDTH = B_HEADS * HEAD_DIM
C_QWIDTH = C_HEADS * HEAD_DIM
C_KWIDTH = C_KV_HEADS * HEAD_DIM
EVEN_CKV = A_Q_LORA
EVEN_KPE = EVEN_CKV + A_KV_LORA
EVEN_QB = EVEN_KPE + LANES
EVEN_KB = EVEN_QB + B_WIDTH
EVEN_VB = EVEN_KB + B_WIDTH


def _params(*sem):
    return pltpu.CompilerParams(dimension_semantics=sem, vmem_limit_bytes=VMEM_LIMIT)


def _const_spec(shape, index=None):
    index = index or (0,) * len(shape)
    return pl.BlockSpec(shape, lambda *_: index, pipeline_mode=pl.Buffered(1))


def _rms(x, g):
    ms = jnp.mean(x * x, axis=-1, keepdims=True)
    return x * lax.rsqrt(ms + NORM_EPS) * g


def _dot(a, b):
    return jnp.dot(a, b, preferred_element_type=F32)


def _dot_nt(a, b):
    return lax.dot_general(a, b, (((1,), (1,)), ((), ())), preferred_element_type=F32)


def _rope_group(x, c, s1, s2, half):
    return x * c + pltpu.roll(x, half, 1) * s1 + pltpu.roll(x, LANES - half, 1) * s2


def _lane_mask(lo):
    lane = lax.broadcasted_iota(jnp.int32, (1, LANES), 1)
    return (lane >= lo) & (lane < lo + HEAD_DIM)


def _swiglu_update(x, g_ref, wg_ref, wu_ref, wd_ref):
    xn = _rms(x, g_ref[0:1, :]).astype(BF16)
    acc = jnp.zeros(x.shape, F32)
    for c in range(D_FF // FF_TILE):
        sl = slice(c * FF_TILE, (c + 1) * FF_TILE)
        a = _dot(xn, wg_ref[:, sl])
        b = _dot(xn, wu_ref[:, sl])
        h = (a * jax.nn.sigmoid(a) * b).astype(BF16)
        acc = acc + _dot(h, wd_ref[sl, :])
    return x + 0.5 * _rms(acc, g_ref[1:2, :])


def _ffn_row_parts():
    return [slice(p * FFN_ROWS, (p + 1) * FFN_ROWS) for p in range(FFN_PARTS)]


def _ffn_kernel(x_ref, g_ref, wg_ref, wu_ref, wd_ref, o_ref):
    for rows in _ffn_row_parts():
        o_ref[rows, :] = _swiglu_update(x_ref[rows, :], g_ref, wg_ref, wu_ref, wd_ref)


def _mix_ffn_kernel(*refs, n_mix):
    x_ref = refs[0]
    mix_refs = refs[1:1 + n_mix]
    wo_ref, gm_ref, g_ref, wg_ref, wu_ref, wd_ref, o_ref = refs[1 + n_mix:]
    ys = []
    for rows in _ffn_row_parts():
        y = None
        row = 0
        for m_ref in mix_refs:
            w = m_ref.shape[1]
            part = _dot(m_ref[rows, :], wo_ref[row:row + w, :])
            y = part if y is None else y + part
            row += w
        ys.append(y)
    for rows, y in zip(_ffn_row_parts(), ys):
        x = x_ref[rows, :] + _rms(y, gm_ref[...])
        o_ref[rows, :] = _swiglu_update(x, g_ref, wg_ref, wu_ref, wd_ref)


def _ffn_weight_specs(layer, which):
    idx = (layer, which, 0, 0)
    return [_const_spec((None, None, D_MODEL, D_FF), idx), _const_spec((None, None, D_MODEL, D_FF), idx),
            _const_spec((None, None, D_FF, D_MODEL), idx)]


def _ffn(x, g, wg, wu, wd, layer, which):
    t = x.shape[0]
    tile = pl.BlockSpec((FFN_TILE, D_MODEL), lambda i: (i, 0))
    return pl.pallas_call(
        _ffn_kernel,
        out_shape=jax.ShapeDtypeStruct((t, D_MODEL), F32),
        grid=(t // FFN_TILE,),
        in_specs=[tile, _const_spec((2, D_MODEL))] + _ffn_weight_specs(layer, which),
        out_specs=tile,
        compiler_params=_params("parallel"),
        name="ffn",
    )(x, g, wg, wu, wd)


def _mix_ffn(x, mixes, wo, gm, g, wg, wu, wd, layer, which):
    t = x.shape[0]
    tile = pl.BlockSpec((FFN_TILE, D_MODEL), lambda i: (i, 0))
    mix_specs = [pl.BlockSpec((FFN_TILE, m.shape[1]), lambda i: (i, 0)) for m in mixes]
    return pl.pallas_call(
        functools.partial(_mix_ffn_kernel, n_mix=len(mixes)),
        out_shape=jax.ShapeDtypeStruct((t, D_MODEL), F32),
        grid=(t // FFN_TILE,),
        in_specs=[tile] + mix_specs + [_const_spec(wo.shape), _const_spec((1, D_MODEL)), _const_spec((2, D_MODEL))]
                 + _ffn_weight_specs(layer, which),
        out_specs=tile,
        compiler_params=_params("parallel"),
        name="mix_ffn",
    )(x, *mixes, wo, gm, g, wg, wu, wd)


def _even_proj_kernel(x_ref, g_ref, win_ref, qn_ref, kvn_ref, wuq_ref, wukv_ref,
                      c_ref, s1_ref, s2_ref,
                      qa_ref, ka_ref, va_ref, qb_ref, kb_ref, vb_ref,
                      ckv_ref, kpe_ref, kbf_ref, vbf_ref, *scratch, prompt, tiles_per_seq):
    half = A_ROPE // 2
    for part in range(PROJ_PARTS):
        rows = slice(part * PROJ_ROWS, (part + 1) * PROJ_ROWS)
        h = _rms(x_ref[rows, :], g_ref[...]).astype(BF16)
        proj = _dot(h, win_ref[...])
        cq = _rms(proj[:, 0:EVEN_CKV], qn_ref[...]).astype(BF16)
        ckv = _rms(proj[:, EVEN_CKV:EVEN_KPE], kvn_ref[...])
        c, s1, s2 = c_ref[rows, :], s1_ref[rows, :], s2_ref[rows, :]
        kpe = _rope_group(proj[:, EVEN_KPE:EVEN_QB], c, s1, s2, half)
        ckv_ref[rows, :] = ckv
        kpe_ref[rows, :] = pltpu.roll(kpe, LANES - A_NOPE, 1)[:, 0:A_ROPE]
        qa = _dot(cq, wuq_ref[...])
        kv = _dot(ckv.astype(BF16), wukv_ref[...])
        for hd in range(A_HEADS):
            sl = slice(hd * LANES, (hd + 1) * LANES)
            qa_ref[rows, sl] = (_rope_group(qa[:, sl], c, s1, s2, half) * (A_SCALE * LOG2E)).astype(BF16)
            ka_ref[rows, sl] = (kv[:, sl] + kpe).astype(BF16)
        for pair in range(B_HEADS // 2):
            grp = proj[:, EVEN_QB + pair * LANES:EVEN_QB + (pair + 1) * LANES] * (HEAD_SCALE * LOG2E)
            for sub in range(2):
                qb_ref[2 * pair + sub, rows, :] = jnp.where(_lane_mask(sub * HEAD_DIM), grp, 0.0).astype(BF16)
        va = kv[:, A_HEADS * LANES:]
        kb = proj[:, EVEN_KB:EVEN_VB]
        vb = proj[:, EVEN_VB:EVEN_VB + B_WIDTH]
        kb_ref[rows, :] = kb.astype(BF16)
        if prompt:
            va_ref[:, rows] = va.T.astype(BF16)
            vb_ref[:, rows] = vb.T.astype(BF16)
            scratch[0][rows, :] = kb
            scratch[1][rows, :] = vb
        else:
            va_ref[rows, :] = va.astype(BF16)
            vb_ref[rows, :] = vb.astype(BF16)
            kbf_ref[rows, :] = kb
            vbf_ref[rows, :] = vb
    if prompt:
        @pl.when(pl.program_id(0) % tiles_per_seq == tiles_per_seq - 1)
        def _():
            kbf_ref[...] = scratch[0][...]
            vbf_ref[...] = scratch[1][...]


def _last_tile_index(tiles_per_seq):
    return lambda i: (jnp.maximum((i + 1) // tiles_per_seq - 1, 0), 0)


def _even_proj(x, g, win, qn, kvn, wuq, wukv, tabs, prompt, batch, seq):
    t = x.shape[0]
    tiles_per_seq = seq // TOKEN_TILE if prompt else 1

    def tile(w):
        return pl.BlockSpec((TOKEN_TILE, w), lambda i: (i, 0))

    def tile_t(w):
        return pl.BlockSpec((w, TOKEN_TILE), lambda i: (0, i))

    def sds(shape, dtype):
        return jax.ShapeDtypeStruct(shape, dtype)

    wa = A_HEADS * LANES
    if prompt:
        v_shapes = [sds((A_HEADS * A_V, t), BF16), sds((B_WIDTH, t), BF16)]
        v_specs = [tile_t(A_HEADS * A_V), tile_t(B_WIDTH)]
        keep_shape = sds((batch * TOKEN_TILE, B_WIDTH), F32)
        keep_spec = pl.BlockSpec((TOKEN_TILE, B_WIDTH), _last_tile_index(tiles_per_seq))
        tab_idx = lambda i: (i % tiles_per_seq, 0)
    else:
        v_shapes = [sds((t, A_HEADS * A_V), BF16), sds((t, B_WIDTH), BF16)]
        v_specs = [tile(A_HEADS * A_V), tile(B_WIDTH)]
        keep_shape = sds((t, B_WIDTH), F32)
        keep_spec = tile(B_WIDTH)
        tab_idx = lambda i: (0, 0)
    qb_spec = pl.BlockSpec((B_HEADS, TOKEN_TILE, LANES), lambda i: (0, i, 0))
    out_shape = [sds((t, wa), BF16), sds((t, wa), BF16), v_shapes[0], sds((B_HEADS, t, LANES), BF16),
                 sds((t, B_WIDTH), BF16), v_shapes[1], sds((t, A_KV_LORA), F32), sds((t, A_ROPE), F32),
                 keep_shape, keep_shape]
    out_specs = [tile(wa), tile(wa), v_specs[0], qb_spec, tile(B_WIDTH), v_specs[1],
                 tile(A_KV_LORA), tile(A_ROPE), keep_spec, keep_spec]
    return pl.pallas_call(
        functools.partial(_even_proj_kernel, prompt=prompt, tiles_per_seq=tiles_per_seq),
        out_shape=out_shape,
        grid=(t // TOKEN_TILE,),
        in_specs=[tile(D_MODEL), _const_spec((1, D_MODEL)), _const_spec(win.shape),
                  _const_spec((1, A_Q_LORA)), _const_spec((1, A_KV_LORA)),
                  _const_spec(wuq.shape), _const_spec(wukv.shape)]
                 + [pl.BlockSpec((TOKEN_TILE, LANES), tab_idx)] * 3,
        out_specs=out_specs,
        scratch_shapes=[pltpu.VMEM((TOKEN_TILE, B_WIDTH), F32)] * 2 if prompt else [],
        compiler_params=_params("arbitrary"),
        name="even_proj",
    )(x, g, win, qn, kvn, wuq, wukv, *tabs)


def _odd_proj_kernel(x_ref, g_ref, win_ref, c_ref, s1_ref, s2_ref,
                     q_ref, k_ref, v_ref, kf_ref, vf_ref, *scratch, prompt, tiles_per_seq):
    half = C_ROT // 2
    group = C_HEADS // C_KV_HEADS
    for part in range(PROJ_PARTS):
        rows = slice(part * PROJ_ROWS, (part + 1) * PROJ_ROWS)
        h = _rms(x_ref[rows, :], g_ref[...]).astype(BF16)
        proj = _dot(h, win_ref[...])
        c, s1, s2 = c_ref[rows, :], s1_ref[rows, :], s2_ref[rows, :]
        k = _rope_group(proj[:, C_QWIDTH:C_QWIDTH + LANES], c, s1, s2, half)
        v = proj[:, C_QWIDTH + LANES:C_QWIDTH + 2 * LANES]
        k_ref[rows, :] = k.astype(BF16)
        if prompt:
            for pair in range(C_HEADS // 2):
                grp = _rope_group(proj[:, pair * LANES:(pair + 1) * LANES], c, s1, s2, half) * (HEAD_SCALE * LOG2E)
                swapped = pltpu.roll(grp, HEAD_DIM, 1)
                for sub in range(2):
                    hd = 2 * pair + sub
                    kv = hd // group
                    src = grp if sub == kv else swapped
                    q_ref[hd, rows, :] = jnp.where(_lane_mask(kv * HEAD_DIM), src, 0.0).astype(BF16)
            v_ref[:, rows] = v.T.astype(BF16)
            if part == PROJ_PARTS - 1:
                scratch[0][...] = k[PROJ_ROWS - C_WINDOW:, :]
                scratch[1][...] = v[PROJ_ROWS - C_WINDOW:, :]
        else:
            for grp in range(C_QWIDTH // LANES):
                sl = slice(grp * LANES, (grp + 1) * LANES)
                q_ref[rows, sl] = (_rope_group(proj[:, sl], c, s1, s2, half) * (HEAD_SCALE * LOG2E)).astype(BF16)
            v_ref[rows, :] = v.astype(BF16)
            kf_ref[rows, :] = k
            vf_ref[rows, :] = v
    if prompt:
        @pl.when(pl.program_id(0) % tiles_per_seq == tiles_per_seq - 1)
        def _():
            kf_ref[...] = scratch[0][...]
            vf_ref[...] = scratch[1][...]


def _odd_proj(x, g, win, tabs, prompt, batch, seq):
    t = x.shape[0]
    tiles_per_seq = seq // TOKEN_TILE if prompt else 1

    def tile(w):
        return pl.BlockSpec((TOKEN_TILE, w), lambda i: (i, 0))

    def sds(shape, dtype):
        return jax.ShapeDtypeStruct(shape, dtype)

    if prompt:
        q_shape = sds((C_HEADS, t, LANES), BF16)
        q_spec = pl.BlockSpec((C_HEADS, TOKEN_TILE, LANES), lambda i: (0, i, 0))
        v_shape, v_spec = sds((LANES, t), BF16), pl.BlockSpec((LANES, TOKEN_TILE), lambda i: (0, i))
        keep_shape = sds((batch * C_WINDOW, LANES), F32)
        keep_spec = pl.BlockSpec((C_WINDOW, LANES), _last_tile_index(tiles_per_seq))
        tab_idx = lambda i: (i % tiles_per_seq, 0)
    else:
        q_shape, q_spec = sds((t, C_QWIDTH), BF16), tile(C_QWIDTH)
        v_shape, v_spec = sds((t, LANES), BF16), tile(LANES)
        keep_shape, keep_spec = sds((t, LANES), F32), tile(LANES)
        tab_idx = lambda i: (0, 0)
    return pl.pallas_call(
        functools.partial(_odd_proj_kernel, prompt=prompt, tiles_per_seq=tiles_per_seq),
        out_shape=[q_shape, sds((t, LANES), BF16), v_shape, keep_shape, keep_shape],
        grid=(t // TOKEN_TILE,),
        in_specs=[tile(D_MODEL), _const_spec((1, D_MODEL)), _const_spec(win.shape)]
                 + [pl.BlockSpec((TOKEN_TILE, LANES), tab_idx)] * 3,
        out_specs=[q_spec, tile(LANES), v_spec, keep_spec, keep_spec],
        scratch_shapes=[pltpu.VMEM((C_WINDOW, LANES), F32)] * 2 if prompt else [],
        compiler_params=_params("arbitrary"),
        name="odd_proj",
    )(x, g, win, *tabs)


def _band_bias_kernel(tab_ref, o_ref, ot_ref):
    hd = pl.program_id(0)
    w = B_REACH + BAND_TILE
    row = lax.broadcasted_iota(jnp.int32, (BAND_TILE, w), 0)
    col = lax.broadcasted_iota(jnp.int32, (BAND_TILE, w), 1)
    lo = (row // CHUNK) * CHUNK
    visible = (col >= lo) & (col < lo + B_REACH + CHUNK)
    n = BIAS_DIAG_LANES
    e = lax.broadcasted_iota(jnp.int32, (1, n), 1)
    rel = jnp.clip(B_REACH + BAND_TILE - 1 - e, -B_MAX_REL, B_MAX_REL) + B_MAX_REL

    def body(r, acc):
        return jnp.where(rel == r, tab_ref[hd, r], acc)

    diag = lax.fori_loop(0, 2 * B_MAX_REL + 1, body, jnp.zeros((1, n), F32))
    shifted = pltpu.roll(jnp.broadcast_to(diag, (BAND_TILE, n)), n - (BAND_TILE - 1), 1, stride=1, stride_axis=0)
    bias = jnp.where(visible, shifted[:, 0:w] * LOG2E, NEG_INF)
    o_ref[...] = bias
    ot_ref[...] = bias.T


def _band_bias(table):
    w = B_REACH + BAND_TILE
    return pl.pallas_call(
        _band_bias_kernel,
        out_shape=[jax.ShapeDtypeStruct((B_HEADS, BAND_TILE, w), F32),
                   jax.ShapeDtypeStruct((B_HEADS, w, BAND_TILE), F32)],
        grid=(B_HEADS,),
        in_specs=[pl.BlockSpec(memory_space=pltpu.SMEM)],
        out_specs=[pl.BlockSpec((None, BAND_TILE, w), lambda h: (h, 0, 0)),
                   pl.BlockSpec((None, w, BAND_TILE), lambda h: (h, 0, 0))],
        compiler_params=_params("arbitrary"),
        name="band_bias",
    )(table)


def _ones_rows(n):
    return jnp.ones((ONES_ROWS, n), BF16)


def _mla_prompt_kernel(q_ref, k_ref, vt_ref, o_ref, m_ref, l_ref, acc_ref, s0_ref, s1_ref, smax0_ref, smax1_ref):
    i = pl.program_id(1)
    tq, tk = MLA_Q_TILE, MLA_K_TILE
    m_ref[...] = jnp.full(m_ref.shape, NEG_INF, F32)
    l_ref[...] = jnp.zeros(l_ref.shape, F32)
    acc_ref[...] = jnp.zeros(acc_ref.shape, F32)
    per_q = tq // tk
    assert per_q == 2
    s_refs = (s0_ref, s1_ref)
    smax_refs = (smax0_ref, smax1_ref)

    def live_queries(diag):
        return slice(0 if diag is None else diag * tk, tq)

    def score(j, slot, hd, diag=None):
        ks = pl.multiple_of(j * tk, tk)
        sl = slice(hd * LANES, (hd + 1) * LANES)
        cols = live_queries(diag)
        s = _dot_nt(k_ref[pl.ds(ks, tk), sl], q_ref[cols, sl])
        s_refs[slot][hd, :, cols] = s
        if diag is None:
            smax_refs[slot][hd] = jnp.max(s, axis=0, keepdims=True)

    def block(j, slot, diag, next_diag, stage_next=True):
        ks = pl.multiple_of(j * tk, tk)
        cols = live_queries(diag)
        m_all = [m_ref[hd, :, cols] for hd in range(A_HEADS)]
        l_all = [l_ref[hd, :, cols] for hd in range(A_HEADS)]
        acc_all = [acc_ref[hd, :, cols] for hd in range(A_HEADS)]
        ones = _ones_rows(tk)
        if diag is not None:
            shape = (tk, tq - cols.start)
            key_chunk = (diag * tk + lax.broadcasted_iota(jnp.int32, shape, 0)) // CHUNK
            qry_chunk = (cols.start + lax.broadcasted_iota(jnp.int32, shape, 1)) // CHUNK
            visible = key_chunk <= qry_chunk
        new = []
        for hd in range(A_HEADS):
            if stage_next:
                score(j + 1, 1 - slot, hd, next_diag)
            s = s_refs[slot][hd, :, cols]
            if diag is None:
                s_max = smax_refs[slot][hd]
            else:
                s = jnp.where(visible, s, NEG_INF)
                s_max = jnp.max(s, axis=0, keepdims=True)
            m_new = jnp.maximum(m_all[hd], s_max)
            alpha = jnp.exp2(m_all[hd] - m_new)
            p = jnp.exp2(s - m_new).astype(BF16)
            vt = jnp.concatenate([vt_ref[hd * A_V:(hd + 1) * A_V, pl.ds(ks, tk)], ones], axis=0)
            pv = _dot(vt, p)
            new.append((m_new, alpha * l_all[hd] + pv[A_V:A_V + 1], alpha * acc_all[hd] + pv[0:A_V]))
        for hd, (m_new, l_new, acc_new) in enumerate(new):
            m_ref[hd, :, cols] = m_new
            l_ref[hd, :, cols] = l_new
            acc_ref[hd, :, cols] = acc_new

    for hd in range(A_HEADS):
        score(0, 0, hd)

    def body(pair, carry):
        for d in range(per_q):
            block(pair * per_q + d, d, None, None)
        return carry

    lax.fori_loop(0, i, body, 0)
    for d in range(per_q):
        block(i * per_q + d, d, d, d + 1, stage_next=d + 1 < per_q)
    outs = [acc_ref[hd] / l_ref[hd] for hd in range(A_HEADS)]
    o_ref[...] = jnp.concatenate(outs, axis=0).T.astype(BF16)


def _mla_prompt(qa, ka, vat, batch, seq):
    nq = seq // MLA_Q_TILE
    wa = A_HEADS * LANES
    wv = A_HEADS * A_V
    return pl.pallas_call(
        _mla_prompt_kernel,
        out_shape=jax.ShapeDtypeStruct((batch * seq, wv), BF16),
        grid=(batch, nq),
        in_specs=[pl.BlockSpec((MLA_Q_TILE, wa), lambda b, i: (b * nq + i, 0)),
                  pl.BlockSpec((seq, wa), lambda b, i: (b, 0)),
                  pl.BlockSpec((wv, seq), lambda b, i: (0, b))],
        out_specs=pl.BlockSpec((MLA_Q_TILE, wv), lambda b, i: (b * nq + i, 0)),
        scratch_shapes=[pltpu.VMEM((A_HEADS, 1, MLA_Q_TILE), F32), pltpu.VMEM((A_HEADS, 1, MLA_Q_TILE), F32),
                        pltpu.VMEM((A_HEADS, A_V, MLA_Q_TILE), F32),
                        pltpu.VMEM((A_HEADS, MLA_K_TILE, MLA_Q_TILE), F32),
                        pltpu.VMEM((A_HEADS, MLA_K_TILE, MLA_Q_TILE), F32),
                        pltpu.VMEM((A_HEADS, 1, MLA_Q_TILE), F32), pltpu.VMEM((A_HEADS, 1, MLA_Q_TILE), F32)],
        compiler_params=_params("parallel", "arbitrary"),
        name="mla_prompt",
    )(qa, ka, vat)


def _fill_padded(kpad, vtpad, k_ref, vt_ref, reach):
    kpad[0:reach, :] = jnp.zeros((reach, kpad.shape[1]), kpad.dtype)
    kpad[reach:, :] = k_ref[...]
    vtpad[:, 0:reach] = jnp.zeros((vtpad.shape[0], reach), vtpad.dtype)
    vtpad[:, reach:] = vt_ref[...]


def _window_attend(s, vt, sink=None):
    m = jnp.max(s, axis=0, keepdims=True)
    if sink is not None:
        m = jnp.maximum(m, sink)
    p = jnp.exp2(s - m).astype(BF16)
    d = vt.shape[0]
    pv = _dot(jnp.concatenate([vt, _ones_rows(vt.shape[1])], axis=0), p)
    l = pv[d:d + 1]
    if sink is not None:
        l = l + jnp.exp2(sink - m)
    return pv[0:d] * (1.0 / l)


def _band_prompt_kernel(q_ref, k_ref, vt_ref, bias_ref, o_ref, kpad, vtpad, s0_ref, s1_ref):
    i = pl.program_id(1)
    tq = BAND_TILE
    s_refs = (s0_ref, s1_ref)

    @pl.when(i == 0)
    def _():
        _fill_padded(kpad, vtpad, k_ref, vt_ref, B_REACH)

    w = B_REACH + tq
    n_pairs = B_HEADS // 2
    step = BAND_BLOCKS * tq

    def attend(mask_start):
        key = lax.broadcasted_iota(jnp.int32, (w, 2 * tq), 0)

        def first_row(blk):
            return pl.multiple_of((i * BAND_BLOCKS + blk) * tq, tq)

        def score(unit):
            blk, pair = divmod(unit, n_pairs)
            q2 = q_ref[2 * pair:2 * pair + 2, blk * tq:(blk + 1) * tq, :].reshape(2 * tq, LANES)
            s_refs[unit % 2][...] = _dot_nt(kpad[pl.ds(first_row(blk), w), pair * LANES:(pair + 1) * LANES], q2)

        n_units = BAND_BLOCKS * n_pairs
        score(0)
        for unit in range(n_units):
            if unit + 1 < n_units:
                score(unit + 1)
            blk, pair = divmod(unit, n_pairs)
            if pair == 0:
                outs = []
            s = s_refs[unit % 2][...] + bias_ref[pair]
            if mask_start:
                s = jnp.where(key >= B_REACH - first_row(blk), s, NEG_INF)
            o2 = _window_attend(s, vtpad[pair * LANES:(pair + 1) * LANES, pl.ds(first_row(blk), w)])
            outs.append(o2[0:HEAD_DIM, 0:tq])
            outs.append(o2[HEAD_DIM:, tq:])
            if pair == n_pairs - 1:
                o_ref[blk * tq:(blk + 1) * tq, :] = jnp.concatenate(outs, axis=0).T.astype(BF16)

    pl.when(i * step < B_REACH)(functools.partial(attend, True))
    pl.when(i * step >= B_REACH)(functools.partial(attend, False))


def _band_prompt(qb, kb, vbt, bias_pairs, batch, seq):
    step = BAND_BLOCKS * BAND_TILE
    nq = seq // step
    return pl.pallas_call(
        _band_prompt_kernel,
        out_shape=jax.ShapeDtypeStruct((batch * seq, B_WIDTH), BF16),
        grid=(batch, nq),
        in_specs=[pl.BlockSpec((B_HEADS, step, LANES), lambda b, i: (0, b * nq + i, 0)),
                  pl.BlockSpec((seq, B_WIDTH), lambda b, i: (b, 0)),
                  pl.BlockSpec((B_WIDTH, seq), lambda b, i: (0, b)),
                  _const_spec(bias_pairs.shape)],
        out_specs=pl.BlockSpec((step, B_WIDTH), lambda b, i: (b * nq + i, 0)),
        scratch_shapes=[pltpu.VMEM((seq + B_REACH, B_WIDTH), BF16), pltpu.VMEM((B_WIDTH, seq + B_REACH), BF16)]
                       + [pltpu.VMEM((B_REACH + BAND_TILE, 2 * BAND_TILE), F32)] * 2,
        compiler_params=_params("parallel", "arbitrary"),
        name="band_prompt",
    )(qb, kb, vbt, bias_pairs)


def _swa_prompt_kernel(sink_ref, q_ref, k_ref, vt_ref, o_ref, kpad, vtpad, s0_ref, s1_ref, *, layer):
    i = pl.program_id(1)
    s_refs = (s0_ref, s1_ref)

    @pl.when(i == 0)
    def _():
        _fill_padded(kpad, vtpad, k_ref, vt_ref, C_WINDOW)

    w = C_WINDOW + ATT_TILE
    group = C_HEADS // C_KV_HEADS
    key = lax.broadcasted_iota(jnp.int32, (w, ATT_TILE), 0)
    qry = lax.broadcasted_iota(jnp.int32, (w, ATT_TILE), 1)
    lo = (qry // CHUNK) * CHUNK
    in_window = (key >= lo) & (key < lo + C_WINDOW + CHUNK)

    def score(unit):
        blk, kv = divmod(unit, C_KV_HEADS)
        start = pl.multiple_of((i * SWA_BLOCKS + blk) * ATT_TILE, ATT_TILE)
        q = q_ref[kv * group:(kv + 1) * group, blk * ATT_TILE:(blk + 1) * ATT_TILE, :]
        s_refs[unit % 2][...] = _dot_nt(kpad[pl.ds(start, w), :], q.reshape(group * ATT_TILE, LANES))

    n_units = SWA_BLOCKS * C_KV_HEADS
    score(0)
    for unit in range(n_units):
        if unit + 1 < n_units:
            score(unit + 1)
        blk, kv = divmod(unit, C_KV_HEADS)
        if kv == 0:
            first = (i * SWA_BLOCKS + blk) * ATT_TILE
            start = pl.multiple_of(first, ATT_TILE)
            visible = in_window & (key >= C_WINDOW - first)
            mask = jnp.concatenate([jnp.where(visible, 0.0, NEG_INF)] * group, axis=1)
            outs = []
        sink = jnp.concatenate([jnp.full((1, ATT_TILE), sink_ref[layer, kv * group + g] * LOG2E, F32)
                                for g in range(group)], axis=1)
        og = _window_attend(s_refs[unit % 2][...] + mask, vtpad[kv * HEAD_DIM:(kv + 1) * HEAD_DIM, pl.ds(start, w)],
                            sink)
        outs.extend(og[:, g * ATT_TILE:(g + 1) * ATT_TILE] for g in range(group))
        if kv == C_KV_HEADS - 1:
            o_ref[blk * ATT_TILE:(blk + 1) * ATT_TILE, :] = jnp.concatenate(outs, axis=0).T.astype(BF16)


def _swa_prompt(sinks, layer, q, k, vt, batch, seq):
    step = SWA_BLOCKS * ATT_TILE
    nq = seq // step
    return pl.pallas_call(
        functools.partial(_swa_prompt_kernel, layer=layer),
        out_shape=jax.ShapeDtypeStruct((batch * seq, C_QWIDTH), BF16),
        grid=(batch, nq),
        in_specs=[pl.BlockSpec(memory_space=pltpu.SMEM),
                  pl.BlockSpec((C_HEADS, step, LANES), lambda b, i: (0, b * nq + i, 0)),
                  pl.BlockSpec((seq, C_KWIDTH), lambda b, i: (b, 0)),
                  pl.BlockSpec((C_KWIDTH, seq), lambda b, i: (0, b))],
        out_specs=pl.BlockSpec((step, C_QWIDTH), lambda b, i: (b * nq + i, 0)),
        scratch_shapes=[pltpu.VMEM((seq + C_WINDOW, C_KWIDTH), BF16), pltpu.VMEM((C_KWIDTH, seq + C_WINDOW), BF16)]
                       + [pltpu.VMEM((C_WINDOW + ATT_TILE, C_HEADS // C_KV_HEADS * ATT_TILE), F32)] * 2,
        compiler_params=_params("parallel", "arbitrary"),
        name="swa_prompt",
    )(sinks, q, k, vt)


def _softmax_pv2(s_c, s_n, v_c, v_n, sink=None, v_c_transposed=False):
    m = jnp.maximum(jnp.max(s_c, axis=-1, keepdims=True), jnp.max(s_n, axis=-1, keepdims=True))
    if sink is not None:
        m = jnp.maximum(m, sink)
    p_c = jnp.exp2(s_c - m)
    p_n = jnp.exp2(s_n - m)
    l = jnp.sum(p_c, axis=-1, keepdims=True) + jnp.sum(p_n, axis=-1, keepdims=True)
    if sink is not None:
        l = l + jnp.exp2(sink - m)
    pv_c = _dot_nt(p_c.astype(BF16), v_c) if v_c_transposed else _dot(p_c.astype(BF16), v_c)
    return (pv_c + _dot(p_n.astype(BF16), v_n)) / l


def _even_sample_kernel(qa_ref, ka_ref, va_ref, qb_ref, kb_ref, vb_ref,
                        cckv_ref, ckpe_ref, cbk_ref, cbv_ref, wukv_ref, pick_ref, bias_ref,
                        oa_ref, ob_ref):
    t = qa_ref.shape[0]
    nb = cbk_ref.shape[1]
    kvc = _dot(cckv_ref[...].astype(BF16), wukv_ref[...])
    kpe_t = ckpe_ref[...].astype(BF16)
    scores_a, scores_b = [], []
    for hd in range(A_HEADS):
        sl = slice(hd * LANES, (hd + 1) * LANES)
        q = qa_ref[:, sl]
        q_rope = _dot(q, pick_ref[...]).astype(BF16)
        s_c = _dot_nt(q, kvc[:, sl].astype(BF16)) + _dot(q_rope, kpe_t)
        scores_a.append((s_c, _dot_nt(q, ka_ref[:, sl])))
    for hd in range(B_HEADS):
        pair = slice((hd // 2) * LANES, (hd // 2 + 1) * LANES)
        q = qb_ref[hd]
        scores_b.append((_dot(q, cbk_ref[pair, :].astype(BF16)) + bias_ref[hd, 0:t, 0:nb],
                         _dot_nt(q, kb_ref[:, pair]) + bias_ref[hd, 0:t, nb:nb + t]))
    outs_a, outs_b = [], []
    for hd in range(A_HEADS):
        vsl = slice(hd * A_V, (hd + 1) * A_V)
        v_c = kvc[:, A_HEADS * LANES + hd * A_V:A_HEADS * LANES + (hd + 1) * A_V].astype(BF16)
        outs_a.append(_softmax_pv2(*scores_a[hd], v_c, va_ref[:, vsl]))
    for hd in range(B_HEADS):
        sl = slice(hd * HEAD_DIM, (hd + 1) * HEAD_DIM)
        outs_b.append(_softmax_pv2(*scores_b[hd], cbv_ref[sl, :].astype(BF16), vb_ref[:, sl], v_c_transposed=True))
    oa_ref[...] = jnp.concatenate(outs_a, axis=-1).astype(BF16)
    ob_ref[...] = jnp.concatenate(outs_b, axis=-1).astype(BF16)


def _even_sample(layer, qa, ka, va, qb, kb, vb, c_ckv, c_kpe_t, c_bk_t, c_bv_t, wukv, pick, bias, dec_batch, t):
    past = c_ckv.shape[2]
    nb = c_bk_t.shape[3]

    def new(w):
        return pl.BlockSpec((t, w), lambda b: (b, 0))

    def cache(n, w):
        return pl.BlockSpec((None, None, n, w), lambda b: (layer, b, 0, 0))

    return pl.pallas_call(
        _even_sample_kernel,
        out_shape=[jax.ShapeDtypeStruct((dec_batch * t, B_WIDTH), BF16)] * 2,
        grid=(dec_batch,),
        in_specs=[new(A_HEADS * LANES), new(A_HEADS * LANES), new(A_HEADS * A_V),
                  pl.BlockSpec((B_HEADS, t, LANES), lambda b: (0, b, 0)), new(B_WIDTH), new(B_WIDTH),
                  cache(past, A_KV_LORA), cache(A_ROPE, past), cache(B_WIDTH, nb), cache(B_WIDTH, nb),
                  _const_spec(wukv.shape), _const_spec(pick.shape), _const_spec(bias.shape)],
        out_specs=[new(B_WIDTH), new(B_WIDTH)],
        compiler_params=_params("parallel"),
        name="even_sample",
    )(qa, ka, va, qb, kb, vb, c_ckv, c_kpe_t, c_bk_t, c_bv_t, wukv, pick, bias)


def _odd_sample_kernel(sink_ref, q_ref, k_ref, v_ref, ck_ref, cv_ref, o_ref, *, layer):
    group = C_HEADS // C_KV_HEADS
    scores = []
    for hd in range(C_HEADS):
        ksl = slice((hd // group) * HEAD_DIM, (hd // group + 1) * HEAD_DIM)
        q = q_ref[:, hd * HEAD_DIM:(hd + 1) * HEAD_DIM]
        scores.append((_dot_nt(q, ck_ref[:, ksl].astype(BF16)), _dot_nt(q, k_ref[:, ksl])))
    outs = []
    for hd in range(C_HEADS):
        ksl = slice((hd // group) * HEAD_DIM, (hd // group + 1) * HEAD_DIM)
        outs.append(_softmax_pv2(*scores[hd], cv_ref[:, ksl].astype(BF16), v_ref[:, ksl],
                                 sink_ref[layer, hd] * LOG2E))
    o_ref[...] = jnp.concatenate(outs, axis=-1).astype(BF16)


def _odd_sample(sinks, layer, q, k, v, c_k, c_v, dec_batch, t):
    nc = c_k.shape[2]

    def new(w):
        return pl.BlockSpec((t, w), lambda b: (b, 0))

    def cache():
        return pl.BlockSpec((None, None, nc, C_KWIDTH), lambda b: (layer, b, 0, 0))

    return pl.pallas_call(
        functools.partial(_odd_sample_kernel, layer=layer),
        out_shape=jax.ShapeDtypeStruct((dec_batch * t, C_QWIDTH), BF16),
        grid=(dec_batch,),
        in_specs=[pl.BlockSpec(memory_space=pltpu.SMEM), new(C_QWIDTH), new(C_KWIDTH), new(C_KWIDTH),
                  cache(), cache()],
        out_specs=new(C_QWIDTH),
        compiler_params=_params("parallel"),
        name="odd_sample",
    )(sinks, q, k, v, c_k, c_v)


def _rope_parts(pos, n_rot):
    half = n_rot // 2
    inv = ROPE_THETA ** (-jnp.arange(half, dtype=F32) / half)
    ang = pos.astype(F32)[:, None] * inv[None, :]
    return jnp.cos(ang), jnp.sin(ang)


def _rope_tables(pos, n_rot, pre, width):
    cos, sin = _rope_parts(pos, n_rot)
    n = pos.shape[0]
    post = width - pre - n_rot

    def head(first, second, fill):
        return jnp.concatenate([jnp.full((n, pre), fill, F32), first, second, jnp.full((n, post), fill, F32)], axis=1)

    zero = jnp.zeros_like(sin)
    tabs = (head(cos, cos, 1.0), head(zero, sin, 0.0), head(-sin, zero, 0.0))
    return tuple(jnp.tile(t, (1, LANES // width)) for t in tabs)


def kernel(x_prompt, x_sample, cache_mla_ckv, cache_mla_kpe, cache_band_k, cache_band_v,
           cache_swa_k, cache_swa_v, norm_g, ffn_w_gate, ffn_w_up, ffn_w_down, even_w_in,
           mla_q_norm, mla_w_uq, mla_kv_norm, mla_w_ukv, band_rel_bias, even_w_out,
           odd_w_in, swa_sinks, odd_w_out):
    batch, seq, _ = x_prompt.shape
    dec_batch, t_new, _ = x_sample.shape
    depth = norm_g.shape[0]
    n_even = even_w_in.shape[0]
    n_odd = odd_w_in.shape[0]
    past = cache_mla_ckv.shape[2]
    n_p = batch * seq
    n_s = dec_batch * t_new
    assert seq % TOKEN_TILE == 0 and n_s % TOKEN_TILE == 0 and TOKEN_TILE % t_new == 0
    assert n_p % FFN_TILE == 0 and n_s % FFN_TILE == 0
    assert seq % MLA_Q_TILE == 0 and seq % (BAND_BLOCKS * BAND_TILE) == 0 and seq % (SWA_BLOCKS * ATT_TILE) == 0 and min(B_REACH, seq) == TOKEN_TILE and t_new <= CHUNK
    assert cache_band_k.shape[2] == B_REACH and cache_swa_k.shape[2] == C_WINDOW and past >= B_REACH

    pos_p = jnp.arange(seq, dtype=jnp.int32)
    pos_s = past + (jnp.arange(TOKEN_TILE, dtype=jnp.int32) % t_new)
    mla_tabs = {True: _rope_tables(pos_p, A_ROPE, A_NOPE, LANES), False: _rope_tables(pos_s, A_ROPE, A_NOPE, LANES)}
    swa_tabs = {True: _rope_tables(pos_p, C_ROT, 0, HEAD_DIM), False: _rope_tables(pos_s, C_ROT, 0, HEAD_DIM)}

    wg = ffn_w_gate.astype(BF16)
    wu = ffn_w_up.astype(BF16)
    wd = ffn_w_down.astype(BF16)

    pick = jnp.zeros((LANES, A_ROPE), F32).at[A_NOPE + jnp.arange(A_ROPE), jnp.arange(A_ROPE)].set(1.0).astype(BF16)
    c_kpe_t = cache_mla_kpe.transpose(0, 1, 3, 2)
    c_bk_t = cache_band_k.transpose(0, 1, 3, 4, 2).reshape(n_even, dec_batch, B_WIDTH, B_REACH)
    c_bv_t = cache_band_v.transpose(0, 1, 3, 4, 2).reshape(n_even, dec_batch, B_WIDTH, B_REACH)
    c_sk = cache_swa_k.reshape(n_odd, dec_batch, C_WINDOW, C_KWIDTH)
    c_sv = cache_swa_v.reshape(n_odd, dec_batch, C_WINDOW, C_KWIDTH)

    c0 = A_Q_LORA + A_KV_LORA
    c1 = c0 + A_ROPE
    xs = {True: x_prompt.reshape(n_p, D_MODEL), False: x_sample.reshape(n_s, D_MODEL)}
    new = {True: [[] for _ in range(6)], False: [[] for _ in range(6)]}
    for l in range(depth):
        g = norm_g[l]
        i = l // 2
        if l % 2 == 0:
            w = even_w_in[i].astype(BF16)
            zc = lambda n: jnp.zeros((D_MODEL, n), BF16)
            win = jnp.concatenate([w[:, :c0], zc(A_NOPE), w[:, c0:c1], zc(LANES - A_NOPE - A_ROPE), w[:, c1:]], axis=1)
            wuq = jnp.pad(mla_w_uq[i].reshape(A_Q_LORA, A_HEADS, A_QK),
                          ((0, 0), (0, 0), (0, LANES - A_QK))).reshape(A_Q_LORA, A_HEADS * LANES).astype(BF16)
            wkv = mla_w_ukv[i].reshape(A_KV_LORA, A_HEADS, A_NOPE + A_V)
            wk = jnp.pad(wkv[:, :, :A_NOPE], ((0, 0), (0, 0), (0, LANES - A_NOPE))).reshape(A_KV_LORA, A_HEADS * LANES)
            wv = wkv[:, :, A_NOPE:].reshape(A_KV_LORA, A_HEADS * A_V)
            wukv = jnp.concatenate([wk, wv], axis=1).astype(BF16)
            bias, bias_t = _band_bias(band_rel_bias[i])
            w_band = B_REACH + BAND_TILE
            bias_pairs = bias_t.reshape(B_HEADS // 2, 2, w_band, BAND_TILE).transpose(0, 2, 1, 3).reshape(
                B_HEADS // 2, w_band, 2 * BAND_TILE)
            wo = even_w_out[i].astype(BF16)
        else:
            win = odd_w_in[i].astype(BF16)
            wo = odd_w_out[i].astype(BF16)
        for prompt in (True, False):
            x = _ffn(xs[prompt], g[0:2], wg, wu, wd, l, 0)
            st = new[prompt]
            if l % 2 == 0:
                qa, ka, va, qb, kb, vb, ckv, kpe, kbf, vbf = _even_proj(
                    x, g[2:3], win, mla_q_norm[i][None], mla_kv_norm[i][None], wuq, wukv, mla_tabs[prompt],
                    prompt, batch, seq)
                if prompt:
                    mixes = [_mla_prompt(qa, ka, va, batch, seq), _band_prompt(qb, kb, vb, bias_pairs, batch, seq)]
                    lead = (batch, seq)
                else:
                    mixes = _even_sample(i, qa, ka, va, qb, kb, vb, cache_mla_ckv, c_kpe_t, c_bk_t, c_bv_t,
                                         wukv, pick, bias, dec_batch, t_new)
                    lead = (dec_batch, t_new)
                st[0].append(ckv.reshape(*lead, A_KV_LORA))
                st[1].append(kpe.reshape(*lead, A_ROPE))
                st[2].append(kbf.reshape(lead[0], -1, B_HEADS, HEAD_DIM))
                st[3].append(vbf.reshape(lead[0], -1, B_HEADS, HEAD_DIM))
            else:
                q, k, v, kf, vf = _odd_proj(x, g[2:3], win, swa_tabs[prompt], prompt, batch, seq)
                if prompt:
                    mixes = [_swa_prompt(swa_sinks, i, q, k, v, batch, seq)]
                    lead = batch
                else:
                    mixes = [_odd_sample(swa_sinks, i, q, k, v, c_sk, c_sv, dec_batch, t_new)]
                    lead = dec_batch
                st[4].append(kf.reshape(lead, -1, C_KV_HEADS, HEAD_DIM))
                st[5].append(vf.reshape(lead, -1, C_KV_HEADS, HEAD_DIM))
            xs[prompt] = _mix_ffn(x, mixes, wo, g[3:4], g[4:6], wg, wu, wd, l, 1)

    y_prompt = xs[True].reshape(batch, seq, D_MODEL)
    y_sample = xs[False].reshape(dec_batch, t_new, D_MODEL)
    st_p = [jnp.stack(s, axis=0) for s in new[True]]
    st_s = [jnp.stack(s, axis=0) for s in new[False]]
    return (y_prompt, y_sample, *st_p, *st_s)
```

```python
import functools
import math

import jax
import jax.numpy as jnp
from jax import lax
from jax.experimental import pallas as pl
from jax.experimental.pallas import tpu as pltpu

F32 = jnp.float32
BF16 = jnp.bfloat16

D_MODEL = 1024
D_FF = 2816
CHUNK = 64
HEAD_DIM = 64
ROPE_THETA = 500000.0
NORM_EPS = 1e-6
NEG_INF = -1e30
A_HEADS = 8
A_Q_LORA = 256
A_KV_LORA = 128
A_NOPE = 64
A_ROPE = 32
A_QK = A_NOPE + A_ROPE
A_V = 64
A_SCALE = A_QK ** -0.5
B_HEADS = 8
B_REACH = 512
B_MAX_REL = 128
C_HEADS = 16
C_KV_HEADS = 2
C_WINDOW = 128
C_ROT = HEAD_DIM // 4
HEAD_SCALE = HEAD_DIM ** -0.5
LOG2E = math.log2(math.e)

LANES = 128
TOKEN_TILE = 512
FFN_TILE = 1024
FFN_PARTS = 2
FFN_ROWS = FFN_TILE // FFN_PARTS
FF_TILE = 256
PROJ_PARTS = 2
PROJ_ROWS = TOKEN_TILE // PROJ_PARTS
ATT_TILE = 128
BAND_TILE = 256
BAND_BLOCKS = 4
MLA_Q_TILE = 512
MLA_K_TILE = 256
BIAS_DIAG_LANES = 1024
SWA_BLOCKS = 8
ONES_ROWS = 16
VMEM_LIMIT = 56 * 1024 * 1024
B_WIDTH = B_HEADS * HEAD_DIM
C_QWIDTH = C_HEADS * HEAD_DIM
C_KWIDTH = C_KV_HEADS * HEAD_DIM
EVEN_CKV = A_Q_LORA
EVEN_KPE = EVEN_CKV + A_KV_LORA
EVEN_QB = EVEN_KPE + LANES
EVEN_KB = EVEN_QB + B_WIDTH
EVEN_VB = EVEN_KB + B_WIDTH


def _params(*sem):
    return pltpu.CompilerParams(dimension_semantics=sem, vmem_limit_bytes=VMEM_LIMIT)


def _const_spec(shape, index=None):
    index = index or (0,) * len(shape)
    return pl.BlockSpec(shape, lambda *_: index, pipeline_mode=pl.Buffered(1))


def _rms(x, g):
    ms = jnp.mean(x * x, axis=-1, keepdims=True)
    return x * lax.rsqrt(ms + NORM_EPS) * g


def _dot(a, b):
    return jnp.dot(a, b, preferred_element_type=F32)


def _dot_nt(a, b):
    return lax.dot_general(a, b, (((1,), (1,)), ((), ())), preferred_element_type=F32)


def _rope_group(x, c, s1, s2, half):
    return x * c + pltpu.roll(x, half, 1) * s1 + pltpu.roll(x, LANES - half, 1) * s2


def _lane_mask(lo):
    lane = lax.broadcasted_iota(jnp.int32, (1, LANES), 1)
    return (lane >= lo) & (lane < lo + HEAD_DIM)


def _swiglu_update(x, g_ref, wg_ref, wu_ref, wd_ref):
    xn = _rms(x, g_ref[0:1, :]).astype(BF16)
    acc = jnp.zeros(x.shape, F32)
    for c in range(D_FF // FF_TILE):
        sl = slice(c * FF_TILE, (c + 1) * FF_TILE)
        a = _dot(xn, wg_ref[:, sl])
        b = _dot(xn, wu_ref[:, sl])
        h = (a * jax.nn.sigmoid(a) * b).astype(BF16)
        acc = acc + _dot(h, wd_ref[sl, :])
    return x + 0.5 * _rms(acc, g_ref[1:2, :])


def _ffn_row_parts():
    return [slice(p * FFN_ROWS, (p + 1) * FFN_ROWS) for p in range(FFN_PARTS)]


def _ffn_kernel(x_ref, g_ref, wg_ref, wu_ref, wd_ref, o_ref):
    for rows in _ffn_row_parts():
        o_ref[rows, :] = _swiglu_update(x_ref[rows, :], g_ref, wg_ref, wu_ref, wd_ref)


def _mix_ffn_kernel(*refs, n_mix):
    x_ref = refs[0]
    mix_refs = refs[1:1 + n_mix]
    wo_ref, gm_ref, g_ref, wg_ref, wu_ref, wd_ref, o_ref = refs[1 + n_mix:]
    ys = []
    for rows in _ffn_row_parts():
        y = None
        row = 0
        for m_ref in mix_refs:
            w = m_ref.shape[1]
            part = _dot(m_ref[rows, :], wo_ref[row:row + w, :])
            y = part if y is None else y + part
            row += w
        ys.append(y)
    for rows, y in zip(_ffn_row_parts(), ys):
        x = x_ref[rows, :] + _rms(y, gm_ref[...])
        o_ref[rows, :] = _swiglu_update(x, g_ref, wg_ref, wu_ref, wd_ref)


def _ffn_weight_specs(layer, which):
    idx = (layer, which, 0, 0)
    return [_const_spec((None, None, D_MODEL, D_FF), idx), _const_spec((None, None, D_MODEL, D_FF), idx),
            _const_spec((None, None, D_FF, D_MODEL), idx)]


def _ffn(x, g, wg, wu, wd, layer, which):
    t = x.shape[0]
    tile = pl.BlockSpec((FFN_TILE, D_MODEL), lambda i: (i, 0))
    return pl.pallas_call(
        _ffn_kernel,
        out_shape=jax.ShapeDtypeStruct((t, D_MODEL), F32),
        grid=(t // FFN_TILE,),
        in_specs=[tile, _const_spec((2, D_MODEL))] + _ffn_weight_specs(layer, which),
        out_specs=tile,
        compiler_params=_params("parallel"),
        name="ffn",
    )(x, g, wg, wu, wd)


def _mix_ffn(x, mixes, wo, gm, g, wg, wu, wd, layer, which):
    t = x.shape[0]
    tile = pl.BlockSpec((FFN_TILE, D_MODEL), lambda i: (i, 0))
    mix_specs = [pl.BlockSpec((FFN_TILE, m.shape[1]), lambda i: (i, 0)) for m in mixes]
    return pl.pallas_call(
        functools.partial(_mix_ffn_kernel, n_mix=len(mixes)),
        out_shape=jax.ShapeDtypeStruct((t, D_MODEL), F32),
        grid=(t // FFN_TILE,),
        in_specs=[tile] + mix_specs + [_const_spec(wo.shape), _const_spec((1, D_MODEL)), _const_spec((2, D_MODEL))]
                 + _ffn_weight_specs(layer, which),
        out_specs=tile,
        compiler_params=_params("parallel"),
        name="mix_ffn",
    )(x, *mixes, wo, gm, g, wg, wu, wd)


def _even_proj_kernel(x_ref, g_ref, win_ref, qn_ref, kvn_ref, wuq_ref, wukv_ref,
                      c_ref, s1_ref, s2_ref,
                      qa_ref, ka_ref, va_ref, qb_ref, kb_ref, vb_ref,
                      ckv_ref, kpe_ref, kbf_ref, vbf_ref, *scratch, prompt, tiles_per_seq):
    half = A_ROPE // 2
    for part in range(PROJ_PARTS):
        rows = slice(part * PROJ_ROWS, (part + 1) * PROJ_ROWS)
        h = _rms(x_ref[rows, :], g_ref[...]).astype(BF16)
        proj = _dot(h, win_ref[...])
        cq = _rms(proj[:, 0:EVEN_CKV], qn_ref[...]).astype(BF16)
        ckv = _rms(proj[:, EVEN_CKV:EVEN_KPE], kvn_ref[...])
        c, s1, s2 = c_ref[rows, :], s1_ref[rows, :], s2_ref[rows, :]
        kpe = _rope_group(proj[:, EVEN_KPE:EVEN_QB], c, s1, s2, half)
        ckv_ref[rows, :] = ckv
        kpe_ref[rows, :] = pltpu.roll(kpe, LANES - A_NOPE, 1)[:, 0:A_ROPE]
        qa = _dot(cq, wuq_ref[...])
        kv = _dot(ckv.astype(BF16), wukv_ref[...])
        for hd in range(A_HEADS):
            sl = slice(hd * LANES, (hd + 1) * LANES)
            qa_ref[rows, sl] = (_rope_group(qa[:, sl], c, s1, s2, half) * (A_SCALE * LOG2E)).astype(BF16)
            ka_ref[rows, sl] = (kv[:, sl] + kpe).astype(BF16)
        for pair in range(B_HEADS // 2):
            grp = proj[:, EVEN_QB + pair * LANES:EVEN_QB + (pair + 1) * LANES] * (HEAD_SCALE * LOG2E)
            for sub in range(2):
                qb_ref[2 * pair + sub, rows, :] = jnp.where(_lane_mask(sub * HEAD_DIM), grp, 0.0).astype(BF16)
        va = kv[:, A_HEADS * LANES:]
        kb = proj[:, EVEN_KB:EVEN_VB]
        vb = proj[:, EVEN_VB:EVEN_VB + B_WIDTH]
        kb_ref[rows, :] = kb.astype(BF16)
        if prompt:
            va_ref[:, rows] = va.T.astype(BF16)
            vb_ref[:, rows] = vb.T.astype(BF16)
            scratch[0][rows, :] = kb
            scratch[1][rows, :] = vb
        else:
            va_ref[rows, :] = va.astype(BF16)
            vb_ref[rows, :] = vb.astype(BF16)
            kbf_ref[rows, :] = kb
            vbf_ref[rows, :] = vb
    if prompt:
        @pl.when(pl.program_id(0) % tiles_per_seq == tiles_per_seq - 1)
        def _():
            kbf_ref[...] = scratch[0][...]
            vbf_ref[...] = scratch[1][...]


def _last_tile_index(tiles_per_seq):
    return lambda i: (jnp.maximum((i + 1) // tiles_per_seq - 1, 0), 0)


def _even_proj(x, g, win, qn, kvn, wuq, wukv, tabs, prompt, batch, seq):
    t = x.shape[0]
    tiles_per_seq = seq // TOKEN_TILE if prompt else 1

    def tile(w):
        return pl.BlockSpec((TOKEN_TILE, w), lambda i: (i, 0))

    def tile_t(w):
        return pl.BlockSpec((w, TOKEN_TILE), lambda i: (0, i))

    def sds(shape, dtype):
        return jax.ShapeDtypeStruct(shape, dtype)

    wa = A_HEADS * LANES
    if prompt:
        v_shapes = [sds((A_HEADS * A_V, t), BF16), sds((B_WIDTH, t), BF16)]
        v_specs = [tile_t(A_HEADS * A_V), tile_t(B_WIDTH)]
        keep_shape = sds((batch * TOKEN_TILE, B_WIDTH), F32)
        keep_spec = pl.BlockSpec((TOKEN_TILE, B_WIDTH), _last_tile_index(tiles_per_seq))
        tab_idx = lambda i: (i % tiles_per_seq, 0)
    else:
        v_shapes = [sds((t, A_HEADS * A_V), BF16), sds((t, B_WIDTH), BF16)]
        v_specs = [tile(A_HEADS * A_V), tile(B_WIDTH)]
        keep_shape = sds((t, B_WIDTH), F32)
        keep_spec = tile(B_WIDTH)
        tab_idx = lambda i: (0, 0)
    qb_spec = pl.BlockSpec((B_HEADS, TOKEN_TILE, LANES), lambda i: (0, i, 0))
    out_shape = [sds((t, wa), BF16), sds((t, wa), BF16), v_shapes[0], sds((B_HEADS, t, LANES), BF16),
                 sds((t, B_WIDTH), BF16), v_shapes[1], sds((t, A_KV_LORA), F32), sds((t, A_ROPE), F32),
                 keep_shape, keep_shape]
    out_specs = [tile(wa), tile(wa), v_specs[0], qb_spec, tile(B_WIDTH), v_specs[1],
                 tile(A_KV_LORA), tile(A_ROPE), keep_spec, keep_spec]
    return pl.pallas_call(
        functools.partial(_even_proj_kernel, prompt=prompt, tiles_per_seq=tiles_per_seq),
        out_shape=out_shape,
        grid=(t // TOKEN_TILE,),
        in_specs=[tile(D_MODEL), _const_spec((1, D_MODEL)), _const_spec(win.shape),
                  _const_spec((1, A_Q_LORA)), _const_spec((1, A_KV_LORA)),
                  _const_spec(wuq.shape), _const_spec(wukv.shape)]
                 + [pl.BlockSpec((TOKEN_TILE, LANES), tab_idx)] * 3,
        out_specs=out_specs,
        scratch_shapes=[pltpu.VMEM((TOKEN_TILE, B_WIDTH), F32)] * 2 if prompt else [],
        compiler_params=_params("arbitrary"),
        name="even_proj",
    )(x, g, win, qn, kvn, wuq, wukv, *tabs)


def _odd_proj_kernel(x_ref, g_ref, win_ref, c_ref, s1_ref, s2_ref,
                     q_ref, k_ref, v_ref, kf_ref, vf_ref, *scratch, prompt, tiles_per_seq):
    half = C_ROT // 2
    group = C_HEADS // C_KV_HEADS
    for part in range(PROJ_PARTS):
        rows = slice(part * PROJ_ROWS, (part + 1) * PROJ_ROWS)
        h = _rms(x_ref[rows, :], g_ref[...]).astype(BF16)
        proj = _dot(h, win_ref[...])
        c, s1, s2 = c_ref[rows, :], s1_ref[rows, :], s2_ref[rows, :]
        k = _rope_group(proj[:, C_QWIDTH:C_QWIDTH + LANES], c, s1, s2, half)
        v = proj[:, C_QWIDTH + LANES:C_QWIDTH + 2 * LANES]
        k_ref[rows, :] = k.astype(BF16)
        if prompt:
            for pair in range(C_HEADS // 2):
                grp = _rope_group(proj[:, pair * LANES:(pair + 1) * LANES], c, s1, s2, half) * (HEAD_SCALE * LOG2E)
                swapped = pltpu.roll(grp, HEAD_DIM, 1)
                for sub in range(2):
                    hd = 2 * pair + sub
                    kv = hd // group
                    src = grp if sub == kv else swapped
                    q_ref[hd, rows, :] = jnp.where(_lane_mask(kv * HEAD_DIM), src, 0.0).astype(BF16)
            v_ref[:, rows] = v.T.astype(BF16)
            if part == PROJ_PARTS - 1:
                scratch[0][...] = k[PROJ_ROWS - C_WINDOW:, :]
                scratch[1][...] = v[PROJ_ROWS - C_WINDOW:, :]
        else:
            for grp in range(C_QWIDTH // LANES):
                sl = slice(grp * LANES, (grp + 1) * LANES)
                q_ref[rows, sl] = (_rope_group(proj[:, sl], c, s1, s2, half) * (HEAD_SCALE * LOG2E)).astype(BF16)
            v_ref[rows, :] = v.astype(BF16)
            kf_ref[rows, :] = k
            vf_ref[rows, :] = v
    if prompt:
        @pl.when(pl.program_id(0) % tiles_per_seq == tiles_per_seq - 1)
        def _():
            kf_ref[...] = scratch[0][...]
            vf_ref[...] = scratch[1][...]


def _odd_proj(x, g, win, tabs, prompt, batch, seq):
    t = x.shape[0]
    tiles_per_seq = seq // TOKEN_TILE if prompt else 1

    def tile(w):
        return pl.BlockSpec((TOKEN_TILE, w), lambda i: (i, 0))

    def sds(shape, dtype):
        return jax.ShapeDtypeStruct(shape, dtype)

    if prompt:
        q_shape = sds((C_HEADS, t, LANES), BF16)
        q_spec = pl.BlockSpec((C_HEADS, TOKEN_TILE, LANES), lambda i: (0, i, 0))
        v_shape, v_spec = sds((LANES, t), BF16), pl.BlockSpec((LANES, TOKEN_TILE), lambda i: (0, i))
        keep_shape = sds((batch * C_WINDOW, LANES), F32)
        keep_spec = pl.BlockSpec((C_WINDOW, LANES), _last_tile_index(tiles_per_seq))
        tab_idx = lambda i: (i % tiles_per_seq, 0)
    else:
        q_shape, q_spec = sds((t, C_QWIDTH), BF16), tile(C_QWIDTH)
        v_shape, v_spec = sds((t, LANES), BF16), tile(LANES)
        keep_shape, keep_spec = sds((t, LANES), F32), tile(LANES)
        tab_idx = lambda i: (0, 0)
    return pl.pallas_call(
        functools.partial(_odd_proj_kernel, prompt=prompt, tiles_per_seq=tiles_per_seq),
        out_shape=[q_shape, sds((t, LANES), BF16), v_shape, keep_shape, keep_shape],
        grid=(t // TOKEN_TILE,),
        in_specs=[tile(D_MODEL), _const_spec((1, D_MODEL)), _const_spec(win.shape)]
                 + [pl.BlockSpec((TOKEN_TILE, LANES), tab_idx)] * 3,
        out_specs=[q_spec, tile(LANES), v_spec, keep_spec, keep_spec],
        scratch_shapes=[pltpu.VMEM((C_WINDOW, LANES), F32)] * 2 if prompt else [],
        compiler_params=_params("arbitrary"),
        name="odd_proj",
    )(x, g, win, *tabs)


def _band_bias_kernel(tab_ref, o_ref, ot_ref):
    hd = pl.program_id(0)
    w = B_REACH + BAND_TILE
    row = lax.broadcasted_iota(jnp.int32, (BAND_TILE, w), 0)
    col = lax.broadcasted_iota(jnp.int32, (BAND_TILE, w), 1)
    lo = (row // CHUNK) * CHUNK
    visible = (col >= lo) & (col < lo + B_REACH + CHUNK)
    n = BIAS_DIAG_LANES
    e = lax.broadcasted_iota(jnp.int32, (1, n), 1)
    rel = jnp.clip(B_REACH + BAND_TILE - 1 - e, -B_MAX_REL, B_MAX_REL) + B_MAX_REL

    def body(r, acc):
        return jnp.where(rel == r, tab_ref[hd, r], acc)

    diag = lax.fori_loop(0, 2 * B_MAX_REL + 1, body, jnp.zeros((1, n), F32))
    shifted = pltpu.roll(jnp.broadcast_to(diag, (BAND_TILE, n)), n - (BAND_TILE - 1), 1, stride=1, stride_axis=0)
    bias = jnp.where(visible, shifted[:, 0:w] * LOG2E, NEG_INF)
    o_ref[...] = bias
    ot_ref[...] = bias.T


def _band_bias(table):
    w = B_REACH + BAND_TILE
    return pl.pallas_call(
        _band_bias_kernel,
        out_shape=[jax.ShapeDtypeStruct((B_HEADS, BAND_TILE, w), F32),
                   jax.ShapeDtypeStruct((B_HEADS, w, BAND_TILE), F32)],
        grid=(B_HEADS,),
        in_specs=[pl.BlockSpec(memory_space=pltpu.SMEM)],
        out_specs=[pl.BlockSpec((None, BAND_TILE, w), lambda h: (h, 0, 0)),
                   pl.BlockSpec((None, w, BAND_TILE), lambda h: (h, 0, 0))],
        compiler_params=_params("arbitrary"),
        name="band_bias",
    )(table)


def _ones_rows(n):
    return jnp.ones((ONES_ROWS, n), BF16)


def _mla_prompt_kernel(q_ref, k_ref, vt_ref, o_ref, m_ref, l_ref, acc_ref, s0_ref, s1_ref, smax0_ref, smax1_ref):
    i = pl.program_id(1)
    tq, tk = MLA_Q_TILE, MLA_K_TILE
    m_ref[...] = jnp.full(m_ref.shape, NEG_INF, F32)
    l_ref[...] = jnp.zeros(l_ref.shape, F32)
    acc_ref[...] = jnp.zeros(acc_ref.shape, F32)
    per_q = tq // tk
    assert per_q == 2
    s_refs = (s0_ref, s1_ref)
    smax_refs = (smax0_ref, smax1_ref)

    def live_queries(diag):
        return slice(0 if diag is None else diag * tk, tq)

    def score(j, slot, hd, diag=None):
        ks = pl.multiple_of(j * tk, tk)
        sl = slice(hd * LANES, (hd + 1) * LANES)
        cols = live_queries(diag)
        s = _dot_nt(k_ref[pl.ds(ks, tk), sl], q_ref[cols, sl])
        s_refs[slot][hd, :, cols] = s
        if diag is None:
            smax_refs[slot][hd] = jnp.max(s, axis=0, keepdims=True)

    def block(j, slot, diag, next_diag, stage_next=True):
        ks = pl.multiple_of(j * tk, tk)
        cols = live_queries(diag)
        m_all = [m_ref[hd, :, cols] for hd in range(A_HEADS)]
        l_all = [l_ref[hd, :, cols] for hd in range(A_HEADS)]
        acc_all = [acc_ref[hd, :, cols] for hd in range(A_HEADS)]
        ones = _ones_rows(tk)
        if diag is not None:
            shape = (tk, tq - cols.start)
            key_chunk = (diag * tk + lax.broadcasted_iota(jnp.int32, shape, 0)) // CHUNK
            qry_chunk = (cols.start + lax.broadcasted_iota(jnp.int32, shape, 1)) // CHUNK
            visible = key_chunk <= qry_chunk
        new = []
        for hd in range(A_HEADS):
            if stage_next:
                score(j + 1, 1 - slot, hd, next_diag)
            s = s_refs[slot][hd, :, cols]
            if diag is None:
                s_max = smax_refs[slot][hd]
            else:
                s = jnp.where(visible, s, NEG_INF)
                s_max = jnp.max(s, axis=0, keepdims=True)
            m_new = jnp.maximum(m_all[hd], s_max)
            alpha = jnp.exp2(m_all[hd] - m_new)
            p = jnp.exp2(s - m_new).astype(BF16)
            vt = jnp.concatenate([vt_ref[hd * A_V:(hd + 1) * A_V, pl.ds(ks, tk)], ones], axis=0)
            pv = _dot(vt, p)
            new.append((m_new, alpha * l_all[hd] + pv[A_V:A_V + 1], alpha * acc_all[hd] + pv[0:A_V]))
        for hd, (m_new, l_new, acc_new) in enumerate(new):
            m_ref[hd, :, cols] = m_new
            l_ref[hd, :, cols] = l_new
            acc_ref[hd, :, cols] = acc_new

    for hd in range(A_HEADS):
        score(0, 0, hd)

    def body(pair, carry):
        for d in range(per_q):
            block(pair * per_q + d, d, None, None)
        return carry

    lax.fori_loop(0, i, body, 0)
    for d in range(per_q):
        block(i * per_q + d, d, d, d + 1, stage_next=d + 1 < per_q)
    outs = [acc_ref[hd] / l_ref[hd] for hd in range(A_HEADS)]
    o_ref[...] = jnp.concatenate(outs, axis=0).T.astype(BF16)


def _mla_prompt(qa, ka, vat, batch, seq):
    nq = seq // MLA_Q_TILE
    wa = A_HEADS * LANES
    wv = A_HEADS * A_V
    return pl.pallas_call(
        _mla_prompt_kernel,
        out_shape=jax.ShapeDtypeStruct((batch * seq, wv), BF16),
        grid=(batch, nq),
        in_specs=[pl.BlockSpec((MLA_Q_TILE, wa), lambda b, i: (b * nq + i, 0)),
                  pl.BlockSpec((seq, wa), lambda b, i: (b, 0)),
                  pl.BlockSpec((wv, seq), lambda b, i: (0, b))],
        out_specs=pl.BlockSpec((MLA_Q_TILE, wv), lambda b, i: (b * nq + i, 0)),
        scratch_shapes=[pltpu.VMEM((A_HEADS, 1, MLA_Q_TILE), F32), pltpu.VMEM((A_HEADS, 1, MLA_Q_TILE), F32),
                        pltpu.VMEM((A_HEADS, A_V, MLA_Q_TILE), F32),
                        pltpu.VMEM((A_HEADS, MLA_K_TILE, MLA_Q_TILE), F32),
                        pltpu.VMEM((A_HEADS, MLA_K_TILE, MLA_Q_TILE), F32),
                        pltpu.VMEM((A_HEADS, 1, MLA_Q_TILE), F32), pltpu.VMEM((A_HEADS, 1, MLA_Q_TILE), F32)],
        compiler_params=_params("parallel", "arbitrary"),
        name="mla_prompt",
    )(qa, ka, vat)


def _fill_padded(kpad, vtpad, k_ref, vt_ref, reach):
    kpad[0:reach, :] = jnp.zeros((reach, kpad.shape[1]), kpad.dtype)
    kpad[reach:, :] = k_ref[...]
    vtpad[:, 0:reach] = jnp.zeros((vtpad.shape[0], reach), vtpad.dtype)
    vtpad[:, reach:] = vt_ref[...]


def _window_attend(s, vt, sink=None):
    m = jnp.max(s, axis=0, keepdims=True)
    if sink is not None:
        m = jnp.maximum(m, sink)
    p = jnp.exp2(s - m).astype(BF16)
    d = vt.shape[0]
    pv = _dot(jnp.concatenate([vt, _ones_rows(vt.shape[1])], axis=0), p)
    l = pv[d:d + 1]
    if sink is not None:
        l = l + jnp.exp2(sink - m)
    return pv[0:d] * (1.0 / l)


def _band_prompt_kernel(q_ref, k_ref, vt_ref, bias_ref, o_ref, kpad, vtpad, s0_ref, s1_ref):
    i = pl.program_id(1)
    tq = BAND_TILE
    s_refs = (s0_ref, s1_ref)

    @pl.when(i == 0)
    def _():
        _fill_padded(kpad, vtpad, k_ref, vt_ref, B_REACH)

    w = B_REACH + tq
    n_pairs = B_HEADS // 2
    step = BAND_BLOCKS * tq

    def attend(mask_start):
        key = lax.broadcasted_iota(jnp.int32, (w, 2 * tq), 0)

        def first_row(blk):
            return pl.multiple_of((i * BAND_BLOCKS + blk) * tq, tq)

        def score(unit):
            blk, pair = divmod(unit, n_pairs)
            q2 = q_ref[2 * pair:2 * pair + 2, blk * tq:(blk + 1) * tq, :].reshape(2 * tq, LANES)
            s_refs[unit % 2][...] = _dot_nt(kpad[pl.ds(first_row(blk), w), pair * LANES:(pair + 1) * LANES], q2)

        n_units = BAND_BLOCKS * n_pairs
        score(0)
        for unit in range(n_units):
            if unit + 1 < n_units:
                score(unit + 1)
            blk, pair = divmod(unit, n_pairs)
            if pair == 0:
                outs = []
            s = s_refs[unit % 2][...] + bias_ref[pair]
            if mask_start:
                s = jnp.where(key >= B_REACH - first_row(blk), s, NEG_INF)
            o2 = _window_attend(s, vtpad[pair * LANES:(pair + 1) * LANES, pl.ds(first_row(blk), w)])
            outs.append(o2[0:HEAD_DIM, 0:tq])
            outs.append(o2[HEAD_DIM:, tq:])
            if pair == n_pairs - 1:
                o_ref[blk * tq:(blk + 1) * tq, :] = jnp.concatenate(outs, axis=0).T.astype(BF16)

    pl.when(i * step < B_REACH)(functools.partial(attend, True))
    pl.when(i * step >= B_REACH)(functools.partial(attend, False))


def _band_prompt(qb, kb, vbt, bias_pairs, batch, seq):
    step = BAND_BLOCKS * BAND_TILE
    nq = seq // step
    return pl.pallas_call(
        _band_prompt_kernel,
        out_shape=jax.ShapeDtypeStruct((batch * seq, B_WIDTH), BF16),
        grid=(batch, nq),
        in_specs=[pl.BlockSpec((B_HEADS, step, LANES), lambda b, i: (0, b * nq + i, 0)),
                  pl.BlockSpec((seq, B_WIDTH), lambda b, i: (b, 0)),
                  pl.BlockSpec((B_WIDTH, seq), lambda b, i: (0, b)),
                  _const_spec(bias_pairs.shape)],
        out_specs=pl.BlockSpec((step, B_WIDTH), lambda b, i: (b * nq + i, 0)),
        scratch_shapes=[pltpu.VMEM((seq + B_REACH, B_WIDTH), BF16), pltpu.VMEM((B_WIDTH, seq + B_REACH), BF16)]
                       + [pltpu.VMEM((B_REACH + BAND_TILE, 2 * BAND_TILE), F32)] * 2,
        compiler_params=_params("parallel", "arbitrary"),
        name="band_prompt",
    )(qb, kb, vbt, bias_pairs)


def _swa_prompt_kernel(sink_ref, q_ref, k_ref, vt_ref, o_ref, kpad, vtpad, s0_ref, s1_ref, *, layer):
    i = pl.program_id(1)
    s_refs = (s0_ref, s1_ref)

    @pl.when(i == 0)
    def _():
        _fill_padded(kpad, vtpad, k_ref, vt_ref, C_WINDOW)

    w = C_WINDOW + ATT_TILE
    group = C_HEADS // C_KV_HEADS
    key = lax.broadcasted_iota(jnp.int32, (w, ATT_TILE), 0)
    qry = lax.broadcasted_iota(jnp.int32, (w, ATT_TILE), 1)
    lo = (qry // CHUNK) * CHUNK
    in_window = (key >= lo) & (key < lo + C_WINDOW + CHUNK)

    def score(unit):
        blk, kv = divmod(unit, C_KV_HEADS)
        start = pl.multiple_of((i * SWA_BLOCKS + blk) * ATT_TILE, ATT_TILE)
        q = q_ref[kv * group:(kv + 1) * group, blk * ATT_TILE:(blk + 1) * ATT_TILE, :]
        s_refs[unit % 2][...] = _dot_nt(kpad[pl.ds(start, w), :], q.reshape(group * ATT_TILE, LANES))

    n_units = SWA_BLOCKS * C_KV_HEADS
    score(0)
    for unit in range(n_units):
        if unit + 1 < n_units:
            score(unit + 1)
        blk, kv = divmod(unit, C_KV_HEADS)
        if kv == 0:
            first = (i * SWA_BLOCKS + blk) * ATT_TILE
            start = pl.multiple_of(first, ATT_TILE)
            visible = in_window & (key >= C_WINDOW - first)
            mask = jnp.concatenate([jnp.where(visible, 0.0, NEG_INF)] * group, axis=1)
            outs = []
        sink = jnp.concatenate([jnp.full((1, ATT_TILE), sink_ref[layer, kv * group + g] * LOG2E, F32)
                                for g in range(group)], axis=1)
        og = _window_attend(s_refs[unit % 2][...] + mask, vtpad[kv * HEAD_DIM:(kv + 1) * HEAD_DIM, pl.ds(start, w)],
                            sink)
        outs.extend(og[:, g * ATT_TILE:(g + 1) * ATT_TILE] for g in range(group))
        if kv == C_KV_HEADS - 1:
            o_ref[blk * ATT_TILE:(blk + 1) * ATT_TILE, :] = jnp.concatenate(outs, axis=0).T.astype(BF16)


def _swa_prompt(sinks, layer, q, k, vt, batch, seq):
    step = SWA_BLOCKS * ATT_TILE
    nq = seq // step
    return pl.pallas_call(
        functools.partial(_swa_prompt_kernel, layer=layer),
        out_shape=jax.ShapeDtypeStruct((batch * seq, C_QWIDTH), BF16),
        grid=(batch, nq),
        in_specs=[pl.BlockSpec(memory_space=pltpu.SMEM),
                  pl.BlockSpec((C_HEADS, step, LANES), lambda b, i: (0, b * nq + i, 0)),
                  pl.BlockSpec((seq, C_KWIDTH), lambda b, i: (b, 0)),
                  pl.BlockSpec((C_KWIDTH, seq), lambda b, i: (0, b))],
        out_specs=pl.BlockSpec((step, C_QWIDTH), lambda b, i: (b * nq + i, 0)),
        scratch_shapes=[pltpu.VMEM((seq + C_WINDOW, C_KWIDTH), BF16), pltpu.VMEM((C_KWIDTH, seq + C_WINDOW), BF16)]
                       + [pltpu.VMEM((C_WINDOW + ATT_TILE, C_HEADS // C_KV_HEADS * ATT_TILE), F32)] * 2,
        compiler_params=_params("parallel", "arbitrary"),
        name="swa_prompt",
    )(sinks, q, k, vt)


def _softmax_pv2(s_c, s_n, v_c, v_n, sink=None, v_c_transposed=False):
    m = jnp.maximum(jnp.max(s_c, axis=-1, keepdims=True), jnp.max(s_n, axis=-1, keepdims=True))
    if sink is not None:
        m = jnp.maximum(m, sink)
    p_c = jnp.exp2(s_c - m)
    p_n = jnp.exp2(s_n - m)
    l = jnp.sum(p_c, axis=-1, keepdims=True) + jnp.sum(p_n, axis=-1, keepdims=True)
    if sink is not None:
        l = l + jnp.exp2(sink - m)
    pv_c = _dot_nt(p_c.astype(BF16), v_c) if v_c_transposed else _dot(p_c.astype(BF16), v_c)
    return (pv_c + _dot(p_n.astype(BF16), v_n)) / l


def _even_sample_kernel(qa_ref, ka_ref, va_ref, qb_ref, kb_ref, vb_ref,
                        cckv_ref, ckpe_ref, cbk_ref, cbv_ref, wukv_ref, pick_ref, bias_ref,
                        oa_ref, ob_ref):
    t = qa_ref.shape[0]
    nb = cbk_ref.shape[1]
    kvc = _dot(cckv_ref[...].astype(BF16), wukv_ref[...])
    kpe_t = ckpe_ref[...].astype(BF16)
    scores_a, scores_b = [], []
    for hd in range(A_HEADS):
        sl = slice(hd * LANES, (hd + 1) * LANES)
        q = qa_ref[:, sl]
        q_rope = _dot(q, pick_ref[...]).astype(BF16)
        s_c = _dot_nt(q, kvc[:, sl].astype(BF16)) + _dot(q_rope, kpe_t)
        scores_a.append((s_c, _dot_nt(q, ka_ref[:, sl])))
    for hd in range(B_HEADS):
        pair = slice((hd // 2) * LANES, (hd // 2 + 1) * LANES)
        q = qb_ref[hd]
        scores_b.append((_dot(q, cbk_ref[pair, :].astype(BF16)) + bias_ref[hd, 0:t, 0:nb],
                         _dot_nt(q, kb_ref[:, pair]) + bias_ref[hd, 0:t, nb:nb + t]))
    outs_a, outs_b = [], []
    for hd in range(A_HEADS):
        vsl = slice(hd * A_V, (hd + 1) * A_V)
        v_c = kvc[:, A_HEADS * LANES + hd * A_V:A_HEADS * LANES + (hd + 1) * A_V].astype(BF16)
        outs_a.append(_softmax_pv2(*scores_a[hd], v_c, va_ref[:, vsl]))
    for hd in range(B_HEADS):
        sl = slice(hd * HEAD_DIM, (hd + 1) * HEAD_DIM)
        outs_b.append(_softmax_pv2(*scores_b[hd], cbv_ref[sl, :].astype(BF16), vb_ref[:, sl], v_c_transposed=True))
    oa_ref[...] = jnp.concatenate(outs_a, axis=-1).astype(BF16)
    ob_ref[...] = jnp.concatenate(outs_b, axis=-1).astype(BF16)


def _even_sample(layer, qa, ka, va, qb, kb, vb, c_ckv, c_kpe_t, c_bk_t, c_bv_t, wukv, pick, bias, dec_batch, t):
    past = c_ckv.shape[2]
    nb = c_bk_t.shape[3]

    def new(w):
        return pl.BlockSpec((t, w), lambda b: (b, 0))

    def cache(n, w):
        return pl.BlockSpec((None, None, n, w), lambda b: (layer, b, 0, 0))

    return pl.pallas_call(
        _even_sample_kernel,
        out_shape=[jax.ShapeDtypeStruct((dec_batch * t, B_WIDTH), BF16)] * 2,
        grid=(dec_batch,),
        in_specs=[new(A_HEADS * LANES), new(A_HEADS * LANES), new(A_HEADS * A_V),
                  pl.BlockSpec((B_HEADS, t, LANES), lambda b: (0, b, 0)), new(B_WIDTH), new(B_WIDTH),
                  cache(past, A_KV_LORA), cache(A_ROPE, past), cache(B_WIDTH, nb), cache(B_WIDTH, nb),
                  _const_spec(wukv.shape), _const_spec(pick.shape), _const_spec(bias.shape)],
        out_specs=[new(B_WIDTH), new(B_WIDTH)],
        compiler_params=_params("parallel"),
        name="even_sample",
    )(qa, ka, va, qb, kb, vb, c_ckv, c_kpe_t, c_bk_t, c_bv_t, wukv, pick, bias)


def _odd_sample_kernel(sink_ref, q_ref, k_ref, v_ref, ck_ref, cv_ref, o_ref, *, layer):
    group = C_HEADS // C_KV_HEADS
    scores = []
    for hd in range(C_HEADS):
        ksl = slice((hd // group) * HEAD_DIM, (hd // group + 1) * HEAD_DIM)
        q = q_ref[:, hd * HEAD_DIM:(hd + 1) * HEAD_DIM]
        scores.append((_dot_nt(q, ck_ref[:, ksl].astype(BF16)), _dot_nt(q, k_ref[:, ksl])))
    outs = []
    for hd in range(C_HEADS):
        ksl = slice((hd // group) * HEAD_DIM, (hd // group + 1) * HEAD_DIM)
        outs.append(_softmax_pv2(*scores[hd], cv_ref[:, ksl].astype(BF16), v_ref[:, ksl],
                                 sink_ref[layer, hd] * LOG2E))
    o_ref[...] = jnp.concatenate(outs, axis=-1).astype(BF16)


def _odd_sample(sinks, layer, q, k, v, c_k, c_v, dec_batch, t):
    nc = c_k.shape[2]

    def new(w):
        return pl.BlockSpec((t, w), lambda b: (b, 0))

    def cache():
        return pl.BlockSpec((None, None, nc, C_KWIDTH), lambda b: (layer, b, 0, 0))

    return pl.pallas_call(
        functools.partial(_odd_sample_kernel, layer=layer),
        out_shape=jax.ShapeDtypeStruct((dec_batch * t, C_QWIDTH), BF16),
        grid=(dec_batch,),
        in_specs=[pl.BlockSpec(memory_space=pltpu.SMEM), new(C_QWIDTH), new(C_KWIDTH), new(C_KWIDTH),
                  cache(), cache()],
        out_specs=new(C_QWIDTH),
        compiler_params=_params("parallel"),
        name="odd_sample",
    )(sinks, q, k, v, c_k, c_v)


def _rope_parts(pos, n_rot):
    half = n_rot // 2
    inv = ROPE_THETA ** (-jnp.arange(half, dtype=F32) / half)
    ang = pos.astype(F32)[:, None] * inv[None, :]
    return jnp.cos(ang), jnp.sin(ang)


def _rope_tables(pos, n_rot, pre, width):
    cos, sin = _rope_parts(pos, n_rot)
    n = pos.shape[0]
    post = width - pre - n_rot

    def head(first, second, fill):
        return jnp.concatenate([jnp.full((n, pre), fill, F32), first, second, jnp.full((n, post), fill, F32)], axis=1)

    zero = jnp.zeros_like(sin)
    tabs = (head(cos, cos, 1.0), head(zero, sin, 0.0), head(-sin, zero, 0.0))
    return tuple(jnp.tile(t, (1, LANES // width)) for t in tabs)


def kernel(x_prompt, x_sample, cache_mla_ckv, cache_mla_kpe, cache_band_k, cache_band_v,
           cache_swa_k, cache_swa_v, norm_g, ffn_w_gate, ffn_w_up, ffn_w_down, even_w_in,
           mla_q_norm, mla_w_uq, mla_kv_norm, mla_w_ukv, band_rel_bias, even_w_out,
           odd_w_in, swa_sinks, odd_w_out):
    batch, seq, _ = x_prompt.shape
    dec_batch, t_new, _ = x_sample.shape
    depth = norm_g.shape[0]
    n_even = even_w_in.shape[0]
    n_odd = odd_w_in.shape[0]
    past = cache_mla_ckv.shape[2]
    n_p = batch * seq
    n_s = dec_batch * t_new
    assert seq % TOKEN_TILE == 0 and n_s % TOKEN_TILE == 0 and TOKEN_TILE % t_new == 0
    assert n_p % FFN_TILE == 0 and n_s % FFN_TILE == 0
    assert seq % MLA_Q_TILE == 0 and seq % (BAND_BLOCKS * BAND_TILE) == 0 and seq % (SWA_BLOCKS * ATT_TILE) == 0 and min(B_REACH, seq) == TOKEN_TILE and t_new <= CHUNK
    assert cache_band_k.shape[2] == B_REACH and cache_swa_k.shape[2] == C_WINDOW and past >= B_REACH

    pos_p = jnp.arange(seq, dtype=jnp.int32)
    pos_s = past + (jnp.arange(TOKEN_TILE, dtype=jnp.int32) % t_new)
    mla_tabs = {True: _rope_tables(pos_p, A_ROPE, A_NOPE, LANES), False: _rope_tables(pos_s, A_ROPE, A_NOPE, LANES)}
    swa_tabs = {True: _rope_tables(pos_p, C_ROT, 0, HEAD_DIM), False: _rope_tables(pos_s, C_ROT, 0, HEAD_DIM)}

    wg = ffn_w_gate.astype(BF16)
    wu = ffn_w_up.astype(BF16)
    wd = ffn_w_down.astype(BF16)

    pick = jnp.zeros((LANES, A_ROPE), F32).at[A_NOPE + jnp.arange(A_ROPE), jnp.arange(A_ROPE)].set(1.0).astype(BF16)
    c_kpe_t = cache_mla_kpe.transpose(0, 1, 3, 2)
    c_bk_t = cache_band_k.transpose(0, 1, 3, 4, 2).reshape(n_even, dec_batch, B_WIDTH, B_REACH)
    c_bv_t = cache_band_v.transpose(0, 1, 3, 4, 2).reshape(n_even, dec_batch, B_WIDTH, B_REACH)
    c_sk = cache_swa_k.reshape(n_odd, dec_batch, C_WINDOW, C_KWIDTH)
    c_sv = cache_swa_v.reshape(n_odd, dec_batch, C_WINDOW, C_KWIDTH)

    c0 = A_Q_LORA + A_KV_LORA
    c1 = c0 + A_ROPE
    xs = {True: x_prompt.reshape(n_p, D_MODEL), False: x_sample.reshape(n_s, D_MODEL)}
    new = {True: [[] for _ in range(6)], False: [[] for _ in range(6)]}
    for l in range(depth):
        g = norm_g[l]
        i = l // 2
        if l % 2 == 0:
            w = even_w_in[i].astype(BF16)
            zc = lambda n: jnp.zeros((D_MODEL, n), BF16)
            win = jnp.concatenate([w[:, :c0], zc(A_NOPE), w[:, c0:c1], zc(LANES - A_NOPE - A_ROPE), w[:, c1:]], axis=1)
            wuq = jnp.pad(mla_w_uq[i].reshape(A_Q_LORA, A_HEADS, A_QK),
                          ((0, 0), (0, 0), (0, LANES - A_QK))).reshape(A_Q_LORA, A_HEADS * LANES).astype(BF16)
            wkv = mla_w_ukv[i].reshape(A_KV_LORA, A_HEADS, A_NOPE + A_V)
            wk = jnp.pad(wkv[:, :, :A_NOPE], ((0, 0), (0, 0), (0, LANES - A_NOPE))).reshape(A_KV_LORA, A_HEADS * LANES)
            wv = wkv[:, :, A_NOPE:].reshape(A_KV_LORA, A_HEADS * A_V)
            wukv = jnp.concatenate([wk, wv], axis=1).astype(BF16)
            bias, bias_t = _band_bias(band_rel_bias[i])
            w_band = B_REACH + BAND_TILE
            bias_pairs = bias_t.reshape(B_HEADS // 2, 2, w_band, BAND_TILE).transpose(0, 2, 1, 3).reshape(
                B_HEADS // 2, w_band, 2 * BAND_TILE)
            wo = even_w_out[i].astype(BF16)
        else:
            win = odd_w_in[i].astype(BF16)
            wo = odd_w_out[i].astype(BF16)
        for prompt in (True, False):
            x = _ffn(xs[prompt], g[0:2], wg, wu, wd, l, 0)
            st = new[prompt]
            if l % 2 == 0:
                qa, ka, va, qb, kb, vb, ckv, kpe, kbf, vbf = _even_proj(
                    x, g[2:3], win, mla_q_norm[i][None], mla_kv_norm[i][None], wuq, wukv, mla_tabs[prompt],
                    prompt, batch, seq)
                if prompt:
                    mixes = [_mla_prompt(qa, ka, va, batch, seq), _band_prompt(qb, kb, vb, bias_pairs, batch, seq)]
                    lead = (batch, seq)
                else:
                    mixes = _even_sample(i, qa, ka, va, qb, kb, vb, cache_mla_ckv, c_kpe_t, c_bk_t, c_bv_t,
                                         wukv, pick, bias, dec_batch, t_new)
                    lead = (dec_batch, t_new)
                st[0].append(ckv.reshape(*lead, A_KV_LORA))
                st[1].append(kpe.reshape(*lead, A_ROPE))
                st[2].append(kbf.reshape(lead[0], -1, B_HEADS, HEAD_DIM))
                st[3].append(vbf.reshape(lead[0], -1, B_HEADS, HEAD_DIM))
            else:
                q, k, v, kf, vf = _odd_proj(x, g[2:3], win, swa_tabs[prompt], prompt, batch, seq)
                if prompt:
                    mixes = [_swa_prompt(swa_sinks, i, q, k, v, batch, seq)]
                    lead = batch
                else:
                    mixes = [_odd_sample(swa_sinks, i, q, k, v, c_sk, c_sv, dec_batch, t_new)]
                    lead = dec_batch
                st[4].append(kf.reshape(lead, -1, C_KV_HEADS, HEAD_DIM))
                st[5].append(vf.reshape(lead, -1, C_KV_HEADS, HEAD_DIM))
            xs[prompt] = _mix_ffn(x, mixes, wo, g[3:4], g[4:6], wg, wu, wd, l, 1)

    y_prompt = xs[True].reshape(batch, seq, D_MODEL)
    y_sample = xs[False].reshape(dec_batch, t_new, D_MODEL)
    st_p = [jnp.stack(s, axis=0) for s in new[True]]
    st_s = [jnp.stack(s, axis=0) for s in new[False]]
    return (y_prompt, y_sample, *st_p, *st_s)
```

```python
import functools
import math

import jax
import jax.numpy as jnp
from jax import lax
from jax.experimental import pallas as pl
from jax.experimental.pallas import tpu as pltpu

F32 = jnp.float32
BF16 = jnp.bfloat16

D_MODEL = 1024
D_FF = 2816
CHUNK = 64
HEAD_DIM = 64
ROPE_THETA = 500000.0
NORM_EPS = 1e-6
NEG_INF = -1e30
A_HEADS = 8
A_Q_LORA = 256
A_KV_LORA = 128
A_NOPE = 64
A_ROPE = 32
A_QK = A_NOPE + A_ROPE
A_V = 64
A_SCALE = A_QK ** -0.5
B_HEADS = 8
B_REACH = 512
B_MAX_REL = 128
C_HEADS = 16
C_KV_HEADS = 2
C_WINDOW = 128
C_ROT = HEAD_DIM // 4
HEAD_SCALE = HEAD_DIM ** -0.5
LOG2E = math.log2(math.e)

LANES = 128
TOKEN_TILE = 512
FFN_TILE = 1024
FFN_PARTS = 2
FFN_ROWS = FFN_TILE // FFN_PARTS
FF_TILE = 256
PROJ_PARTS = 2
PROJ_ROWS = TOKEN_TILE // PROJ_PARTS
ATT_TILE = 128
BAND_TILE = 256
BAND_BLOCKS = 4
MLA_Q_TILE = 512
MLA_K_TILE = 256
BIAS_DIAG_LANES = 1024
SWA_BLOCKS = 8
ONES_ROWS = 16
VMEM_LIMIT = 56 * 1024 * 1024
B_WIDTH = B_HEADS * HEAD_DIM
C_QWIDTH = C_HEADS * HEAD_DIM
C_KWIDTH = C_KV_HEADS * HEAD_DIM
EVEN_CKV = A_Q_LORA
EVEN_KPE = EVEN_CKV + A_KV_LORA
EVEN_QB = EVEN_KPE + LANES
EVEN_KB = EVEN_QB + B_WIDTH
EVEN_VB = EVEN_KB + B_WIDTH


def _params(*sem):
    return pltpu.CompilerParams(dimension_semantics=sem, vmem_limit_bytes=VMEM_LIMIT)


def _const_spec(shape, index=None):
    index = index or (0,) * len(shape)
    return pl.BlockSpec(shape, lambda *_: index, pipeline_mode=pl.Buffered(1))


def _rms(x, g):
    ms = jnp.mean(x * x, axis=-1, keepdims=True)
    return x * lax.rsqrt(ms + NORM_EPS) * g


def _dot(a, b):
    return jnp.dot(a, b, preferred_element_type=F32)


def _dot_nt(a, b):
    return lax.dot_general(a, b, (((1,), (1,)), ((), ())), preferred_element_type=F32)


def _rope_group(x, c, s1, s2, half):
    return x * c + pltpu.roll(x, half, 1) * s1 + pltpu.roll(x, LANES - half, 1) * s2


def _lane_mask(lo):
    lane = lax.broadcasted_iota(jnp.int32, (1, LANES), 1)
    return (lane >= lo) & (lane < lo + HEAD_DIM)


def _swiglu_update(x, g_ref, wg_ref, wu_ref, wd_ref):
    xn = _rms(x, g_ref[0:1, :]).astype(BF16)
    acc = jnp.zeros(x.shape, F32)
    for c in range(D_FF // FF_TILE):
        sl = slice(c * FF_TILE, (c + 1) * FF_TILE)
        a = _dot(xn, wg_ref[:, sl])
        b = _dot(xn, wu_ref[:, sl])
        h = (a * jax.nn.sigmoid(a) * b).astype(BF16)
        acc = acc + _dot(h, wd_ref[sl, :])
    return x + 0.5 * _rms(acc, g_ref[1:2, :])


def _ffn_row_parts():
    return [slice(p * FFN_ROWS, (p + 1) * FFN_ROWS) for p in range(FFN_PARTS)]


def _ffn_kernel(x_ref, g_ref, wg_ref, wu_ref, wd_ref, o_ref):
    for rows in _ffn_row_parts():
        o_ref[rows, :] = _swiglu_update(x_ref[rows, :], g_ref, wg_ref, wu_ref, wd_ref)


def _mix_ffn_kernel(*refs, n_mix):
    x_ref = refs[0]
    mix_refs = refs[1:1 + n_mix]
    wo_ref, gm_ref, g_ref, wg_ref, wu_ref, wd_ref, o_ref = refs[1 + n_mix:]
    ys = []
    for rows in _ffn_row_parts():
        y = None
        row = 0
        for m_ref in mix_refs:
            w = m_ref.shape[1]
            part = _dot(m_ref[rows, :], wo_ref[row:row + w, :])
            y = part if y is None else y + part
            row += w
        ys.append(y)
    for rows, y in zip(_ffn_row_parts(), ys):
        x = x_ref[rows, :] + _rms(y, gm_ref[...])
        o_ref[rows, :] = _swiglu_update(x, g_ref, wg_ref, wu_ref, wd_ref)


def _ffn_weight_specs(layer, which):
    idx = (layer, which, 0, 0)
    return [_const_spec((None, None, D_MODEL, D_FF), idx), _const_spec((None, None, D_MODEL, D_FF), idx),
            _const_spec((None, None, D_FF, D_MODEL), idx)]


def _ffn(x, g, wg, wu, wd, layer, which):
    t = x.shape[0]
    tile = pl.BlockSpec((FFN_TILE, D_MODEL), lambda i: (i, 0))
    return pl.pallas_call(
        _ffn_kernel,
        out_shape=jax.ShapeDtypeStruct((t, D_MODEL), F32),
        grid=(t // FFN_TILE,),
        in_specs=[tile, _const_spec((2, D_MODEL))] + _ffn_weight_specs(layer, which),
        out_specs=tile,
        compiler_params=_params("parallel"),
        name="ffn",
    )(x, g, wg, wu, wd)


def _mix_ffn(x, mixes, wo, gm, g, wg, wu, wd, layer, which):
    t = x.shape[0]
    tile = pl.BlockSpec((FFN_TILE, D_MODEL), lambda i: (i, 0))
    mix_specs = [pl.BlockSpec((FFN_TILE, m.shape[1]), lambda i: (i, 0)) for m in mixes]
    return pl.pallas_call(
        functools.partial(_mix_ffn_kernel, n_mix=len(mixes)),
        out_shape=jax.ShapeDtypeStruct((t, D_MODEL), F32),
        grid=(t // FFN_TILE,),
        in_specs=[tile] + mix_specs + [_const_spec(wo.shape), _const_spec((1, D_MODEL)), _const_spec((2, D_MODEL))]
                 + _ffn_weight_specs(layer, which),
        out_specs=tile,
        compiler_params=_params("parallel"),
        name="mix_ffn",
    )(x, *mixes, wo, gm, g, wg, wu, wd)


def _even_proj_kernel(x_ref, g_ref, win_ref, qn_ref, kvn_ref, wuq_ref, wukv_ref,
                      c_ref, s1_ref, s2_ref,
                      qa_ref, ka_ref, va_ref, qb_ref, kb_ref, vb_ref,
                      ckv_ref, kpe_ref, kbf_ref, vbf_ref, *scratch, prompt, tiles_per_seq):
    half = A_ROPE // 2
    for part in range(PROJ_PARTS):
        rows = slice(part * PROJ_ROWS, (part + 1) * PROJ_ROWS)
        h = _rms(x_ref[rows, :], g_ref[...]).astype(BF16)
        proj = _dot(h, win_ref[...])
        cq = _rms(proj[:, 0:EVEN_CKV], qn_ref[...]).astype(BF16)
        ckv = _rms(proj[:, EVEN_CKV:EVEN_KPE], kvn_ref[...])
        c, s1, s2 = c_ref[rows, :], s1_ref[rows, :], s2_ref[rows, :]
        kpe = _rope_group(proj[:, EVEN_KPE:EVEN_QB], c, s1, s2, half)
        ckv_ref[rows, :] = ckv
        kpe_ref[rows, :] = pltpu.roll(kpe, LANES - A_NOPE, 1)[:, 0:A_ROPE]
        qa = _dot(cq, wuq_ref[...])
        kv = _dot(ckv.astype(BF16), wukv_ref[...])
        for hd in range(A_HEADS):
            sl = slice(hd * LANES, (hd + 1) * LANES)
            qa_ref[rows, sl] = (_rope_group(qa[:, sl], c, s1, s2, half) * (A_SCALE * LOG2E)).astype(BF16)
            ka_ref[rows, sl] = (kv[:, sl] + kpe).astype(BF16)
        for pair in range(B_HEADS // 2):
            grp = proj[:, EVEN_QB + pair * LANES:EVEN_QB + (pair + 1) * LANES] * (HEAD_SCALE * LOG2E)
            for sub in range(2):
                qb_ref[2 * pair + sub, rows, :] = jnp.where(_lane_mask(sub * HEAD_DIM), grp, 0.0).astype(BF16)
        va = kv[:, A_HEADS * LANES:]
        kb = proj[:, EVEN_KB:EVEN_VB]
        vb = proj[:, EVEN_VB:EVEN_VB + B_WIDTH]
        kb_ref[rows, :] = kb.astype(BF16)
        if prompt:
            va_ref[:, rows] = va.T.astype(BF16)
            vb_ref[:, rows] = vb.T.astype(BF16)
            scratch[0][rows, :] = kb
            scratch[1][rows, :] = vb
        else:
            va_ref[rows, :] = va.astype(BF16)
            vb_ref[rows, :] = vb.astype(BF16)
            kbf_ref[rows, :] = kb
            vbf_ref[rows, :] = vb
    if prompt:
        @pl.when(pl.program_id(0) % tiles_per_seq == tiles_per_seq - 1)
        def _():
            kbf_ref[...] = scratch[0][...]
            vbf_ref[...] = scratch[1][...]


def _last_tile_index(tiles_per_seq):
    return lambda i: jnp.maximum((i + 1) // tiles_per_seq - 1, 0)


def _skip_aliased(kernel, n_in, n_alias):
    def body(*refs):
        return kernel(*refs[:n_in], *refs[n_in + n_alias:])
    return body


def _layer_slot(n_layers, layer, rows, width, block_rows, row_index=lambda i: i):
    return (jax.ShapeDtypeStruct((n_layers, rows, width), F32),
            pl.BlockSpec((None, block_rows, width), lambda i: (layer, row_index(i), 0)))


def _alias_args(prev, n_in, first_out):
    if prev is None:
        return [], {}
    return [pl.BlockSpec(memory_space=pl.ANY)] * len(prev), {n_in + k: first_out + k for k in range(len(prev))}


def _even_proj(x, g, win, qn, kvn, wuq, wukv, tabs, prompt, batch, seq, layer, n_layers, prev):
    t = x.shape[0]
    tiles_per_seq = seq // TOKEN_TILE if prompt else 1

    def tile(w):
        return pl.BlockSpec((TOKEN_TILE, w), lambda i: (i, 0))

    def tile_t(w):
        return pl.BlockSpec((w, TOKEN_TILE), lambda i: (0, i))

    def sds(shape, dtype):
        return jax.ShapeDtypeStruct(shape, dtype)

    wa = A_HEADS * LANES
    if prompt:
        v_shapes = [sds((A_HEADS * A_V, t), BF16), sds((B_WIDTH, t), BF16)]
        v_specs = [tile_t(A_HEADS * A_V), tile_t(B_WIDTH)]
        keep = _layer_slot(n_layers, layer, batch * TOKEN_TILE, B_WIDTH, TOKEN_TILE, _last_tile_index(tiles_per_seq))
        tab_idx = lambda i: (i % tiles_per_seq, 0)
    else:
        v_shapes = [sds((t, A_HEADS * A_V), BF16), sds((t, B_WIDTH), BF16)]
        v_specs = [tile(A_HEADS * A_V), tile(B_WIDTH)]
        keep = _layer_slot(n_layers, layer, t, B_WIDTH, TOKEN_TILE)
        tab_idx = lambda i: (0, 0)
    ckv = _layer_slot(n_layers, layer, t, A_KV_LORA, TOKEN_TILE)
    kpe = _layer_slot(n_layers, layer, t, A_ROPE, TOKEN_TILE)
    qb_spec = pl.BlockSpec((B_HEADS, TOKEN_TILE, LANES), lambda i: (0, i, 0))
    out_shape = [sds((t, wa), BF16), sds((t, wa), BF16), v_shapes[0], sds((B_HEADS, t, LANES), BF16),
                 sds((t, B_WIDTH), BF16), v_shapes[1], ckv[0], kpe[0], keep[0], keep[0]]
    out_specs = [tile(wa), tile(wa), v_specs[0], qb_spec, tile(B_WIDTH), v_specs[1], ckv[1], kpe[1], keep[1], keep[1]]
    in_specs = ([tile(D_MODEL), _const_spec((1, D_MODEL)), _const_spec(win.shape),
                 _const_spec((1, A_Q_LORA)), _const_spec((1, A_KV_LORA)),
                 _const_spec(wuq.shape), _const_spec(wukv.shape)]
                + [pl.BlockSpec((TOKEN_TILE, LANES), tab_idx)] * 3)
    alias_specs, aliases = _alias_args(prev, len(in_specs), 6)
    kernel_fn = functools.partial(_even_proj_kernel, prompt=prompt, tiles_per_seq=tiles_per_seq)
    return pl.pallas_call(
        _skip_aliased(kernel_fn, len(in_specs), len(alias_specs)),
        out_shape=out_shape,
        grid=(t // TOKEN_TILE,),
        in_specs=in_specs + alias_specs,
        out_specs=out_specs,
        scratch_shapes=[pltpu.VMEM((TOKEN_TILE, B_WIDTH), F32)] * 2 if prompt else [],
        input_output_aliases=aliases,
        compiler_params=_params("arbitrary"),
        name="even_proj",
    )(x, g, win, qn, kvn, wuq, wukv, *tabs, *(prev or ()))


def _odd_proj_kernel(x_ref, g_ref, win_ref, c_ref, s1_ref, s2_ref,
                     q_ref, k_ref, v_ref, kf_ref, vf_ref, *scratch, prompt, tiles_per_seq):
    half = C_ROT // 2
    group = C_HEADS // C_KV_HEADS
    for part in range(PROJ_PARTS):
        rows = slice(part * PROJ_ROWS, (part + 1) * PROJ_ROWS)
        h = _rms(x_ref[rows, :], g_ref[...]).astype(BF16)
        proj = _dot(h, win_ref[...])
        c, s1, s2 = c_ref[rows, :], s1_ref[rows, :], s2_ref[rows, :]
        k = _rope_group(proj[:, C_QWIDTH:C_QWIDTH + LANES], c, s1, s2, half)
        v = proj[:, C_QWIDTH + LANES:C_QWIDTH + 2 * LANES]
        k_ref[rows, :] = k.astype(BF16)
        if prompt:
            for pair in range(C_HEADS // 2):
                grp = _rope_group(proj[:, pair * LANES:(pair + 1) * LANES], c, s1, s2, half) * (HEAD_SCALE * LOG2E)
                swapped = pltpu.roll(grp, HEAD_DIM, 1)
                for sub in range(2):
                    hd = 2 * pair + sub
                    kv = hd // group
                    src = grp if sub == kv else swapped
                    q_ref[hd, rows, :] = jnp.where(_lane_mask(kv * HEAD_DIM), src, 0.0).astype(BF16)
            v_ref[:, rows] = v.T.astype(BF16)
            if part == PROJ_PARTS - 1:
                scratch[0][...] = k[PROJ_ROWS - C_WINDOW:, :]
                scratch[1][...] = v[PROJ_ROWS - C_WINDOW:, :]
        else:
            for grp in range(C_QWIDTH // LANES):
                sl = slice(grp * LANES, (grp + 1) * LANES)
                q_ref[rows, sl] = (_rope_group(proj[:, sl], c, s1, s2, half) * (HEAD_SCALE * LOG2E)).astype(BF16)
            v_ref[rows, :] = v.astype(BF16)
            kf_ref[rows, :] = k
            vf_ref[rows, :] = v
    if prompt:
        @pl.when(pl.program_id(0) % tiles_per_seq == tiles_per_seq - 1)
        def _():
            kf_ref[...] = scratch[0][...]
            vf_ref[...] = scratch[1][...]


def _odd_proj(x, g, win, tabs, prompt, batch, seq, layer, n_layers, prev):
    t = x.shape[0]
    tiles_per_seq = seq // TOKEN_TILE if prompt else 1

    def tile(w):
        return pl.BlockSpec((TOKEN_TILE, w), lambda i: (i, 0))

    def sds(shape, dtype):
        return jax.ShapeDtypeStruct(shape, dtype)

    if prompt:
        q_shape = sds((C_HEADS, t, LANES), BF16)
        q_spec = pl.BlockSpec((C_HEADS, TOKEN_TILE, LANES), lambda i: (0, i, 0))
        v_shape, v_spec = sds((LANES, t), BF16), pl.BlockSpec((LANES, TOKEN_TILE), lambda i: (0, i))
        keep = _layer_slot(n_layers, layer, batch * C_WINDOW, LANES, C_WINDOW, _last_tile_index(tiles_per_seq))
        tab_idx = lambda i: (i % tiles_per_seq, 0)
    else:
        q_shape, q_spec = sds((t, C_QWIDTH), BF16), tile(C_QWIDTH)
        v_shape, v_spec = sds((t, LANES), BF16), tile(LANES)
        keep = _layer_slot(n_layers, layer, t, LANES, TOKEN_TILE)
        tab_idx = lambda i: (0, 0)
    in_specs = ([tile(D_MODEL), _const_spec((1, D_MODEL)), _const_spec(win.shape)]
                + [pl.BlockSpec((TOKEN_TILE, LANES), tab_idx)] * 3)
    alias_specs, aliases = _alias_args(prev, len(in_specs), 3)
    kernel_fn = functools.partial(_odd_proj_kernel, prompt=prompt, tiles_per_seq=tiles_per_seq)
    return pl.pallas_call(
        _skip_aliased(kernel_fn, len(in_specs), len(alias_specs)),
        out_shape=[q_shape, sds((t, LANES), BF16), v_shape, keep[0], keep[0]],
        grid=(t // TOKEN_TILE,),
        in_specs=in_specs + alias_specs,
        out_specs=[q_spec, tile(LANES), v_spec, keep[1], keep[1]],
        scratch_shapes=[pltpu.VMEM((C_WINDOW, LANES), F32)] * 2 if prompt else [],
        input_output_aliases=aliases,
        compiler_params=_params("arbitrary"),
        name="odd_proj",
    )(x, g, win, *tabs, *(prev or ()))


def _band_bias_kernel(tab_ref, o_ref, ot_ref):
    hd = pl.program_id(0)
    w = B_REACH + BAND_TILE
    row = lax.broadcasted_iota(jnp.int32, (BAND_TILE, w), 0)
    col = lax.broadcasted_iota(jnp.int32, (BAND_TILE, w), 1)
    lo = (row // CHUNK) * CHUNK
    visible = (col >= lo) & (col < lo + B_REACH + CHUNK)
    n = BIAS_DIAG_LANES
    e = lax.broadcasted_iota(jnp.int32, (1, n), 1)
    rel = jnp.clip(B_REACH + BAND_TILE - 1 - e, -B_MAX_REL, B_MAX_REL) + B_MAX_REL

    def body(r, acc):
        return jnp.where(rel == r, tab_ref[hd, r], acc)

    diag = lax.fori_loop(0, 2 * B_MAX_REL + 1, body, jnp.zeros((1, n), F32))
    shifted = pltpu.roll(jnp.broadcast_to(diag, (BAND_TILE, n)), n - (BAND_TILE - 1), 1, stride=1, stride_axis=0)
    bias = jnp.where(visible, shifted[:, 0:w] * LOG2E, NEG_INF)
    o_ref[...] = bias
    ot_ref[...] = bias.T


def _band_bias(table):
    w = B_REACH + BAND_TILE
    return pl.pallas_call(
        _band_bias_kernel,
        out_shape=[jax.ShapeDtypeStruct((B_HEADS, BAND_TILE, w), F32),
                   jax.ShapeDtypeStruct((B_HEADS, w, BAND_TILE), F32)],
        grid=(B_HEADS,),
        in_specs=[pl.BlockSpec(memory_space=pltpu.SMEM)],
        out_specs=[pl.BlockSpec((None, BAND_TILE, w), lambda h: (h, 0, 0)),
                   pl.BlockSpec((None, w, BAND_TILE), lambda h: (h, 0, 0))],
        compiler_params=_params("arbitrary"),
        name="band_bias",
    )(table)


def _ones_rows(n):
    return jnp.ones((ONES_ROWS, n), BF16)


def _mla_prompt_kernel(q_ref, k_ref, vt_ref, o_ref, m_ref, l_ref, acc_ref, s0_ref, s1_ref, smax0_ref, smax1_ref):
    i = pl.program_id(1)
    tq, tk = MLA_Q_TILE, MLA_K_TILE
    m_ref[...] = jnp.full(m_ref.shape, NEG_INF, F32)
    l_ref[...] = jnp.zeros(l_ref.shape, F32)
    acc_ref[...] = jnp.zeros(acc_ref.shape, F32)
    per_q = tq // tk
    assert per_q == 2
    s_refs = (s0_ref, s1_ref)
    smax_refs = (smax0_ref, smax1_ref)

    def live_queries(diag):
        return slice(0 if diag is None else diag * tk, tq)

    def score(j, slot, hd, diag=None):
        ks = pl.multiple_of(j * tk, tk)
        sl = slice(hd * LANES, (hd + 1) * LANES)
        cols = live_queries(diag)
        s = _dot_nt(k_ref[pl.ds(ks, tk), sl], q_ref[cols, sl])
        s_refs[slot][hd, :, cols] = s
        if diag is None:
            smax_refs[slot][hd] = jnp.max(s, axis=0, keepdims=True)

    def block(j, slot, diag, next_diag, stage_next=True):
        ks = pl.multiple_of(j * tk, tk)
        cols = live_queries(diag)
        m_all = [m_ref[hd, :, cols] for hd in range(A_HEADS)]
        l_all = [l_ref[hd, :, cols] for hd in range(A_HEADS)]
        acc_all = [acc_ref[hd, :, cols] for hd in range(A_HEADS)]
        ones = _ones_rows(tk)
        if diag is not None:
            shape = (tk, tq - cols.start)
            key_chunk = (diag * tk + lax.broadcasted_iota(jnp.int32, shape, 0)) // CHUNK
            qry_chunk = (cols.start + lax.broadcasted_iota(jnp.int32, shape, 1)) // CHUNK
            visible = key_chunk <= qry_chunk
        new = []
        for hd in range(A_HEADS):
            if stage_next:
                score(j + 1, 1 - slot, hd, next_diag)
            s = s_refs[slot][hd, :, cols]
            if diag is None:
                s_max = smax_refs[slot][hd]
            else:
                s = jnp.where(visible, s, NEG_INF)
                s_max = jnp.max(s, axis=0, keepdims=True)
            m_new = jnp.maximum(m_all[hd], s_max)
            alpha = jnp.exp2(m_all[hd] - m_new)
            p = jnp.exp2(s - m_new).astype(BF16)
            vt = jnp.concatenate([vt_ref[hd * A_V:(hd + 1) * A_V, pl.ds(ks, tk)], ones], axis=0)
            pv = _dot(vt, p)
            new.append((m_new, alpha * l_all[hd] + pv[A_V:A_V + 1], alpha * acc_all[hd] + pv[0:A_V]))
        for hd, (m_new, l_new, acc_new) in enumerate(new):
            m_ref[hd, :, cols] = m_new
            l_ref[hd, :, cols] = l_new
            acc_ref[hd, :, cols] = acc_new

    for hd in range(A_HEADS):
        score(0, 0, hd)

    def body(pair, carry):
        for d in range(per_q):
            block(pair * per_q + d, d, None, None)
        return carry

    lax.fori_loop(0, i, body, 0)
    for d in range(per_q):
        block(i * per_q + d, d, d, d + 1, stage_next=d + 1 < per_q)
    outs = [acc_ref[hd] / l_ref[hd] for hd in range(A_HEADS)]
    o_ref[...] = jnp.concatenate(outs, axis=0).T.astype(BF16)


def _mla_prompt(qa, ka, vat, batch, seq):
    nq = seq // MLA_Q_TILE
    wa = A_HEADS * LANES
    wv = A_HEADS * A_V
    return pl.pallas_call(
        _mla_prompt_kernel,
        out_shape=jax.ShapeDtypeStruct((batch * seq, wv), BF16),
        grid=(batch, nq),
        in_specs=[pl.BlockSpec((MLA_Q_TILE, wa), lambda b, i: (b * nq + i, 0)),
                  pl.BlockSpec((seq, wa), lambda b, i: (b, 0)),
                  pl.BlockSpec((wv, seq), lambda b, i: (0, b))],
        out_specs=pl.BlockSpec((MLA_Q_TILE, wv), lambda b, i: (b * nq + i, 0)),
        scratch_shapes=[pltpu.VMEM((A_HEADS, 1, MLA_Q_TILE), F32), pltpu.VMEM((A_HEADS, 1, MLA_Q_TILE), F32),
                        pltpu.VMEM((A_HEADS, A_V, MLA_Q_TILE), F32),
                        pltpu.VMEM((A_HEADS, MLA_K_TILE, MLA_Q_TILE), F32),
                        pltpu.VMEM((A_HEADS, MLA_K_TILE, MLA_Q_TILE), F32),
                        pltpu.VMEM((A_HEADS, 1, MLA_Q_TILE), F32), pltpu.VMEM((A_HEADS, 1, MLA_Q_TILE), F32)],
        compiler_params=_params("parallel", "arbitrary"),
        name="mla_prompt",
    )(qa, ka, vat)


def _fill_padded(kpad, vtpad, k_ref, vt_ref, reach):
    kpad[0:reach, :] = jnp.zeros((reach, kpad.shape[1]), kpad.dtype)
    kpad[reach:, :] = k_ref[...]
    vtpad[:, 0:reach] = jnp.zeros((vtpad.shape[0], reach), vtpad.dtype)
    vtpad[:, reach:] = vt_ref[...]


def _window_attend(s, vt, sink=None):
    m = jnp.max(s, axis=0, keepdims=True)
    if sink is not None:
        m = jnp.maximum(m, sink)
    p = jnp.exp2(s - m).astype(BF16)
    d = vt.shape[0]
    pv = _dot(jnp.concatenate([vt, _ones_rows(vt.shape[1])], axis=0), p)
    l = pv[d:d + 1]
    if sink is not None:
        l = l + jnp.exp2(sink - m)
    return pv[0:d] * (1.0 / l)


def _band_prompt_kernel(q_ref, k_ref, vt_ref, bias_ref, o_ref, kpad, vtpad, s0_ref, s1_ref):
    i = pl.program_id(1)
    tq = BAND_TILE
    s_refs = (s0_ref, s1_ref)

    @pl.when(i == 0)
    def _():
        _fill_padded(kpad, vtpad, k_ref, vt_ref, B_REACH)

    w = B_REACH + tq
    n_pairs = B_HEADS // 2
    step = BAND_BLOCKS * tq

    def attend(mask_start):
        key = lax.broadcasted_iota(jnp.int32, (w, 2 * tq), 0)

        def first_row(blk):
            return pl.multiple_of((i * BAND_BLOCKS + blk) * tq, tq)

        def score(unit):
            blk, pair = divmod(unit, n_pairs)
            q2 = q_ref[2 * pair:2 * pair + 2, blk * tq:(blk + 1) * tq, :].reshape(2 * tq, LANES)
            s_refs[unit % 2][...] = _dot_nt(kpad[pl.ds(first_row(blk), w), pair * LANES:(pair + 1) * LANES], q2)

        n_units = BAND_BLOCKS * n_pairs
        score(0)
        for unit in range(n_units):
            if unit + 1 < n_units:
                score(unit + 1)
            blk, pair = divmod(unit, n_pairs)
            if pair == 0:
                outs = []
            s = s_refs[unit % 2][...] + bias_ref[pair]
            if mask_start:
                s = jnp.where(key >= B_REACH - first_row(blk), s, NEG_INF)
            o2 = _window_attend(s, vtpad[pair * LANES:(pair + 1) * LANES, pl.ds(first_row(blk), w)])
            outs.append(o2[0:HEAD_DIM, 0:tq])
            outs.append(o2[HEAD_DIM:, tq:])
            if pair == n_pairs - 1:
                o_ref[blk * tq:(blk + 1) * tq, :] = jnp.concatenate(outs, axis=0).T.astype(BF16)

    pl.when(i * step < B_REACH)(functools.partial(attend, True))
    pl.when(i * step >= B_REACH)(functools.partial(attend, False))


def _band_prompt(qb, kb, vbt, bias_pairs, batch, seq):
    step = BAND_BLOCKS * BAND_TILE
    nq = seq // step
    return pl.pallas_call(
        _band_prompt_kernel,
        out_shape=jax.ShapeDtypeStruct((batch * seq, B_WIDTH), BF16),
        grid=(batch, nq),
        in_specs=[pl.BlockSpec((B_HEADS, step, LANES), lambda b, i: (0, b * nq + i, 0)),
                  pl.BlockSpec((seq, B_WIDTH), lambda b, i: (b, 0)),
                  pl.BlockSpec((B_WIDTH, seq), lambda b, i: (0, b)),
                  _const_spec(bias_pairs.shape)],
        out_specs=pl.BlockSpec((step, B_WIDTH), lambda b, i: (b * nq + i, 0)),
        scratch_shapes=[pltpu.VMEM((seq + B_REACH, B_WIDTH), BF16), pltpu.VMEM((B_WIDTH, seq + B_REACH), BF16)]
                       + [pltpu.VMEM((B_REACH + BAND_TILE, 2 * BAND_TILE), F32)] * 2,
        compiler_params=_params("parallel", "arbitrary"),
        name="band_prompt",
    )(qb, kb, vbt, bias_pairs)


def _swa_prompt_kernel(sink_ref, q_ref, k_ref, vt_ref, o_ref, kpad, vtpad, s0_ref, s1_ref, *, layer):
    i = pl.program_id(1)
    s_refs = (s0_ref, s1_ref)

    @pl.when(i == 0)
    def _():
        _fill_padded(kpad, vtpad, k_ref, vt_ref, C_WINDOW)

    w = C_WINDOW + ATT_TILE
    group = C_HEADS // C_KV_HEADS
    key = lax.broadcasted_iota(jnp.int32, (w, ATT_TILE), 0)
    qry = lax.broadcasted_iota(jnp.int32, (w, ATT_TILE), 1)
    lo = (qry // CHUNK) * CHUNK
    in_window = (key >= lo) & (key < lo + C_WINDOW + CHUNK)

    def score(unit):
        blk, kv = divmod(unit, C_KV_HEADS)
        start = pl.multiple_of((i * SWA_BLOCKS + blk) * ATT_TILE, ATT_TILE)
        q = q_ref[kv * group:(kv + 1) * group, blk * ATT_TILE:(blk + 1) * ATT_TILE, :]
        s_refs[unit % 2][...] = _dot_nt(kpad[pl.ds(start, w), :], q.reshape(group * ATT_TILE, LANES))

    n_units = SWA_BLOCKS * C_KV_HEADS
    score(0)
    for unit in range(n_units):
        if unit + 1 < n_units:
            score(unit + 1)
        blk, kv = divmod(unit, C_KV_HEADS)
        if kv == 0:
            first = (i * SWA_BLOCKS + blk) * ATT_TILE
            start = pl.multiple_of(first, ATT_TILE)
            visible = in_window & (key >= C_WINDOW - first)
            mask = jnp.concatenate([jnp.where(visible, 0.0, NEG_INF)] * group, axis=1)
            outs = []
        sink = jnp.concatenate([jnp.full((1, ATT_TILE), sink_ref[layer, kv * group + g] * LOG2E, F32)
                                for g in range(group)], axis=1)
        og = _window_attend(s_refs[unit % 2][...] + mask, vtpad[kv * HEAD_DIM:(kv + 1) * HEAD_DIM, pl.ds(start, w)],
                            sink)
        outs.extend(og[:, g * ATT_TILE:(g + 1) * ATT_TILE] for g in range(group))
        if kv == C_KV_HEADS - 1:
            o_ref[blk * ATT_TILE:(blk + 1) * ATT_TILE, :] = jnp.concatenate(outs, axis=0).T.astype(BF16)


def _swa_prompt(sinks, layer, q, k, vt, batch, seq):
    step = SWA_BLOCKS * ATT_TILE
    nq = seq // step
    return pl.pallas_call(
        functools.partial(_swa_prompt_kernel, layer=layer),
        out_shape=jax.ShapeDtypeStruct((batch * seq, C_QWIDTH), BF16),
        grid=(batch, nq),
        in_specs=[pl.BlockSpec(memory_space=pltpu.SMEM),
                  pl.BlockSpec((C_HEADS, step, LANES), lambda b, i: (0, b * nq + i, 0)),
                  pl.BlockSpec((seq, C_KWIDTH), lambda b, i: (b, 0)),
                  pl.BlockSpec((C_KWIDTH, seq), lambda b, i: (0, b))],
        out_specs=pl.BlockSpec((step, C_QWIDTH), lambda b, i: (b * nq + i, 0)),
        scratch_shapes=[pltpu.VMEM((seq + C_WINDOW, C_KWIDTH), BF16), pltpu.VMEM((C_KWIDTH, seq + C_WINDOW), BF16)]
                       + [pltpu.VMEM((C_WINDOW + ATT_TILE, C_HEADS // C_KV_HEADS * ATT_TILE), F32)] * 2,
        compiler_params=_params("parallel", "arbitrary"),
        name="swa_prompt",
    )(sinks, q, k, vt)


def _softmax_pv2(s_c, s_n, v_c, v_n, sink=None, v_c_transposed=False):
    m = jnp.maximum(jnp.max(s_c, axis=-1, keepdims=True), jnp.max(s_n, axis=-1, keepdims=True))
    if sink is not None:
        m = jnp.maximum(m, sink)
    p_c = jnp.exp2(s_c - m)
    p_n = jnp.exp2(s_n - m)
    l = jnp.sum(p_c, axis=-1, keepdims=True) + jnp.sum(p_n, axis=-1, keepdims=True)
    if sink is not None:
        l = l + jnp.exp2(sink - m)
    pv_c = _dot_nt(p_c.astype(BF16), v_c) if v_c_transposed else _dot(p_c.astype(BF16), v_c)
    return (pv_c + _dot(p_n.astype(BF16), v_n)) / l


def _even_sample_kernel(qa_ref, ka_ref, va_ref, qb_ref, kb_ref, vb_ref,
                        cckv_ref, ckpe_ref, cbk_ref, cbv_ref, wukv_ref, pick_ref, bias_ref,
                        oa_ref, ob_ref):
    t = qa_ref.shape[0]
    nb = cbk_ref.shape[1]
    kvc = _dot(cckv_ref[...].astype(BF16), wukv_ref[...])
    kpe_t = ckpe_ref[...].astype(BF16)
    scores_a, scores_b = [], []
    for hd in range(A_HEADS):
        sl = slice(hd * LANES, (hd + 1) * LANES)
        q = qa_ref[:, sl]
        q_rope = _dot(q, pick_ref[...]).astype(BF16)
        s_c = _dot_nt(q, kvc[:, sl].astype(BF16)) + _dot(q_rope, kpe_t)
        scores_a.append((s_c, _dot_nt(q, ka_ref[:, sl])))
    for hd in range(B_HEADS):
        pair = slice((hd // 2) * LANES, (hd // 2 + 1) * LANES)
        q = qb_ref[hd]
        scores_b.append((_dot(q, cbk_ref[pair, :].astype(BF16)) + bias_ref[hd, 0:t, 0:nb],
                         _dot_nt(q, kb_ref[:, pair]) + bias_ref[hd, 0:t, nb:nb + t]))
    outs_a, outs_b = [], []
    for hd in range(A_HEADS):
        vsl = slice(hd * A_V, (hd + 1) * A_V)
        v_c = kvc[:, A_HEADS * LANES + hd * A_V:A_HEADS * LANES + (hd + 1) * A_V].astype(BF16)
        outs_a.append(_softmax_pv2(*scores_a[hd], v_c, va_ref[:, vsl]))
    for hd in range(B_HEADS):
        sl = slice(hd * HEAD_DIM, (hd + 1) * HEAD_DIM)
        outs_b.append(_softmax_pv2(*scores_b[hd], cbv_ref[sl, :].astype(BF16), vb_ref[:, sl], v_c_transposed=True))
    oa_ref[...] = jnp.concatenate(outs_a, axis=-1).astype(BF16)
    ob_ref[...] = jnp.concatenate(outs_b, axis=-1).astype(BF16)


def _even_sample(layer, qa, ka, va, qb, kb, vb, c_ckv, c_kpe_t, c_bk_t, c_bv_t, wukv, pick, bias, dec_batch, t):
    past = c_ckv.shape[2]
    nb = c_bk_t.shape[3]

    def new(w):
        return pl.BlockSpec((t, w), lambda b: (b, 0))

    def cache(n, w):
        return pl.BlockSpec((None, None, n, w), lambda b: (layer, b, 0, 0))

    return pl.pallas_call(
        _even_sample_kernel,
        out_shape=[jax.ShapeDtypeStruct((dec_batch * t, B_WIDTH), BF16)] * 2,
        grid=(dec_batch,),
        in_specs=[new(A_HEADS * LANES), new(A_HEADS * LANES), new(A_HEADS * A_V),
                  pl.BlockSpec((B_HEADS, t, LANES), lambda b: (0, b, 0)), new(B_WIDTH), new(B_WIDTH),
                  cache(past, A_KV_LORA), cache(A_ROPE, past), cache(B_WIDTH, nb), cache(B_WIDTH, nb),
                  _const_spec(wukv.shape), _const_spec(pick.shape), _const_spec(bias.shape)],
        out_specs=[new(B_WIDTH), new(B_WIDTH)],
        compiler_params=_params("parallel"),
        name="even_sample",
    )(qa, ka, va, qb, kb, vb, c_ckv, c_kpe_t, c_bk_t, c_bv_t, wukv, pick, bias)


def _odd_sample_kernel(sink_ref, q_ref, k_ref, v_ref, ck_ref, cv_ref, o_ref, *, layer):
    group = C_HEADS // C_KV_HEADS
    scores = []
    for hd in range(C_HEADS):
        ksl = slice((hd // group) * HEAD_DIM, (hd // group + 1) * HEAD_DIM)
        q = q_ref[:, hd * HEAD_DIM:(hd + 1) * HEAD_DIM]
        scores.append((_dot_nt(q, ck_ref[:, ksl].astype(BF16)), _dot_nt(q, k_ref[:, ksl])))
    outs = []
    for hd in range(C_HEADS):
        ksl = slice((hd // group) * HEAD_DIM, (hd // group + 1) * HEAD_DIM)
        outs.append(_softmax_pv2(*scores[hd], cv_ref[:, ksl].astype(BF16), v_ref[:, ksl],
                                 sink_ref[layer, hd] * LOG2E))
    o_ref[...] = jnp.concatenate(outs, axis=-1).astype(BF16)


def _odd_sample(sinks, layer, q, k, v, c_k, c_v, dec_batch, t):
    nc = c_k.shape[2]

    def new(w):
        return pl.BlockSpec((t, w), lambda b: (b, 0))

    def cache():
        return pl.BlockSpec((None, None, nc, C_KWIDTH), lambda b: (layer, b, 0, 0))

    return pl.pallas_call(
        functools.partial(_odd_sample_kernel, layer=layer),
        out_shape=jax.ShapeDtypeStruct((dec_batch * t, C_QWIDTH), BF16),
        grid=(dec_batch,),
        in_specs=[pl.BlockSpec(memory_space=pltpu.SMEM), new(C_QWIDTH), new(C_KWIDTH), new(C_KWIDTH),
                  cache(), cache()],
        out_specs=new(C_QWIDTH),
        compiler_params=_params("parallel"),
        name="odd_sample",
    )(sinks, q, k, v, c_k, c_v)


def _rope_parts(pos, n_rot):
    half = n_rot // 2
    inv = ROPE_THETA ** (-jnp.arange(half, dtype=F32) / half)
    ang = pos.astype(F32)[:, None] * inv[None, :]
    return jnp.cos(ang), jnp.sin(ang)


def _rope_tables(pos, n_rot, pre, width):
    cos, sin = _rope_parts(pos, n_rot)
    n = pos.shape[0]
    post = width - pre - n_rot

    def head(first, second, fill):
        return jnp.concatenate([jnp.full((n, pre), fill, F32), first, second, jnp.full((n, post), fill, F32)], axis=1)

    zero = jnp.zeros_like(sin)
    tabs = (head(cos, cos, 1.0), head(zero, sin, 0.0), head(-sin, zero, 0.0))
    return tuple(jnp.tile(t, (1, LANES // width)) for t in tabs)


def kernel(x_prompt, x_sample, cache_mla_ckv, cache_mla_kpe, cache_band_k, cache_band_v,
           cache_swa_k, cache_swa_v, norm_g, ffn_w_gate, ffn_w_up, ffn_w_down, even_w_in,
           mla_q_norm, mla_w_uq, mla_kv_norm, mla_w_ukv, band_rel_bias, even_w_out,
           odd_w_in, swa_sinks, odd_w_out):
    batch, seq, _ = x_prompt.shape
    dec_batch, t_new, _ = x_sample.shape
    depth = norm_g.shape[0]
    n_even = even_w_in.shape[0]
    n_odd = odd_w_in.shape[0]
    past = cache_mla_ckv.shape[2]
    n_p = batch * seq
    n_s = dec_batch * t_new
    assert seq % TOKEN_TILE == 0 and n_s % TOKEN_TILE == 0 and TOKEN_TILE % t_new == 0
    assert n_p % FFN_TILE == 0 and n_s % FFN_TILE == 0
    assert seq % MLA_Q_TILE == 0 and seq % (BAND_BLOCKS * BAND_TILE) == 0 and seq % (SWA_BLOCKS * ATT_TILE) == 0 and min(B_REACH, seq) == TOKEN_TILE and t_new <= CHUNK
    assert cache_band_k.shape[2] == B_REACH and cache_swa_k.shape[2] == C_WINDOW and past >= B_REACH

    pos_p = jnp.arange(seq, dtype=jnp.int32)
    pos_s = past + (jnp.arange(TOKEN_TILE, dtype=jnp.int32) % t_new)
    mla_tabs = {True: _rope_tables(pos_p, A_ROPE, A_NOPE, LANES), False: _rope_tables(pos_s, A_ROPE, A_NOPE, LANES)}
    swa_tabs = {True: _rope_tables(pos_p, C_ROT, 0, HEAD_DIM), False: _rope_tables(pos_s, C_ROT, 0, HEAD_DIM)}

    wg = ffn_w_gate.astype(BF16)
    wu = ffn_w_up.astype(BF16)
    wd = ffn_w_down.astype(BF16)

    pick = jnp.zeros((LANES, A_ROPE), F32).at[A_NOPE + jnp.arange(A_ROPE), jnp.arange(A_ROPE)].set(1.0).astype(BF16)
    c_kpe_t = cache_mla_kpe.transpose(0, 1, 3, 2)
    c_bk_t = cache_band_k.transpose(0, 1, 3, 4, 2).reshape(n_even, dec_batch, B_WIDTH, B_REACH)
    c_bv_t = cache_band_v.transpose(0, 1, 3, 4, 2).reshape(n_even, dec_batch, B_WIDTH, B_REACH)
    c_sk = cache_swa_k.reshape(n_odd, dec_batch, C_WINDOW, C_KWIDTH)
    c_sv = cache_swa_v.reshape(n_odd, dec_batch, C_WINDOW, C_KWIDTH)

    c0 = A_Q_LORA + A_KV_LORA
    c1 = c0 + A_ROPE
    xs = {True: x_prompt.reshape(n_p, D_MODEL), False: x_sample.reshape(n_s, D_MODEL)}
    even_st = {True: None, False: None}
    odd_st = {True: None, False: None}
    for l in range(depth):
        g = norm_g[l]
        i = l // 2
        if l % 2 == 0:
            w = even_w_in[i].astype(BF16)
            zc = lambda n: jnp.zeros((D_MODEL, n), BF16)
            win = jnp.concatenate([w[:, :c0], zc(A_NOPE), w[:, c0:c1], zc(LANES - A_NOPE - A_ROPE), w[:, c1:]], axis=1)
            wuq = jnp.pad(mla_w_uq[i].reshape(A_Q_LORA, A_HEADS, A_QK),
                          ((0, 0), (0, 0), (0, LANES - A_QK))).reshape(A_Q_LORA, A_HEADS * LANES).astype(BF16)
            wkv = mla_w_ukv[i].reshape(A_KV_LORA, A_HEADS, A_NOPE + A_V)
            wk = jnp.pad(wkv[:, :, :A_NOPE], ((0, 0), (0, 0), (0, LANES - A_NOPE))).reshape(A_KV_LORA, A_HEADS * LANES)
            wv = wkv[:, :, A_NOPE:].reshape(A_KV_LORA, A_HEADS * A_V)
            wukv = jnp.concatenate([wk, wv], axis=1).astype(BF16)
            bias, bias_t = _band_bias(band_rel_bias[i])
            w_band = B_REACH + BAND_TILE
            bias_pairs = bias_t.reshape(B_HEADS // 2, 2, w_band, BAND_TILE).transpose(0, 2, 1, 3).reshape(
                B_HEADS // 2, w_band, 2 * BAND_TILE)
            wo = even_w_out[i].astype(BF16)
        else:
            win = odd_w_in[i].astype(BF16)
            wo = odd_w_out[i].astype(BF16)
        for prompt in (True, False):
            x = _ffn(xs[prompt], g[0:2], wg, wu, wd, l, 0)
            if l % 2 == 0:
                qa, ka, va, qb, kb, vb, *even_st[prompt] = _even_proj(
                    x, g[2:3], win, mla_q_norm[i][None], mla_kv_norm[i][None], wuq, wukv, mla_tabs[prompt],
                    prompt, batch, seq, i, n_even, even_st[prompt])
                if prompt:
                    mixes = [_mla_prompt(qa, ka, va, batch, seq), _band_prompt(qb, kb, vb, bias_pairs, batch, seq)]
                else:
                    mixes = _even_sample(i, qa, ka, va, qb, kb, vb, cache_mla_ckv, c_kpe_t, c_bk_t, c_bv_t,
                                         wukv, pick, bias, dec_batch, t_new)
            else:
                q, k, v, *odd_st[prompt] = _odd_proj(x, g[2:3], win, swa_tabs[prompt], prompt, batch, seq,
                                                     i, n_odd, odd_st[prompt])
                if prompt:
                    mixes = [_swa_prompt(swa_sinks, i, q, k, v, batch, seq)]
                else:
                    mixes = [_odd_sample(swa_sinks, i, q, k, v, c_sk, c_sv, dec_batch, t_new)]
            xs[prompt] = _mix_ffn(x, mixes, wo, g[3:4], g[4:6], wg, wu, wd, l, 1)

    y_prompt = xs[True].reshape(batch, seq, D_MODEL)
    y_sample = xs[False].reshape(dec_batch, t_new, D_MODEL)

    def states(prompt, lead):
        ckv, kpe, kbf, vbf = even_st[prompt]
        kf, vf = odd_st[prompt]
        return (ckv.reshape(n_even, *lead, A_KV_LORA), kpe.reshape(n_even, *lead, A_ROPE),
                kbf.reshape(n_even, lead[0], -1, B_HEADS, HEAD_DIM), vbf.reshape(n_even, lead[0], -1, B_HEADS, HEAD_DIM),
                kf.reshape(n_odd, lead[0], -1, C_KV_HEADS, HEAD_DIM), vf.reshape(n_odd, lead[0], -1, C_KV_HEADS, HEAD_DIM))

    return (y_prompt, y_sample, *states(True, (batch, seq)), *states(False, (dec_batch, t_new)))
```

```python
import functools
import math

import jax
import jax.numpy as jnp
from jax import lax
from jax.experimental import pallas as pl
from jax.experimental.pallas import tpu as pltpu

F32 = jnp.float32
BF16 = jnp.bfloat16

D_MODEL = 1024
D_FF = 2816
CHUNK = 64
HEAD_DIM = 64
ROPE_THETA = 500000.0
NORM_EPS = 1e-6
NEG_INF = -1e30
A_HEADS = 8
A_Q_LORA = 256
A_KV_LORA = 128
A_NOPE = 64
A_ROPE = 32
A_QK = A_NOPE + A_ROPE
A_V = 64
A_SCALE = A_QK ** -0.5
B_HEADS = 8
B_REACH = 512
B_MAX_REL = 128
C_HEADS = 16
C_KV_HEADS = 2
C_WINDOW = 128
C_ROT = HEAD_DIM // 4
HEAD_SCALE = HEAD_DIM ** -0.5
LOG2E = math.log2(math.e)

LANES = 128
TOKEN_TILE = 512
FFN_TILE = 1024
FFN_PARTS = 2
FFN_ROWS = FFN_TILE // FFN_PARTS
FF_TILE = 256
PROJ_PARTS = 2
PROJ_ROWS = TOKEN_TILE // PROJ_PARTS
ATT_TILE = 128
BAND_TILE = 256
BAND_BLOCKS = 4
MLA_Q_TILE = 512
MLA_K_TILE = 256
BIAS_DIAG_LANES = 1024
SWA_BLOCKS = 8
ONES_ROWS = 16
VMEM_LIMIT = 56 * 1024 * 1024
B_WIDTH = B_HEADS * HEAD_DIM
C_QWIDTH = C_HEADS * HEAD_DIM
C_KWIDTH = C_KV_HEADS * HEAD_DIM
EVEN_CKV = A_Q_LORA
EVEN_KPE = EVEN_CKV + A_KV_LORA
EVEN_QB = EVEN_KPE + LANES
EVEN_KB = EVEN_QB + B_WIDTH
EVEN_VB = EVEN_KB + B_WIDTH


def _params(*sem):
    return pltpu.CompilerParams(dimension_semantics=sem, vmem_limit_bytes=VMEM_LIMIT)


def _const_spec(shape, index=None):
    index = index or (0,) * len(shape)
    return pl.BlockSpec(shape, lambda *_: index, pipeline_mode=pl.Buffered(1))


def _rms(x, g):
    ms = jnp.mean(x * x, axis=-1, keepdims=True)
    return x * lax.rsqrt(ms + NORM_EPS) * g


def _dot(a, b):
    return jnp.dot(a, b, preferred_element_type=F32)


def _dot_nt(a, b):
    return lax.dot_general(a, b, (((1,), (1,)), ((), ())), preferred_element_type=F32)


def _rope_group(x, c, s1, s2, half):
    return x * c + pltpu.roll(x, half, 1) * s1 + pltpu.roll(x, LANES - half, 1) * s2


def _lane_mask(lo):
    lane = lax.broadcasted_iota(jnp.int32, (1, LANES), 1)
    return (lane >= lo) & (lane < lo + HEAD_DIM)


def _swiglu_update(x, g_ref, wg_ref, wu_ref, wd_ref):
    xn = _rms(x, g_ref[0:1, :]).astype(BF16)
    acc = jnp.zeros(x.shape, F32)
    for c in range(D_FF // FF_TILE):
        sl = slice(c * FF_TILE, (c + 1) * FF_TILE)
        a = _dot(xn, wg_ref[:, sl])
        b = _dot(xn, wu_ref[:, sl])
        h = (a * jax.nn.sigmoid(a) * b).astype(BF16)
        acc = acc + _dot(h, wd_ref[sl, :])
    return x + 0.5 * _rms(acc, g_ref[1:2, :])


def _ffn_row_parts():
    return [slice(p * FFN_ROWS, (p + 1) * FFN_ROWS) for p in range(FFN_PARTS)]


def _ffn_kernel(x_ref, g_ref, wg_ref, wu_ref, wd_ref, o_ref):
    for rows in _ffn_row_parts():
        o_ref[rows, :] = _swiglu_update(x_ref[rows, :], g_ref, wg_ref, wu_ref, wd_ref)


def _mix_ffn_kernel(*refs, n_mix):
    x_ref = refs[0]
    mix_refs = refs[1:1 + n_mix]
    wo_ref, gm_ref, g_ref, wg_ref, wu_ref, wd_ref, o_ref = refs[1 + n_mix:]
    ys = []
    for rows in _ffn_row_parts():
        y = None
        row = 0
        for m_ref in mix_refs:
            w = m_ref.shape[1]
            part = _dot(m_ref[rows, :], wo_ref[row:row + w, :])
            y = part if y is None else y + part
            row += w
        ys.append(y)
    for rows, y in zip(_ffn_row_parts(), ys):
        x = x_ref[rows, :] + _rms(y, gm_ref[...])
        o_ref[rows, :] = _swiglu_update(x, g_ref, wg_ref, wu_ref, wd_ref)


def _ffn_weight_specs(layer, which):
    idx = (layer, which, 0, 0)
    return [_const_spec((None, None, D_MODEL, D_FF), idx), _const_spec((None, None, D_MODEL, D_FF), idx),
            _const_spec((None, None, D_FF, D_MODEL), idx)]


def _ffn(x, g, wg, wu, wd, layer, which):
    t = x.shape[0]
    tile = pl.BlockSpec((FFN_TILE, D_MODEL), lambda i: (i, 0))
    return pl.pallas_call(
        _ffn_kernel,
        out_shape=jax.ShapeDtypeStruct((t, D_MODEL), F32),
        grid=(t // FFN_TILE,),
        in_specs=[tile, _const_spec((2, D_MODEL))] + _ffn_weight_specs(layer, which),
        out_specs=tile,
        compiler_params=_params("parallel"),
        name="ffn",
    )(x, g, wg, wu, wd)


def _mix_ffn(x, mixes, wo, gm, g, wg, wu, wd, layer, which):
    t = x.shape[0]
    tile = pl.BlockSpec((FFN_TILE, D_MODEL), lambda i: (i, 0))
    mix_specs = [pl.BlockSpec((FFN_TILE, m.shape[1]), lambda i: (i, 0)) for m in mixes]
    return pl.pallas_call(
        functools.partial(_mix_ffn_kernel, n_mix=len(mixes)),
        out_shape=jax.ShapeDtypeStruct((t, D_MODEL), F32),
        grid=(t // FFN_TILE,),
        in_specs=[tile] + mix_specs + [_const_spec(wo.shape), _const_spec((1, D_MODEL)), _const_spec((2, D_MODEL))]
                 + _ffn_weight_specs(layer, which),
        out_specs=tile,
        compiler_params=_params("parallel"),
        name="mix_ffn",
    )(x, *mixes, wo, gm, g, wg, wu, wd)


def _even_proj_kernel(x_ref, g_ref, win_ref, qn_ref, kvn_ref, wuq_ref, wukv_ref,
                      c_ref, s1_ref, s2_ref,
                      qa_ref, ka_ref, va_ref, qb_ref, kb_ref, vb_ref,
                      ckv_ref, kpe_ref, kbf_ref, vbf_ref, *scratch, prompt, tiles_per_seq):
    half = A_ROPE // 2
    for part in range(PROJ_PARTS):
        rows = slice(part * PROJ_ROWS, (part + 1) * PROJ_ROWS)
        h = _rms(x_ref[rows, :], g_ref[...]).astype(BF16)
        proj = _dot(h, win_ref[...])
        cq = _rms(proj[:, 0:EVEN_CKV], qn_ref[...]).astype(BF16)
        ckv = _rms(proj[:, EVEN_CKV:EVEN_KPE], kvn_ref[...])
        c, s1, s2 = c_ref[rows, :], s1_ref[rows, :], s2_ref[rows, :]
        kpe = _rope_group(proj[:, EVEN_KPE:EVEN_QB], c, s1, s2, half)
        ckv_ref[rows, :] = ckv
        kpe_ref[rows, :] = pltpu.roll(kpe, LANES - A_NOPE, 1)[:, 0:A_ROPE]
        qa = _dot(cq, wuq_ref[...])
        kv = _dot(ckv.astype(BF16), wukv_ref[...])
        for hd in range(A_HEADS):
            sl = slice(hd * LANES, (hd + 1) * LANES)
            qa_ref[rows, sl] = (_rope_group(qa[:, sl], c, s1, s2, half) * (A_SCALE * LOG2E)).astype(BF16)
            ka_ref[rows, sl] = (kv[:, sl] + kpe).astype(BF16)
        for pair in range(B_HEADS // 2):
            grp = proj[:, EVEN_QB + pair * LANES:EVEN_QB + (pair + 1) * LANES] * (HEAD_SCALE * LOG2E)
            for sub in range(2):
                qb_ref[2 * pair + sub, rows, :] = jnp.where(_lane_mask(sub * HEAD_DIM), grp, 0.0).astype(BF16)
        va = kv[:, A_HEADS * LANES:]
        kb = proj[:, EVEN_KB:EVEN_VB]
        vb = proj[:, EVEN_VB:EVEN_VB + B_WIDTH]
        kb_ref[rows, :] = kb.astype(BF16)
        if prompt:
            va_ref[:, rows] = va.T.astype(BF16)
            vb_ref[:, rows] = vb.T.astype(BF16)
            scratch[0][rows, :] = kb
            scratch[1][rows, :] = vb
        else:
            va_ref[rows, :] = va.astype(BF16)
            vb_ref[rows, :] = vb.astype(BF16)
            kbf_ref[rows, :] = kb
            vbf_ref[rows, :] = vb
    if prompt:
        @pl.when(pl.program_id(0) % tiles_per_seq == tiles_per_seq - 1)
        def _():
            kbf_ref[...] = scratch[0][...].T
            vbf_ref[...] = scratch[1][...].T


def _last_tile_index(tiles_per_seq):
    return lambda i: jnp.maximum((i + 1) // tiles_per_seq - 1, 0)


def _skip_aliased(kernel, n_in, n_alias):
    def body(*refs):
        return kernel(*refs[:n_in], *refs[n_in + n_alias:])
    return body


def _layer_slot(n_layers, layer, rows, width, block_rows, row_index=lambda i: i, transposed=False):
    if transposed:
        return (jax.ShapeDtypeStruct((n_layers, width, rows), F32),
                pl.BlockSpec((None, width, block_rows), lambda i: (layer, 0, row_index(i))))
    return (jax.ShapeDtypeStruct((n_layers, rows, width), F32),
            pl.BlockSpec((None, block_rows, width), lambda i: (layer, row_index(i), 0)))


def _alias_args(prev, n_in, first_out):
    if prev is None:
        return [], {}
    return [pl.BlockSpec(memory_space=pl.ANY)] * len(prev), {n_in + k: first_out + k for k in range(len(prev))}


def _even_proj(x, g, win, qn, kvn, wuq, wukv, tabs, prompt, batch, seq, layer, n_layers, prev):
    t = x.shape[0]
    tiles_per_seq = seq // TOKEN_TILE if prompt else 1

    def tile(w):
        return pl.BlockSpec((TOKEN_TILE, w), lambda i: (i, 0))

    def tile_t(w):
        return pl.BlockSpec((w, TOKEN_TILE), lambda i: (0, i))

    def sds(shape, dtype):
        return jax.ShapeDtypeStruct(shape, dtype)

    wa = A_HEADS * LANES
    if prompt:
        v_shapes = [sds((A_HEADS * A_V, t), BF16), sds((B_WIDTH, t), BF16)]
        v_specs = [tile_t(A_HEADS * A_V), tile_t(B_WIDTH)]
        keep = _layer_slot(n_layers, layer, batch * TOKEN_TILE, B_WIDTH, TOKEN_TILE, _last_tile_index(tiles_per_seq),
                           transposed=True)
        tab_idx = lambda i: (i % tiles_per_seq, 0)
    else:
        v_shapes = [sds((t, A_HEADS * A_V), BF16), sds((t, B_WIDTH), BF16)]
        v_specs = [tile(A_HEADS * A_V), tile(B_WIDTH)]
        keep = _layer_slot(n_layers, layer, t, B_WIDTH, TOKEN_TILE)
        tab_idx = lambda i: (0, 0)
    ckv = _layer_slot(n_layers, layer, t, A_KV_LORA, TOKEN_TILE)
    kpe = _layer_slot(n_layers, layer, t, A_ROPE, TOKEN_TILE)
    qb_spec = pl.BlockSpec((B_HEADS, TOKEN_TILE, LANES), lambda i: (0, i, 0))
    out_shape = [sds((t, wa), BF16), sds((t, wa), BF16), v_shapes[0], sds((B_HEADS, t, LANES), BF16),
                 sds((t, B_WIDTH), BF16), v_shapes[1], ckv[0], kpe[0], keep[0], keep[0]]
    out_specs = [tile(wa), tile(wa), v_specs[0], qb_spec, tile(B_WIDTH), v_specs[1], ckv[1], kpe[1], keep[1], keep[1]]
    in_specs = ([tile(D_MODEL), _const_spec((1, D_MODEL)), _const_spec(win.shape),
                 _const_spec((1, A_Q_LORA)), _const_spec((1, A_KV_LORA)),
                 _const_spec(wuq.shape), _const_spec(wukv.shape)]
                + [pl.BlockSpec((TOKEN_TILE, LANES), tab_idx)] * 3)
    alias_specs, aliases = _alias_args(prev, len(in_specs), 6)
    kernel_fn = functools.partial(_even_proj_kernel, prompt=prompt, tiles_per_seq=tiles_per_seq)
    return pl.pallas_call(
        _skip_aliased(kernel_fn, len(in_specs), len(alias_specs)),
        out_shape=out_shape,
        grid=(t // TOKEN_TILE,),
        in_specs=in_specs + alias_specs,
        out_specs=out_specs,
        scratch_shapes=[pltpu.VMEM((TOKEN_TILE, B_WIDTH), F32)] * 2 if prompt else [],
        input_output_aliases=aliases,
        compiler_params=_params("arbitrary"),
        name="even_proj",
    )(x, g, win, qn, kvn, wuq, wukv, *tabs, *(prev or ()))


def _odd_proj_kernel(x_ref, g_ref, win_ref, c_ref, s1_ref, s2_ref,
                     q_ref, k_ref, v_ref, kf_ref, vf_ref, *scratch, prompt, tiles_per_seq):
    half = C_ROT // 2
    group = C_HEADS // C_KV_HEADS
    for part in range(PROJ_PARTS):
        rows = slice(part * PROJ_ROWS, (part + 1) * PROJ_ROWS)
        h = _rms(x_ref[rows, :], g_ref[...]).astype(BF16)
        proj = _dot(h, win_ref[...])
        c, s1, s2 = c_ref[rows, :], s1_ref[rows, :], s2_ref[rows, :]
        k = _rope_group(proj[:, C_QWIDTH:C_QWIDTH + LANES], c, s1, s2, half)
        v = proj[:, C_QWIDTH + LANES:C_QWIDTH + 2 * LANES]
        k_ref[rows, :] = k.astype(BF16)
        if prompt:
            for pair in range(C_HEADS // 2):
                grp = _rope_group(proj[:, pair * LANES:(pair + 1) * LANES], c, s1, s2, half) * (HEAD_SCALE * LOG2E)
                swapped = pltpu.roll(grp, HEAD_DIM, 1)
                for sub in range(2):
                    hd = 2 * pair + sub
                    kv = hd // group
                    src = grp if sub == kv else swapped
                    q_ref[hd, rows, :] = jnp.where(_lane_mask(kv * HEAD_DIM), src, 0.0).astype(BF16)
            v_ref[:, rows] = v.T.astype(BF16)
            if part == PROJ_PARTS - 1:
                scratch[0][...] = k[PROJ_ROWS - C_WINDOW:, :]
                scratch[1][...] = v[PROJ_ROWS - C_WINDOW:, :]
        else:
            for grp in range(C_QWIDTH // LANES):
                sl = slice(grp * LANES, (grp + 1) * LANES)
                q_ref[rows, sl] = (_rope_group(proj[:, sl], c, s1, s2, half) * (HEAD_SCALE * LOG2E)).astype(BF16)
            v_ref[rows, :] = v.astype(BF16)
            kf_ref[rows, :] = k
            vf_ref[rows, :] = v
    if prompt:
        @pl.when(pl.program_id(0) % tiles_per_seq == tiles_per_seq - 1)
        def _():
            kf_ref[...] = scratch[0][...]
            vf_ref[...] = scratch[1][...]


def _odd_proj(x, g, win, tabs, prompt, batch, seq, layer, n_layers, prev):
    t = x.shape[0]
    tiles_per_seq = seq // TOKEN_TILE if prompt else 1

    def tile(w):
        return pl.BlockSpec((TOKEN_TILE, w), lambda i: (i, 0))

    def sds(shape, dtype):
        return jax.ShapeDtypeStruct(shape, dtype)

    if prompt:
        q_shape = sds((C_HEADS, t, LANES), BF16)
        q_spec = pl.BlockSpec((C_HEADS, TOKEN_TILE, LANES), lambda i: (0, i, 0))
        v_shape, v_spec = sds((LANES, t), BF16), pl.BlockSpec((LANES, TOKEN_TILE), lambda i: (0, i))
        keep = _layer_slot(n_layers, layer, batch * C_WINDOW, LANES, C_WINDOW, _last_tile_index(tiles_per_seq))
        tab_idx = lambda i: (i % tiles_per_seq, 0)
    else:
        q_shape, q_spec = sds((t, C_QWIDTH), BF16), tile(C_QWIDTH)
        v_shape, v_spec = sds((t, LANES), BF16), tile(LANES)
        keep = _layer_slot(n_layers, layer, t, LANES, TOKEN_TILE)
        tab_idx = lambda i: (0, 0)
    in_specs = ([tile(D_MODEL), _const_spec((1, D_MODEL)), _const_spec(win.shape)]
                + [pl.BlockSpec((TOKEN_TILE, LANES), tab_idx)] * 3)
    alias_specs, aliases = _alias_args(prev, len(in_specs), 3)
    kernel_fn = functools.partial(_odd_proj_kernel, prompt=prompt, tiles_per_seq=tiles_per_seq)
    return pl.pallas_call(
        _skip_aliased(kernel_fn, len(in_specs), len(alias_specs)),
        out_shape=[q_shape, sds((t, LANES), BF16), v_shape, keep[0], keep[0]],
        grid=(t // TOKEN_TILE,),
        in_specs=in_specs + alias_specs,
        out_specs=[q_spec, tile(LANES), v_spec, keep[1], keep[1]],
        scratch_shapes=[pltpu.VMEM((C_WINDOW, LANES), F32)] * 2 if prompt else [],
        input_output_aliases=aliases,
        compiler_params=_params("arbitrary"),
        name="odd_proj",
    )(x, g, win, *tabs, *(prev or ()))


def _band_bias_kernel(tab_ref, o_ref, ot_ref):
    hd = pl.program_id(0)
    w = B_REACH + BAND_TILE
    row = lax.broadcasted_iota(jnp.int32, (BAND_TILE, w), 0)
    col = lax.broadcasted_iota(jnp.int32, (BAND_TILE, w), 1)
    lo = (row // CHUNK) * CHUNK
    visible = (col >= lo) & (col < lo + B_REACH + CHUNK)
    n = BIAS_DIAG_LANES
    e = lax.broadcasted_iota(jnp.int32, (1, n), 1)
    rel = jnp.clip(B_REACH + BAND_TILE - 1 - e, -B_MAX_REL, B_MAX_REL) + B_MAX_REL

    def body(r, acc):
        return jnp.where(rel == r, tab_ref[hd, r], acc)

    diag = lax.fori_loop(0, 2 * B_MAX_REL + 1, body, jnp.zeros((1, n), F32))
    shifted = pltpu.roll(jnp.broadcast_to(diag, (BAND_TILE, n)), n - (BAND_TILE - 1), 1, stride=1, stride_axis=0)
    bias = jnp.where(visible, shifted[:, 0:w] * LOG2E, NEG_INF)
    o_ref[...] = bias
    ot_ref[...] = bias.T


def _band_bias(table):
    w = B_REACH + BAND_TILE
    return pl.pallas_call(
        _band_bias_kernel,
        out_shape=[jax.ShapeDtypeStruct((B_HEADS, BAND_TILE, w), F32),
                   jax.ShapeDtypeStruct((B_HEADS, w, BAND_TILE), F32)],
        grid=(B_HEADS,),
        in_specs=[pl.BlockSpec(memory_space=pltpu.SMEM)],
        out_specs=[pl.BlockSpec((None, BAND_TILE, w), lambda h: (h, 0, 0)),
                   pl.BlockSpec((None, w, BAND_TILE), lambda h: (h, 0, 0))],
        compiler_params=_params("arbitrary"),
        name="band_bias",
    )(table)


def _ones_rows(n):
    return jnp.ones((ONES_ROWS, n), BF16)


def _mla_prompt_kernel(q_ref, k_ref, vt_ref, o_ref, m_ref, l_ref, acc_ref, s0_ref, s1_ref, smax0_ref, smax1_ref):
    i = pl.program_id(1)
    tq, tk = MLA_Q_TILE, MLA_K_TILE
    m_ref[...] = jnp.full(m_ref.shape, NEG_INF, F32)
    l_ref[...] = jnp.zeros(l_ref.shape, F32)
    acc_ref[...] = jnp.zeros(acc_ref.shape, F32)
    per_q = tq // tk
    assert per_q == 2
    s_refs = (s0_ref, s1_ref)
    smax_refs = (smax0_ref, smax1_ref)

    def live_queries(diag):
        return slice(0 if diag is None else diag * tk, tq)

    def score(j, slot, hd, diag=None):
        ks = pl.multiple_of(j * tk, tk)
        sl = slice(hd * LANES, (hd + 1) * LANES)
        cols = live_queries(diag)
        s = _dot_nt(k_ref[pl.ds(ks, tk), sl], q_ref[cols, sl])
        s_refs[slot][hd, :, cols] = s
        if diag is None:
            smax_refs[slot][hd] = jnp.max(s, axis=0, keepdims=True)

    def block(j, slot, diag, next_diag, stage_next=True):
        ks = pl.multiple_of(j * tk, tk)
        cols = live_queries(diag)
        m_all = [m_ref[hd, :, cols] for hd in range(A_HEADS)]
        l_all = [l_ref[hd, :, cols] for hd in range(A_HEADS)]
        acc_all = [acc_ref[hd, :, cols] for hd in range(A_HEADS)]
        ones = _ones_rows(tk)
        if diag is not None:
            shape = (tk, tq - cols.start)
            key_chunk = (diag * tk + lax.broadcasted_iota(jnp.int32, shape, 0)) // CHUNK
            qry_chunk = (cols.start + lax.broadcasted_iota(jnp.int32, shape, 1)) // CHUNK
            visible = key_chunk <= qry_chunk
        new = []
        for hd in range(A_HEADS):
            if stage_next:
                score(j + 1, 1 - slot, hd, next_diag)
            s = s_refs[slot][hd, :, cols]
            if diag is None:
                s_max = smax_refs[slot][hd]
            else:
                s = jnp.where(visible, s, NEG_INF)
                s_max = jnp.max(s, axis=0, keepdims=True)
            m_new = jnp.maximum(m_all[hd], s_max)
            alpha = jnp.exp2(m_all[hd] - m_new)
            p = jnp.exp2(s - m_new).astype(BF16)
            vt = jnp.concatenate([vt_ref[hd * A_V:(hd + 1) * A_V, pl.ds(ks, tk)], ones], axis=0)
            pv = _dot(vt, p)
            new.append((m_new, alpha * l_all[hd] + pv[A_V:A_V + 1], alpha * acc_all[hd] + pv[0:A_V]))
        for hd, (m_new, l_new, acc_new) in enumerate(new):
            m_ref[hd, :, cols] = m_new
            l_ref[hd, :, cols] = l_new
            acc_ref[hd, :, cols] = acc_new

    for hd in range(A_HEADS):
        score(0, 0, hd)

    def body(pair, carry):
        for d in range(per_q):
            block(pair * per_q + d, d, None, None)
        return carry

    lax.fori_loop(0, i, body, 0)
    for d in range(per_q):
        block(i * per_q + d, d, d, d + 1, stage_next=d + 1 < per_q)
    outs = [acc_ref[hd] / l_ref[hd] for hd in range(A_HEADS)]
    o_ref[...] = jnp.concatenate(outs, axis=0).T.astype(BF16)


def _mla_prompt(qa, ka, vat, batch, seq):
    nq = seq // MLA_Q_TILE
    wa = A_HEADS * LANES
    wv = A_HEADS * A_V
    return pl.pallas_call(
        _mla_prompt_kernel,
        out_shape=jax.ShapeDtypeStruct((batch * seq, wv), BF16),
        grid=(batch, nq),
        in_specs=[pl.BlockSpec((MLA_Q_TILE, wa), lambda b, i: (b * nq + i, 0)),
                  pl.BlockSpec((seq, wa), lambda b, i: (b, 0)),
                  pl.BlockSpec((wv, seq), lambda b, i: (0, b))],
        out_specs=pl.BlockSpec((MLA_Q_TILE, wv), lambda b, i: (b * nq + i, 0)),
        scratch_shapes=[pltpu.VMEM((A_HEADS, 1, MLA_Q_TILE), F32), pltpu.VMEM((A_HEADS, 1, MLA_Q_TILE), F32),
                        pltpu.VMEM((A_HEADS, A_V, MLA_Q_TILE), F32),
                        pltpu.VMEM((A_HEADS, MLA_K_TILE, MLA_Q_TILE), F32),
                        pltpu.VMEM((A_HEADS, MLA_K_TILE, MLA_Q_TILE), F32),
                        pltpu.VMEM((A_HEADS, 1, MLA_Q_TILE), F32), pltpu.VMEM((A_HEADS, 1, MLA_Q_TILE), F32)],
        compiler_params=_params("parallel", "arbitrary"),
        name="mla_prompt",
    )(qa, ka, vat)


def _fill_padded(kpad, vtpad, k_ref, vt_ref, reach):
    kpad[0:reach, :] = jnp.zeros((reach, kpad.shape[1]), kpad.dtype)
    kpad[reach:, :] = k_ref[...]
    vtpad[:, 0:reach] = jnp.zeros((vtpad.shape[0], reach), vtpad.dtype)
    vtpad[:, reach:] = vt_ref[...]


def _window_attend(s, vt, sink=None):
    m = jnp.max(s, axis=0, keepdims=True)
    if sink is not None:
        m = jnp.maximum(m, sink)
    p = jnp.exp2(s - m).astype(BF16)
    d = vt.shape[0]
    pv = _dot(jnp.concatenate([vt, _ones_rows(vt.shape[1])], axis=0), p)
    l = pv[d:d + 1]
    if sink is not None:
        l = l + jnp.exp2(sink - m)
    return pv[0:d] * (1.0 / l)


def _band_prompt_kernel(q_ref, k_ref, vt_ref, bias_ref, o_ref, kpad, vtpad, s0_ref, s1_ref):
    i = pl.program_id(1)
    tq = BAND_TILE
    s_refs = (s0_ref, s1_ref)

    @pl.when(i == 0)
    def _():
        _fill_padded(kpad, vtpad, k_ref, vt_ref, B_REACH)

    w = B_REACH + tq
    n_pairs = B_HEADS // 2
    step = BAND_BLOCKS * tq

    def attend(mask_start):
        key = lax.broadcasted_iota(jnp.int32, (w, 2 * tq), 0)

        def first_row(blk):
            return pl.multiple_of((i * BAND_BLOCKS + blk) * tq, tq)

        def score(unit):
            blk, pair = divmod(unit, n_pairs)
            q2 = q_ref[2 * pair:2 * pair + 2, blk * tq:(blk + 1) * tq, :].reshape(2 * tq, LANES)
            s_refs[unit % 2][...] = _dot_nt(kpad[pl.ds(first_row(blk), w), pair * LANES:(pair + 1) * LANES], q2)

        n_units = BAND_BLOCKS * n_pairs
        score(0)
        for unit in range(n_units):
            if unit + 1 < n_units:
                score(unit + 1)
            blk, pair = divmod(unit, n_pairs)
            if pair == 0:
                outs = []
            s = s_refs[unit % 2][...] + bias_ref[pair]
            if mask_start:
                s = jnp.where(key >= B_REACH - first_row(blk), s, NEG_INF)
            o2 = _window_attend(s, vtpad[pair * LANES:(pair + 1) * LANES, pl.ds(first_row(blk), w)])
            outs.append(o2[0:HEAD_DIM, 0:tq])
            outs.append(o2[HEAD_DIM:, tq:])
            if pair == n_pairs - 1:
                o_ref[blk * tq:(blk + 1) * tq, :] = jnp.concatenate(outs, axis=0).T.astype(BF16)

    pl.when(i * step < B_REACH)(functools.partial(attend, True))
    pl.when(i * step >= B_REACH)(functools.partial(attend, False))


def _band_prompt(qb, kb, vbt, bias_pairs, batch, seq):
    step = BAND_BLOCKS * BAND_TILE
    nq = seq // step
    return pl.pallas_call(
        _band_prompt_kernel,
        out_shape=jax.ShapeDtypeStruct((batch * seq, B_WIDTH), BF16),
        grid=(batch, nq),
        in_specs=[pl.BlockSpec((B_HEADS, step, LANES), lambda b, i: (0, b * nq + i, 0)),
                  pl.BlockSpec((seq, B_WIDTH), lambda b, i: (b, 0)),
                  pl.BlockSpec((B_WIDTH, seq), lambda b, i: (0, b)),
                  _const_spec(bias_pairs.shape)],
        out_specs=pl.BlockSpec((step, B_WIDTH), lambda b, i: (b * nq + i, 0)),
        scratch_shapes=[pltpu.VMEM((seq + B_REACH, B_WIDTH), BF16), pltpu.VMEM((B_WIDTH, seq + B_REACH), BF16)]
                       + [pltpu.VMEM((B_REACH + BAND_TILE, 2 * BAND_TILE), F32)] * 2,
        compiler_params=_params("parallel", "arbitrary"),
        name="band_prompt",
    )(qb, kb, vbt, bias_pairs)


def _swa_prompt_kernel(sink_ref, q_ref, k_ref, vt_ref, o_ref, kpad, vtpad, s0_ref, s1_ref, *, layer):
    i = pl.program_id(1)
    s_refs = (s0_ref, s1_ref)

    @pl.when(i == 0)
    def _():
        _fill_padded(kpad, vtpad, k_ref, vt_ref, C_WINDOW)

    w = C_WINDOW + ATT_TILE
    group = C_HEADS // C_KV_HEADS
    key = lax.broadcasted_iota(jnp.int32, (w, ATT_TILE), 0)
    qry = lax.broadcasted_iota(jnp.int32, (w, ATT_TILE), 1)
    lo = (qry // CHUNK) * CHUNK
    in_window = (key >= lo) & (key < lo + C_WINDOW + CHUNK)

    def score(unit):
        blk, kv = divmod(unit, C_KV_HEADS)
        start = pl.multiple_of((i * SWA_BLOCKS + blk) * ATT_TILE, ATT_TILE)
        q = q_ref[kv * group:(kv + 1) * group, blk * ATT_TILE:(blk + 1) * ATT_TILE, :]
        s_refs[unit % 2][...] = _dot_nt(kpad[pl.ds(start, w), :], q.reshape(group * ATT_TILE, LANES))

    n_units = SWA_BLOCKS * C_KV_HEADS
    score(0)
    for unit in range(n_units):
        if unit + 1 < n_units:
            score(unit + 1)
        blk, kv = divmod(unit, C_KV_HEADS)
        if kv == 0:
            first = (i * SWA_BLOCKS + blk) * ATT_TILE
            start = pl.multiple_of(first, ATT_TILE)
            visible = in_window & (key >= C_WINDOW - first)
            mask = jnp.concatenate([jnp.where(visible, 0.0, NEG_INF)] * group, axis=1)
            outs = []
        sink = jnp.concatenate([jnp.full((1, ATT_TILE), sink_ref[layer, kv * group + g] * LOG2E, F32)
                                for g in range(group)], axis=1)
        og = _window_attend(s_refs[unit % 2][...] + mask, vtpad[kv * HEAD_DIM:(kv + 1) * HEAD_DIM, pl.ds(start, w)],
                            sink)
        outs.extend(og[:, g * ATT_TILE:(g + 1) * ATT_TILE] for g in range(group))
        if kv == C_KV_HEADS - 1:
            o_ref[blk * ATT_TILE:(blk + 1) * ATT_TILE, :] = jnp.concatenate(outs, axis=0).T.astype(BF16)


def _swa_prompt(sinks, layer, q, k, vt, batch, seq):
    step = SWA_BLOCKS * ATT_TILE
    nq = seq // step
    return pl.pallas_call(
        functools.partial(_swa_prompt_kernel, layer=layer),
        out_shape=jax.ShapeDtypeStruct((batch * seq, C_QWIDTH), BF16),
        grid=(batch, nq),
        in_specs=[pl.BlockSpec(memory_space=pltpu.SMEM),
                  pl.BlockSpec((C_HEADS, step, LANES), lambda b, i: (0, b * nq + i, 0)),
                  pl.BlockSpec((seq, C_KWIDTH), lambda b, i: (b, 0)),
                  pl.BlockSpec((C_KWIDTH, seq), lambda b, i: (0, b))],
        out_specs=pl.BlockSpec((step, C_QWIDTH), lambda b, i: (b * nq + i, 0)),
        scratch_shapes=[pltpu.VMEM((seq + C_WINDOW, C_KWIDTH), BF16), pltpu.VMEM((C_KWIDTH, seq + C_WINDOW), BF16)]
                       + [pltpu.VMEM((C_WINDOW + ATT_TILE, C_HEADS // C_KV_HEADS * ATT_TILE), F32)] * 2,
        compiler_params=_params("parallel", "arbitrary"),
        name="swa_prompt",
    )(sinks, q, k, vt)


def _softmax_pv2(s_c, s_n, v_c, v_n, sink=None, v_c_transposed=False):
    m = jnp.maximum(jnp.max(s_c, axis=-1, keepdims=True), jnp.max(s_n, axis=-1, keepdims=True))
    if sink is not None:
        m = jnp.maximum(m, sink)
    p_c = jnp.exp2(s_c - m)
    p_n = jnp.exp2(s_n - m)
    l = jnp.sum(p_c, axis=-1, keepdims=True) + jnp.sum(p_n, axis=-1, keepdims=True)
    if sink is not None:
        l = l + jnp.exp2(sink - m)
    pv_c = _dot_nt(p_c.astype(BF16), v_c) if v_c_transposed else _dot(p_c.astype(BF16), v_c)
    return (pv_c + _dot(p_n.astype(BF16), v_n)) / l


def _even_sample_kernel(qa_ref, ka_ref, va_ref, qb_ref, kb_ref, vb_ref,
                        cckv_ref, ckpe_ref, cbk_ref, cbv_ref, wukv_ref, pick_ref, bias_ref,
                        oa_ref, ob_ref):
    t = qa_ref.shape[0]
    nb = cbk_ref.shape[1]
    kvc = _dot(cckv_ref[...].astype(BF16), wukv_ref[...])
    kpe_t = ckpe_ref[...].astype(BF16)
    scores_a, scores_b = [], []
    for hd in range(A_HEADS):
        sl = slice(hd * LANES, (hd + 1) * LANES)
        q = qa_ref[:, sl]
        q_rope = _dot(q, pick_ref[...]).astype(BF16)
        s_c = _dot_nt(q, kvc[:, sl].astype(BF16)) + _dot(q_rope, kpe_t)
        scores_a.append((s_c, _dot_nt(q, ka_ref[:, sl])))
    for hd in range(B_HEADS):
        pair = slice((hd // 2) * LANES, (hd // 2 + 1) * LANES)
        q = qb_ref[hd]
        scores_b.append((_dot(q, cbk_ref[pair, :].astype(BF16)) + bias_ref[hd, 0:t, 0:nb],
                         _dot_nt(q, kb_ref[:, pair]) + bias_ref[hd, 0:t, nb:nb + t]))
    outs_a, outs_b = [], []
    for hd in range(A_HEADS):
        vsl = slice(hd * A_V, (hd + 1) * A_V)
        v_c = kvc[:, A_HEADS * LANES + hd * A_V:A_HEADS * LANES + (hd + 1) * A_V].astype(BF16)
        outs_a.append(_softmax_pv2(*scores_a[hd], v_c, va_ref[:, vsl]))
    for hd in range(B_HEADS):
        sl = slice(hd * HEAD_DIM, (hd + 1) * HEAD_DIM)
        outs_b.append(_softmax_pv2(*scores_b[hd], cbv_ref[sl, :].astype(BF16), vb_ref[:, sl], v_c_transposed=True))
    oa_ref[...] = jnp.concatenate(outs_a, axis=-1).astype(BF16)
    ob_ref[...] = jnp.concatenate(outs_b, axis=-1).astype(BF16)


def _even_sample(layer, qa, ka, va, qb, kb, vb, c_ckv, c_kpe_t, c_bk_t, c_bv_t, wukv, pick, bias, dec_batch, t):
    past = c_ckv.shape[2]
    nb = c_bk_t.shape[3]

    def new(w):
        return pl.BlockSpec((t, w), lambda b: (b, 0))

    def cache(n, w):
        return pl.BlockSpec((None, None, n, w), lambda b: (layer, b, 0, 0))

    return pl.pallas_call(
        _even_sample_kernel,
        out_shape=[jax.ShapeDtypeStruct((dec_batch * t, B_WIDTH), BF16)] * 2,
        grid=(dec_batch,),
        in_specs=[new(A_HEADS * LANES), new(A_HEADS * LANES), new(A_HEADS * A_V),
                  pl.BlockSpec((B_HEADS, t, LANES), lambda b: (0, b, 0)), new(B_WIDTH), new(B_WIDTH),
                  cache(past, A_KV_LORA), cache(A_ROPE, past), cache(B_WIDTH, nb), cache(B_WIDTH, nb),
                  _const_spec(wukv.shape), _const_spec(pick.shape), _const_spec(bias.shape)],
        out_specs=[new(B_WIDTH), new(B_WIDTH)],
        compiler_params=_params("parallel"),
        name="even_sample",
    )(qa, ka, va, qb, kb, vb, c_ckv, c_kpe_t, c_bk_t, c_bv_t, wukv, pick, bias)


def _odd_sample_kernel(sink_ref, q_ref, k_ref, v_ref, ck_ref, cv_ref, o_ref, *, layer):
    group = C_HEADS // C_KV_HEADS
    scores = []
    for hd in range(C_HEADS):
        ksl = slice((hd // group) * HEAD_DIM, (hd // group + 1) * HEAD_DIM)
        q = q_ref[:, hd * HEAD_DIM:(hd + 1) * HEAD_DIM]
        scores.append((_dot_nt(q, ck_ref[:, ksl].astype(BF16)), _dot_nt(q, k_ref[:, ksl])))
    outs = []
    for hd in range(C_HEADS):
        ksl = slice((hd // group) * HEAD_DIM, (hd // group + 1) * HEAD_DIM)
        outs.append(_softmax_pv2(*scores[hd], cv_ref[:, ksl].astype(BF16), v_ref[:, ksl],
                                 sink_ref[layer, hd] * LOG2E))
    o_ref[...] = jnp.concatenate(outs, axis=-1).astype(BF16)


def _odd_sample(sinks, layer, q, k, v, c_k, c_v, dec_batch, t):
    nc = c_k.shape[2]

    def new(w):
        return pl.BlockSpec((t, w), lambda b: (b, 0))

    def cache():
        return pl.BlockSpec((None, None, nc, C_KWIDTH), lambda b: (layer, b, 0, 0))

    return pl.pallas_call(
        functools.partial(_odd_sample_kernel, layer=layer),
        out_shape=jax.ShapeDtypeStruct((dec_batch * t, C_QWIDTH), BF16),
        grid=(dec_batch,),
        in_specs=[pl.BlockSpec(memory_space=pltpu.SMEM), new(C_QWIDTH), new(C_KWIDTH), new(C_KWIDTH),
                  cache(), cache()],
        out_specs=new(C_QWIDTH),
        compiler_params=_params("parallel"),
        name="odd_sample",
    )(sinks, q, k, v, c_k, c_v)


def _rope_parts(pos, n_rot):
    half = n_rot // 2
    inv = ROPE_THETA ** (-jnp.arange(half, dtype=F32) / half)
    ang = pos.astype(F32)[:, None] * inv[None, :]
    return jnp.cos(ang), jnp.sin(ang)


def _rope_tables(pos, n_rot, pre, width):
    cos, sin = _rope_parts(pos, n_rot)
    n = pos.shape[0]
    post = width - pre - n_rot

    def head(first, second, fill):
        return jnp.concatenate([jnp.full((n, pre), fill, F32), first, second, jnp.full((n, post), fill, F32)], axis=1)

    zero = jnp.zeros_like(sin)
    tabs = (head(cos, cos, 1.0), head(zero, sin, 0.0), head(-sin, zero, 0.0))
    return tuple(jnp.tile(t, (1, LANES // width)) for t in tabs)


def kernel(x_prompt, x_sample, cache_mla_ckv, cache_mla_kpe, cache_band_k, cache_band_v,
           cache_swa_k, cache_swa_v, norm_g, ffn_w_gate, ffn_w_up, ffn_w_down, even_w_in,
           mla_q_norm, mla_w_uq, mla_kv_norm, mla_w_ukv, band_rel_bias, even_w_out,
           odd_w_in, swa_sinks, odd_w_out):
    batch, seq, _ = x_prompt.shape
    dec_batch, t_new, _ = x_sample.shape
    depth = norm_g.shape[0]
    n_even = even_w_in.shape[0]
    n_odd = odd_w_in.shape[0]
    past = cache_mla_ckv.shape[2]
    n_p = batch * seq
    n_s = dec_batch * t_new
    assert seq % TOKEN_TILE == 0 and n_s % TOKEN_TILE == 0 and TOKEN_TILE % t_new == 0
    assert n_p % FFN_TILE == 0 and n_s % FFN_TILE == 0
    assert seq % MLA_Q_TILE == 0 and seq % (BAND_BLOCKS * BAND_TILE) == 0 and seq % (SWA_BLOCKS * ATT_TILE) == 0 and min(B_REACH, seq) == TOKEN_TILE and t_new <= CHUNK
    assert cache_band_k.shape[2] == B_REACH and cache_swa_k.shape[2] == C_WINDOW and past >= B_REACH

    pos_p = jnp.arange(seq, dtype=jnp.int32)
    pos_s = past + (jnp.arange(TOKEN_TILE, dtype=jnp.int32) % t_new)
    mla_tabs = {True: _rope_tables(pos_p, A_ROPE, A_NOPE, LANES), False: _rope_tables(pos_s, A_ROPE, A_NOPE, LANES)}
    swa_tabs = {True: _rope_tables(pos_p, C_ROT, 0, HEAD_DIM), False: _rope_tables(pos_s, C_ROT, 0, HEAD_DIM)}

    wg = ffn_w_gate.astype(BF16)
    wu = ffn_w_up.astype(BF16)
    wd = ffn_w_down.astype(BF16)

    pick = jnp.zeros((LANES, A_ROPE), F32).at[A_NOPE + jnp.arange(A_ROPE), jnp.arange(A_ROPE)].set(1.0).astype(BF16)
    c_kpe_t = cache_mla_kpe.transpose(0, 1, 3, 2)
    c_bk_t = cache_band_k.transpose(0, 1, 3, 4, 2).reshape(n_even, dec_batch, B_WIDTH, B_REACH)
    c_bv_t = cache_band_v.transpose(0, 1, 3, 4, 2).reshape(n_even, dec_batch, B_WIDTH, B_REACH)
    c_sk = cache_swa_k.reshape(n_odd, dec_batch, C_WINDOW, C_KWIDTH)
    c_sv = cache_swa_v.reshape(n_odd, dec_batch, C_WINDOW, C_KWIDTH)

    c0 = A_Q_LORA + A_KV_LORA
    c1 = c0 + A_ROPE
    xs = {True: x_prompt.reshape(n_p, D_MODEL), False: x_sample.reshape(n_s, D_MODEL)}
    even_st = {True: None, False: None}
    odd_st = {True: None, False: None}
    for l in range(depth):
        g = norm_g[l]
        i = l // 2
        if l % 2 == 0:
            w = even_w_in[i].astype(BF16)
            zc = lambda n: jnp.zeros((D_MODEL, n), BF16)
            win = jnp.concatenate([w[:, :c0], zc(A_NOPE), w[:, c0:c1], zc(LANES - A_NOPE - A_ROPE), w[:, c1:]], axis=1)
            wuq = jnp.pad(mla_w_uq[i].reshape(A_Q_LORA, A_HEADS, A_QK),
                          ((0, 0), (0, 0), (0, LANES - A_QK))).reshape(A_Q_LORA, A_HEADS * LANES).astype(BF16)
            wkv = mla_w_ukv[i].reshape(A_KV_LORA, A_HEADS, A_NOPE + A_V)
            wk = jnp.pad(wkv[:, :, :A_NOPE], ((0, 0), (0, 0), (0, LANES - A_NOPE))).reshape(A_KV_LORA, A_HEADS * LANES)
            wv = wkv[:, :, A_NOPE:].reshape(A_KV_LORA, A_HEADS * A_V)
            wukv = jnp.concatenate([wk, wv], axis=1).astype(BF16)
            bias, bias_t = _band_bias(band_rel_bias[i])
            w_band = B_REACH + BAND_TILE
            bias_pairs = bias_t.reshape(B_HEADS // 2, 2, w_band, BAND_TILE).transpose(0, 2, 1, 3).reshape(
                B_HEADS // 2, w_band, 2 * BAND_TILE)
            wo = even_w_out[i].astype(BF16)
        else:
            win = odd_w_in[i].astype(BF16)
            wo = odd_w_out[i].astype(BF16)
        for prompt in (True, False):
            x = _ffn(xs[prompt], g[0:2], wg, wu, wd, l, 0)
            if l % 2 == 0:
                qa, ka, va, qb, kb, vb, *even_st[prompt] = _even_proj(
                    x, g[2:3], win, mla_q_norm[i][None], mla_kv_norm[i][None], wuq, wukv, mla_tabs[prompt],
                    prompt, batch, seq, i, n_even, even_st[prompt])
                if prompt:
                    mixes = [_mla_prompt(qa, ka, va, batch, seq), _band_prompt(qb, kb, vb, bias_pairs, batch, seq)]
                else:
                    mixes = _even_sample(i, qa, ka, va, qb, kb, vb, cache_mla_ckv, c_kpe_t, c_bk_t, c_bv_t,
                                         wukv, pick, bias, dec_batch, t_new)
            else:
                q, k, v, *odd_st[prompt] = _odd_proj(x, g[2:3], win, swa_tabs[prompt], prompt, batch, seq,
                                                     i, n_odd, odd_st[prompt])
                if prompt:
                    mixes = [_swa_prompt(swa_sinks, i, q, k, v, batch, seq)]
                else:
                    mixes = [_odd_sample(swa_sinks, i, q, k, v, c_sk, c_sv, dec_batch, t_new)]
            xs[prompt] = _mix_ffn(x, mixes, wo, g[3:4], g[4:6], wg, wu, wd, l, 1)

    y_prompt = xs[True].reshape(batch, seq, D_MODEL)
    y_sample = xs[False].reshape(dec_batch, t_new, D_MODEL)

    def states(prompt, lead):
        ckv, kpe, kbf, vbf = even_st[prompt]
        kf, vf = odd_st[prompt]
        if prompt:
            kbf, vbf = (a.reshape(n_even, B_HEADS, HEAD_DIM, lead[0], -1).transpose(0, 3, 4, 1, 2) for a in (kbf, vbf))
        return (ckv.reshape(n_even, *lead, A_KV_LORA), kpe.reshape(n_even, *lead, A_ROPE),
                kbf.reshape(n_even, lead[0], -1, B_HEADS, HEAD_DIM), vbf.reshape(n_even, lead[0], -1, B_HEADS, HEAD_DIM),
                kf.reshape(n_odd, lead[0], -1, C_KV_HEADS, HEAD_DIM), vf.reshape(n_odd, lead[0], -1, C_KV_HEADS, HEAD_DIM))

    return (y_prompt, y_sample, *states(True, (batch, seq)), *states(False, (dec_batch, t_new)))
```

```python
import functools
import math

import jax
import jax.numpy as jnp
from jax import lax
from jax.experimental import pallas as pl
from jax.experimental.pallas import tpu as pltpu

F32 = jnp.float32
BF16 = jnp.bfloat16

D_MODEL = 1024
D_FF = 2816
CHUNK = 64
HEAD_DIM = 64
ROPE_THETA = 500000.0
NORM_EPS = 1e-6
NEG_INF = -1e30
A_HEADS = 8
A_Q_LORA = 256
A_KV_LORA = 128
A_NOPE = 64
A_ROPE = 32
A_QK = A_NOPE + A_ROPE
A_V = 64
A_SCALE = A_QK ** -0.5
B_HEADS = 8
B_REACH = 512
B_MAX_REL = 128
C_HEADS = 16
C_KV_HEADS = 2
C_WINDOW = 128
C_ROT = HEAD_DIM // 4
HEAD_SCALE = HEAD_DIM ** -0.5
LOG2E = math.log2(math.e)

LANES = 128
TOKEN_TILE = 512
FFN_TILE = 1024
FFN_PARTS = 2
FFN_ROWS = FFN_TILE // FFN_PARTS
FF_TILE = 256
PROJ_PARTS = 2
PROJ_ROWS = TOKEN_TILE // PROJ_PARTS
ATT_TILE = 128
BAND_TILE = 256
BAND_BLOCKS = 4
MLA_Q_TILE = 512
MLA_K_TILE = 256
BIAS_DIAG_LANES = 1024
SWA_BLOCKS = 8
ONES_ROWS = 16
VMEM_LIMIT = 56 * 1024 * 1024
B_WIDTH = B_HEADS * HEAD_DIM
C_QWIDTH = C_HEADS * HEAD_DIM
C_KWIDTH = C_KV_HEADS * HEAD_DIM
EVEN_CKV = A_Q_LORA
EVEN_KPE = EVEN_CKV + A_KV_LORA
EVEN_QB = EVEN_KPE + LANES
EVEN_KB = EVEN_QB + B_WIDTH
EVEN_VB = EVEN_KB + B_WIDTH


def _params(*sem):
    return pltpu.CompilerParams(dimension_semantics=sem, vmem_limit_bytes=VMEM_LIMIT)


def _const_spec(shape, index=None):
    index = index or (0,) * len(shape)
    return pl.BlockSpec(shape, lambda *_: index, pipeline_mode=pl.Buffered(1))


def _rms(x, g):
    ms = jnp.mean(x * x, axis=-1, keepdims=True)
    return x * lax.rsqrt(ms + NORM_EPS) * g


def _dot(a, b):
    return jnp.dot(a, b, preferred_element_type=F32)


def _dot_nt(a, b):
    return lax.dot_general(a, b, (((1,), (1,)), ((), ())), preferred_element_type=F32)


def _rope_group(x, c, s1, s2, half):
    return x * c + pltpu.roll(x, half, 1) * s1 + pltpu.roll(x, LANES - half, 1) * s2


def _lane_mask(lo):
    lane = lax.broadcasted_iota(jnp.int32, (1, LANES), 1)
    return (lane >= lo) & (lane < lo + HEAD_DIM)


def _swiglu_update(x, g_ref, wg_ref, wu_ref, wd_ref):
    xn = _rms(x, g_ref[0:1, :]).astype(BF16)
    acc = jnp.zeros(x.shape, F32)
    for c in range(D_FF // FF_TILE):
        sl = slice(c * FF_TILE, (c + 1) * FF_TILE)
        a = _dot(xn, wg_ref[:, sl])
        b = _dot(xn, wu_ref[:, sl])
        h = (a * jax.nn.sigmoid(a) * b).astype(BF16)
        acc = acc + _dot(h, wd_ref[sl, :])
    return x + 0.5 * _rms(acc, g_ref[1:2, :])


def _ffn_row_parts():
    return [slice(p * FFN_ROWS, (p + 1) * FFN_ROWS) for p in range(FFN_PARTS)]


def _ffn_kernel(x_ref, g_ref, wg_ref, wu_ref, wd_ref, o_ref):
    for rows in _ffn_row_parts():
        o_ref[rows, :] = _swiglu_update(x_ref[rows, :], g_ref, wg_ref, wu_ref, wd_ref)


def _mix_ffn_kernel(*refs, n_mix):
    x_ref = refs[0]
    mix_refs = refs[1:1 + n_mix]
    wo_ref, gm_ref, g_ref, wg_ref, wu_ref, wd_ref, o_ref = refs[1 + n_mix:]
    ys = []
    for rows in _ffn_row_parts():
        y = None
        row = 0
        for m_ref in mix_refs:
            w = m_ref.shape[1]
            part = _dot(m_ref[rows, :], wo_ref[row:row + w, :])
            y = part if y is None else y + part
            row += w
        ys.append(y)
    for rows, y in zip(_ffn_row_parts(), ys):
        x = x_ref[rows, :] + _rms(y, gm_ref[...])
        o_ref[rows, :] = _swiglu_update(x, g_ref, wg_ref, wu_ref, wd_ref)


def _ffn_weight_specs(layer, which):
    idx = (layer, which, 0, 0)
    return [_const_spec((None, None, D_MODEL, D_FF), idx), _const_spec((None, None, D_MODEL, D_FF), idx),
            _const_spec((None, None, D_FF, D_MODEL), idx)]


def _ffn(x, g, wg, wu, wd, layer, which):
    t = x.shape[0]
    tile = pl.BlockSpec((FFN_TILE, D_MODEL), lambda i: (i, 0))
    return pl.pallas_call(
        _ffn_kernel,
        out_shape=jax.ShapeDtypeStruct((t, D_MODEL), F32),
        grid=(t // FFN_TILE,),
        in_specs=[tile, _const_spec((2, D_MODEL))] + _ffn_weight_specs(layer, which),
        out_specs=tile,
        compiler_params=_params("parallel"),
        name="ffn",
    )(x, g, wg, wu, wd)


def _mix_ffn(x, mixes, wo, gm, g, wg, wu, wd, layer, which):
    t = x.shape[0]
    tile = pl.BlockSpec((FFN_TILE, D_MODEL), lambda i: (i, 0))
    mix_specs = [pl.BlockSpec((FFN_TILE, m.shape[1]), lambda i: (i, 0)) for m in mixes]
    return pl.pallas_call(
        functools.partial(_mix_ffn_kernel, n_mix=len(mixes)),
        out_shape=jax.ShapeDtypeStruct((t, D_MODEL), F32),
        grid=(t // FFN_TILE,),
        in_specs=[tile] + mix_specs + [_const_spec(wo.shape), _const_spec((1, D_MODEL)), _const_spec((2, D_MODEL))]
                 + _ffn_weight_specs(layer, which),
        out_specs=tile,
        compiler_params=_params("parallel"),
        name="mix_ffn",
    )(x, *mixes, wo, gm, g, wg, wu, wd)


def _even_proj_kernel(x_ref, g_ref, win_ref, qn_ref, kvn_ref, wuq_ref, wukv_ref,
                      c_ref, s1_ref, s2_ref,
                      qa_ref, ka_ref, va_ref, qb_ref, kb_ref, vb_ref,
                      ckv_ref, kpe_ref, kbf_ref, vbf_ref, *scratch, prompt, tiles_per_seq):
    half = A_ROPE // 2
    for part in range(PROJ_PARTS):
        rows = slice(part * PROJ_ROWS, (part + 1) * PROJ_ROWS)
        h = _rms(x_ref[rows, :], g_ref[...]).astype(BF16)
        proj = _dot(h, win_ref[...])
        cq = _rms(proj[:, 0:EVEN_CKV], qn_ref[...]).astype(BF16)
        ckv = _rms(proj[:, EVEN_CKV:EVEN_KPE], kvn_ref[...])
        c, s1, s2 = c_ref[rows, :], s1_ref[rows, :], s2_ref[rows, :]
        kpe = _rope_group(proj[:, EVEN_KPE:EVEN_QB], c, s1, s2, half)
        ckv_ref[rows, :] = ckv
        kpe_ref[rows, :] = pltpu.roll(kpe, LANES - A_NOPE, 1)[:, 0:A_ROPE]
        qa = _dot(cq, wuq_ref[...])
        kv = _dot(ckv.astype(BF16), wukv_ref[...])
        for hd in range(A_HEADS):
            sl = slice(hd * LANES, (hd + 1) * LANES)
            qa_ref[rows, sl] = (_rope_group(qa[:, sl], c, s1, s2, half) * (A_SCALE * LOG2E)).astype(BF16)
            ka_ref[rows, sl] = (kv[:, sl] + kpe).astype(BF16)
        for pair in range(B_HEADS // 2):
            grp = proj[:, EVEN_QB + pair * LANES:EVEN_QB + (pair + 1) * LANES] * (HEAD_SCALE * LOG2E)
            for sub in range(2):
                qb_ref[2 * pair + sub, rows, :] = jnp.where(_lane_mask(sub * HEAD_DIM), grp, 0.0).astype(BF16)
        va = kv[:, A_HEADS * LANES:]
        kb = proj[:, EVEN_KB:EVEN_VB]
        vb = proj[:, EVEN_VB:EVEN_VB + B_WIDTH]
        kb_ref[rows, :] = kb.astype(BF16)
        if prompt:
            va_ref[:, rows] = va.T.astype(BF16)
            vb_ref[:, rows] = vb.T.astype(BF16)
            scratch[0][rows, :] = kb
            scratch[1][rows, :] = vb
        else:
            va_ref[rows, :] = va.astype(BF16)
            vb_ref[rows, :] = vb.astype(BF16)
            kbf_ref[rows, :] = kb
            vbf_ref[rows, :] = vb
    if prompt:
        @pl.when(pl.program_id(0) % tiles_per_seq == tiles_per_seq - 1)
        def _():
            kbf_ref[...] = scratch[0][...].T
            vbf_ref[...] = scratch[1][...].T


def _last_tile_index(tiles_per_seq):
    return lambda i: jnp.maximum((i + 1) // tiles_per_seq - 1, 0)


def _skip_aliased(kernel, n_in, n_alias):
    def body(*refs):
        return kernel(*refs[:n_in], *refs[n_in + n_alias:])
    return body


def _layer_slot(n_layers, layer, rows, width, block_rows, row_index=lambda i: i, transposed=False):
    if transposed:
        return (jax.ShapeDtypeStruct((n_layers, rows // block_rows, width, block_rows), F32),
                pl.BlockSpec((None, None, width, block_rows), lambda i: (layer, row_index(i), 0, 0)))
    return (jax.ShapeDtypeStruct((n_layers, rows, width), F32),
            pl.BlockSpec((None, block_rows, width), lambda i: (layer, row_index(i), 0)))


def _alias_args(prev, n_in, first_out):
    if prev is None:
        return [], {}
    return [pl.BlockSpec(memory_space=pl.ANY)] * len(prev), {n_in + k: first_out + k for k in range(len(prev))}


def _even_proj(x, g, win, qn, kvn, wuq, wukv, tabs, prompt, batch, seq, layer, n_layers, prev):
    t = x.shape[0]
    tiles_per_seq = seq // TOKEN_TILE if prompt else 1

    def tile(w):
        return pl.BlockSpec((TOKEN_TILE, w), lambda i: (i, 0))

    def tile_t(w):
        return pl.BlockSpec((w, TOKEN_TILE), lambda i: (0, i))

    def sds(shape, dtype):
        return jax.ShapeDtypeStruct(shape, dtype)

    wa = A_HEADS * LANES
    if prompt:
        v_shapes = [sds((A_HEADS * A_V, t), BF16), sds((B_WIDTH, t), BF16)]
        v_specs = [tile_t(A_HEADS * A_V), tile_t(B_WIDTH)]
        keep = _layer_slot(n_layers, layer, batch * TOKEN_TILE, B_WIDTH, TOKEN_TILE, _last_tile_index(tiles_per_seq),
                           transposed=True)
        tab_idx = lambda i: (i % tiles_per_seq, 0)
    else:
        v_shapes = [sds((t, A_HEADS * A_V), BF16), sds((t, B_WIDTH), BF16)]
        v_specs = [tile(A_HEADS * A_V), tile(B_WIDTH)]
        keep = _layer_slot(n_layers, layer, t, B_WIDTH, TOKEN_TILE)
        tab_idx = lambda i: (0, 0)
    ckv = _layer_slot(n_layers, layer, t, A_KV_LORA, TOKEN_TILE)
    kpe = _layer_slot(n_layers, layer, t, A_ROPE, TOKEN_TILE)
    qb_spec = pl.BlockSpec((B_HEADS, TOKEN_TILE, LANES), lambda i: (0, i, 0))
    out_shape = [sds((t, wa), BF16), sds((t, wa), BF16), v_shapes[0], sds((B_HEADS, t, LANES), BF16),
                 sds((t, B_WIDTH), BF16), v_shapes[1], ckv[0], kpe[0], keep[0], keep[0]]
    out_specs = [tile(wa), tile(wa), v_specs[0], qb_spec, tile(B_WIDTH), v_specs[1], ckv[1], kpe[1], keep[1], keep[1]]
    in_specs = ([tile(D_MODEL), _const_spec((1, D_MODEL)), _const_spec(win.shape),
                 _const_spec((1, A_Q_LORA)), _const_spec((1, A_KV_LORA)),
                 _const_spec(wuq.shape), _const_spec(wukv.shape)]
                + [pl.BlockSpec((TOKEN_TILE, LANES), tab_idx)] * 3)
    alias_specs, aliases = _alias_args(prev, len(in_specs), 6)
    kernel_fn = functools.partial(_even_proj_kernel, prompt=prompt, tiles_per_seq=tiles_per_seq)
    return pl.pallas_call(
        _skip_aliased(kernel_fn, len(in_specs), len(alias_specs)),
        out_shape=out_shape,
        grid=(t // TOKEN_TILE,),
        in_specs=in_specs + alias_specs,
        out_specs=out_specs,
        scratch_shapes=[pltpu.VMEM((TOKEN_TILE, B_WIDTH), F32)] * 2 if prompt else [],
        input_output_aliases=aliases,
        compiler_params=_params("arbitrary"),
        name="even_proj",
    )(x, g, win, qn, kvn, wuq, wukv, *tabs, *(prev or ()))


def _odd_proj_kernel(x_ref, g_ref, win_ref, c_ref, s1_ref, s2_ref,
                     q_ref, k_ref, v_ref, kf_ref, vf_ref, *scratch, prompt, tiles_per_seq):
    half = C_ROT // 2
    group = C_HEADS // C_KV_HEADS
    for part in range(PROJ_PARTS):
        rows = slice(part * PROJ_ROWS, (part + 1) * PROJ_ROWS)
        h = _rms(x_ref[rows, :], g_ref[...]).astype(BF16)
        proj = _dot(h, win_ref[...])
        c, s1, s2 = c_ref[rows, :], s1_ref[rows, :], s2_ref[rows, :]
        k = _rope_group(proj[:, C_QWIDTH:C_QWIDTH + LANES], c, s1, s2, half)
        v = proj[:, C_QWIDTH + LANES:C_QWIDTH + 2 * LANES]
        k_ref[rows, :] = k.astype(BF16)
        if prompt:
            for pair in range(C_HEADS // 2):
                grp = _rope_group(proj[:, pair * LANES:(pair + 1) * LANES], c, s1, s2, half) * (HEAD_SCALE * LOG2E)
                swapped = pltpu.roll(grp, HEAD_DIM, 1)
                for sub in range(2):
                    hd = 2 * pair + sub
                    kv = hd // group
                    src = grp if sub == kv else swapped
                    q_ref[hd, rows, :] = jnp.where(_lane_mask(kv * HEAD_DIM), src, 0.0).astype(BF16)
            v_ref[:, rows] = v.T.astype(BF16)
            if part == PROJ_PARTS - 1:
                scratch[0][...] = k[PROJ_ROWS - C_WINDOW:, :]
                scratch[1][...] = v[PROJ_ROWS - C_WINDOW:, :]
        else:
            for grp in range(C_QWIDTH // LANES):
                sl = slice(grp * LANES, (grp + 1) * LANES)
                q_ref[rows, sl] = (_rope_group(proj[:, sl], c, s1, s2, half) * (HEAD_SCALE * LOG2E)).astype(BF16)
            v_ref[rows, :] = v.astype(BF16)
            kf_ref[rows, :] = k
            vf_ref[rows, :] = v
    if prompt:
        @pl.when(pl.program_id(0) % tiles_per_seq == tiles_per_seq - 1)
        def _():
            kf_ref[...] = scratch[0][...]
            vf_ref[...] = scratch[1][...]


def _odd_proj(x, g, win, tabs, prompt, batch, seq, layer, n_layers, prev):
    t = x.shape[0]
    tiles_per_seq = seq // TOKEN_TILE if prompt else 1

    def tile(w):
        return pl.BlockSpec((TOKEN_TILE, w), lambda i: (i, 0))

    def sds(shape, dtype):
        return jax.ShapeDtypeStruct(shape, dtype)

    if prompt:
        q_shape = sds((C_HEADS, t, LANES), BF16)
        q_spec = pl.BlockSpec((C_HEADS, TOKEN_TILE, LANES), lambda i: (0, i, 0))
        v_shape, v_spec = sds((LANES, t), BF16), pl.BlockSpec((LANES, TOKEN_TILE), lambda i: (0, i))
        keep = _layer_slot(n_layers, layer, batch * C_WINDOW, LANES, C_WINDOW, _last_tile_index(tiles_per_seq))
        tab_idx = lambda i: (i % tiles_per_seq, 0)
    else:
        q_shape, q_spec = sds((t, C_QWIDTH), BF16), tile(C_QWIDTH)
        v_shape, v_spec = sds((t, LANES), BF16), tile(LANES)
        keep = _layer_slot(n_layers, layer, t, LANES, TOKEN_TILE)
        tab_idx = lambda i: (0, 0)
    in_specs = ([tile(D_MODEL), _const_spec((1, D_MODEL)), _const_spec(win.shape)]
                + [pl.BlockSpec((TOKEN_TILE, LANES), tab_idx)] * 3)
    alias_specs, aliases = _alias_args(prev, len(in_specs), 3)
    kernel_fn = functools.partial(_odd_proj_kernel, prompt=prompt, tiles_per_seq=tiles_per_seq)
    return pl.pallas_call(
        _skip_aliased(kernel_fn, len(in_specs), len(alias_specs)),
        out_shape=[q_shape, sds((t, LANES), BF16), v_shape, keep[0], keep[0]],
        grid=(t // TOKEN_TILE,),
        in_specs=in_specs + alias_specs,
        out_specs=[q_spec, tile(LANES), v_spec, keep[1], keep[1]],
        scratch_shapes=[pltpu.VMEM((C_WINDOW, LANES), F32)] * 2 if prompt else [],
        input_output_aliases=aliases,
        compiler_params=_params("arbitrary"),
        name="odd_proj",
    )(x, g, win, *tabs, *(prev or ()))


def _band_bias_kernel(tab_ref, o_ref, ot_ref):
    hd = pl.program_id(0)
    w = B_REACH + BAND_TILE
    row = lax.broadcasted_iota(jnp.int32, (BAND_TILE, w), 0)
    col = lax.broadcasted_iota(jnp.int32, (BAND_TILE, w), 1)
    lo = (row // CHUNK) * CHUNK
    visible = (col >= lo) & (col < lo + B_REACH + CHUNK)
    n = BIAS_DIAG_LANES
    e = lax.broadcasted_iota(jnp.int32, (1, n), 1)
    rel = jnp.clip(B_REACH + BAND_TILE - 1 - e, -B_MAX_REL, B_MAX_REL) + B_MAX_REL

    def body(r, acc):
        return jnp.where(rel == r, tab_ref[hd, r], acc)

    diag = lax.fori_loop(0, 2 * B_MAX_REL + 1, body, jnp.zeros((1, n), F32))
    shifted = pltpu.roll(jnp.broadcast_to(diag, (BAND_TILE, n)), n - (BAND_TILE - 1), 1, stride=1, stride_axis=0)
    bias = jnp.where(visible, shifted[:, 0:w] * LOG2E, NEG_INF)
    o_ref[...] = bias
    ot_ref[...] = bias.T


def _band_bias(table):
    w = B_REACH + BAND_TILE
    return pl.pallas_call(
        _band_bias_kernel,
        out_shape=[jax.ShapeDtypeStruct((B_HEADS, BAND_TILE, w), F32),
                   jax.ShapeDtypeStruct((B_HEADS, w, BAND_TILE), F32)],
        grid=(B_HEADS,),
        in_specs=[pl.BlockSpec(memory_space=pltpu.SMEM)],
        out_specs=[pl.BlockSpec((None, BAND_TILE, w), lambda h: (h, 0, 0)),
                   pl.BlockSpec((None, w, BAND_TILE), lambda h: (h, 0, 0))],
        compiler_params=_params("arbitrary"),
        name="band_bias",
    )(table)


def _ones_rows(n):
    return jnp.ones((ONES_ROWS, n), BF16)


def _mla_prompt_kernel(q_ref, k_ref, vt_ref, o_ref, m_ref, l_ref, acc_ref, s0_ref, s1_ref, smax0_ref, smax1_ref):
    i = pl.program_id(1)
    tq, tk = MLA_Q_TILE, MLA_K_TILE
    m_ref[...] = jnp.full(m_ref.shape, NEG_INF, F32)
    l_ref[...] = jnp.zeros(l_ref.shape, F32)
    acc_ref[...] = jnp.zeros(acc_ref.shape, F32)
    per_q = tq // tk
    assert per_q == 2
    s_refs = (s0_ref, s1_ref)
    smax_refs = (smax0_ref, smax1_ref)

    def live_queries(diag):
        return slice(0 if diag is None else diag * tk, tq)

    def score(j, slot, hd, diag=None):
        ks = pl.multiple_of(j * tk, tk)
        sl = slice(hd * LANES, (hd + 1) * LANES)
        cols = live_queries(diag)
        s = _dot_nt(k_ref[pl.ds(ks, tk), sl], q_ref[cols, sl])
        s_refs[slot][hd, :, cols] = s
        if diag is None:
            smax_refs[slot][hd] = jnp.max(s, axis=0, keepdims=True)

    def block(j, slot, diag, next_diag, stage_next=True):
        ks = pl.multiple_of(j * tk, tk)
        cols = live_queries(diag)
        m_all = [m_ref[hd, :, cols] for hd in range(A_HEADS)]
        l_all = [l_ref[hd, :, cols] for hd in range(A_HEADS)]
        acc_all = [acc_ref[hd, :, cols] for hd in range(A_HEADS)]
        ones = _ones_rows(tk)
        if diag is not None:
            shape = (tk, tq - cols.start)
            key_chunk = (diag * tk + lax.broadcasted_iota(jnp.int32, shape, 0)) // CHUNK
            qry_chunk = (cols.start + lax.broadcasted_iota(jnp.int32, shape, 1)) // CHUNK
            visible = key_chunk <= qry_chunk
        new = []
        for hd in range(A_HEADS):
            if stage_next:
                score(j + 1, 1 - slot, hd, next_diag)
            s = s_refs[slot][hd, :, cols]
            if diag is None:
                s_max = smax_refs[slot][hd]
            else:
                s = jnp.where(visible, s, NEG_INF)
                s_max = jnp.max(s, axis=0, keepdims=True)
            m_new = jnp.maximum(m_all[hd], s_max)
            alpha = jnp.exp2(m_all[hd] - m_new)
            p = jnp.exp2(s - m_new).astype(BF16)
            vt = jnp.concatenate([vt_ref[hd * A_V:(hd + 1) * A_V, pl.ds(ks, tk)], ones], axis=0)
            pv = _dot(vt, p)
            new.append((m_new, alpha * l_all[hd] + pv[A_V:A_V + 1], alpha * acc_all[hd] + pv[0:A_V]))
        for hd, (m_new, l_new, acc_new) in enumerate(new):
            m_ref[hd, :, cols] = m_new
            l_ref[hd, :, cols] = l_new
            acc_ref[hd, :, cols] = acc_new

    for hd in range(A_HEADS):
        score(0, 0, hd)

    def body(pair, carry):
        for d in range(per_q):
            block(pair * per_q + d, d, None, None)
        return carry

    lax.fori_loop(0, i, body, 0)
    for d in range(per_q):
        block(i * per_q + d, d, d, d + 1, stage_next=d + 1 < per_q)
    outs = [acc_ref[hd] / l_ref[hd] for hd in range(A_HEADS)]
    o_ref[...] = jnp.concatenate(outs, axis=0).T.astype(BF16)


def _mla_prompt(qa, ka, vat, batch, seq):
    nq = seq // MLA_Q_TILE
    wa = A_HEADS * LANES
    wv = A_HEADS * A_V
    return pl.pallas_call(
        _mla_prompt_kernel,
        out_shape=jax.ShapeDtypeStruct((batch * seq, wv), BF16),
        grid=(batch, nq),
        in_specs=[pl.BlockSpec((MLA_Q_TILE, wa), lambda b, i: (b * nq + i, 0)),
                  pl.BlockSpec((seq, wa), lambda b, i: (b, 0)),
                  pl.BlockSpec((wv, seq), lambda b, i: (0, b))],
        out_specs=pl.BlockSpec((MLA_Q_TILE, wv), lambda b, i: (b * nq + i, 0)),
        scratch_shapes=[pltpu.VMEM((A_HEADS, 1, MLA_Q_TILE), F32), pltpu.VMEM((A_HEADS, 1, MLA_Q_TILE), F32),
                        pltpu.VMEM((A_HEADS, A_V, MLA_Q_TILE), F32),
                        pltpu.VMEM((A_HEADS, MLA_K_TILE, MLA_Q_TILE), F32),
                        pltpu.VMEM((A_HEADS, MLA_K_TILE, MLA_Q_TILE), F32),
                        pltpu.VMEM((A_HEADS, 1, MLA_Q_TILE), F32), pltpu.VMEM((A_HEADS, 1, MLA_Q_TILE), F32)],
        compiler_params=_params("parallel", "arbitrary"),
        name="mla_prompt",
    )(qa, ka, vat)


def _fill_padded(kpad, vtpad, k_ref, vt_ref, reach):
    kpad[0:reach, :] = jnp.zeros((reach, kpad.shape[1]), kpad.dtype)
    kpad[reach:, :] = k_ref[...]
    vtpad[:, 0:reach] = jnp.zeros((vtpad.shape[0], reach), vtpad.dtype)
    vtpad[:, reach:] = vt_ref[...]


def _window_attend(s, vt, sink=None):
    m = jnp.max(s, axis=0, keepdims=True)
    if sink is not None:
        m = jnp.maximum(m, sink)
    p = jnp.exp2(s - m).astype(BF16)
    d = vt.shape[0]
    pv = _dot(jnp.concatenate([vt, _ones_rows(vt.shape[1])], axis=0), p)
    l = pv[d:d + 1]
    if sink is not None:
        l = l + jnp.exp2(sink - m)
    return pv[0:d] * (1.0 / l)


def _band_prompt_kernel(q_ref, k_ref, vt_ref, bias_ref, o_ref, kpad, vtpad, s0_ref, s1_ref):
    i = pl.program_id(1)
    tq = BAND_TILE
    s_refs = (s0_ref, s1_ref)

    @pl.when(i == 0)
    def _():
        _fill_padded(kpad, vtpad, k_ref, vt_ref, B_REACH)

    w = B_REACH + tq
    n_pairs = B_HEADS // 2
    step = BAND_BLOCKS * tq

    def attend(mask_start):
        key = lax.broadcasted_iota(jnp.int32, (w, 2 * tq), 0)

        def first_row(blk):
            return pl.multiple_of((i * BAND_BLOCKS + blk) * tq, tq)

        def score(unit):
            blk, pair = divmod(unit, n_pairs)
            q2 = q_ref[2 * pair:2 * pair + 2, blk * tq:(blk + 1) * tq, :].reshape(2 * tq, LANES)
            s_refs[unit % 2][...] = _dot_nt(kpad[pl.ds(first_row(blk), w), pair * LANES:(pair + 1) * LANES], q2)

        n_units = BAND_BLOCKS * n_pairs
        score(0)
        for unit in range(n_units):
            if unit + 1 < n_units:
                score(unit + 1)
            blk, pair = divmod(unit, n_pairs)
            if pair == 0:
                outs = []
            s = s_refs[unit % 2][...] + bias_ref[pair]
            if mask_start:
                s = jnp.where(key >= B_REACH - first_row(blk), s, NEG_INF)
            o2 = _window_attend(s, vtpad[pair * LANES:(pair + 1) * LANES, pl.ds(first_row(blk), w)])
            outs.append(o2[0:HEAD_DIM, 0:tq])
            outs.append(o2[HEAD_DIM:, tq:])
            if pair == n_pairs - 1:
                o_ref[blk * tq:(blk + 1) * tq, :] = jnp.concatenate(outs, axis=0).T.astype(BF16)

    pl.when(i * step < B_REACH)(functools.partial(attend, True))
    pl.when(i * step >= B_REACH)(functools.partial(attend, False))


def _band_prompt(qb, kb, vbt, bias_pairs, batch, seq):
    step = BAND_BLOCKS * BAND_TILE
    nq = seq // step
    return pl.pallas_call(
        _band_prompt_kernel,
        out_shape=jax.ShapeDtypeStruct((batch * seq, B_WIDTH), BF16),
        grid=(batch, nq),
        in_specs=[pl.BlockSpec((B_HEADS, step, LANES), lambda b, i: (0, b * nq + i, 0)),
                  pl.BlockSpec((seq, B_WIDTH), lambda b, i: (b, 0)),
                  pl.BlockSpec((B_WIDTH, seq), lambda b, i: (0, b)),
                  _const_spec(bias_pairs.shape)],
        out_specs=pl.BlockSpec((step, B_WIDTH), lambda b, i: (b * nq + i, 0)),
        scratch_shapes=[pltpu.VMEM((seq + B_REACH, B_WIDTH), BF16), pltpu.VMEM((B_WIDTH, seq + B_REACH), BF16)]
                       + [pltpu.VMEM((B_REACH + BAND_TILE, 2 * BAND_TILE), F32)] * 2,
        compiler_params=_params("parallel", "arbitrary"),
        name="band_prompt",
    )(qb, kb, vbt, bias_pairs)


def _swa_prompt_kernel(sink_ref, q_ref, k_ref, vt_ref, o_ref, kpad, vtpad, s0_ref, s1_ref, *, layer):
    i = pl.program_id(1)
    s_refs = (s0_ref, s1_ref)

    @pl.when(i == 0)
    def _():
        _fill_padded(kpad, vtpad, k_ref, vt_ref, C_WINDOW)

    w = C_WINDOW + ATT_TILE
    group = C_HEADS // C_KV_HEADS
    key = lax.broadcasted_iota(jnp.int32, (w, ATT_TILE), 0)
    qry = lax.broadcasted_iota(jnp.int32, (w, ATT_TILE), 1)
    lo = (qry // CHUNK) * CHUNK
    in_window = (key >= lo) & (key < lo + C_WINDOW + CHUNK)

    def score(unit):
        blk, kv = divmod(unit, C_KV_HEADS)
        start = pl.multiple_of((i * SWA_BLOCKS + blk) * ATT_TILE, ATT_TILE)
        q = q_ref[kv * group:(kv + 1) * group, blk * ATT_TILE:(blk + 1) * ATT_TILE, :]
        s_refs[unit % 2][...] = _dot_nt(kpad[pl.ds(start, w), :], q.reshape(group * ATT_TILE, LANES))

    n_units = SWA_BLOCKS * C_KV_HEADS
    score(0)
    for unit in range(n_units):
        if unit + 1 < n_units:
            score(unit + 1)
        blk, kv = divmod(unit, C_KV_HEADS)
        if kv == 0:
            first = (i * SWA_BLOCKS + blk) * ATT_TILE
            start = pl.multiple_of(first, ATT_TILE)
            visible = in_window & (key >= C_WINDOW - first)
            mask = jnp.concatenate([jnp.where(visible, 0.0, NEG_INF)] * group, axis=1)
            outs = []
        sink = jnp.concatenate([jnp.full((1, ATT_TILE), sink_ref[layer, kv * group + g] * LOG2E, F32)
                                for g in range(group)], axis=1)
        og = _window_attend(s_refs[unit % 2][...] + mask, vtpad[kv * HEAD_DIM:(kv + 1) * HEAD_DIM, pl.ds(start, w)],
                            sink)
        outs.extend(og[:, g * ATT_TILE:(g + 1) * ATT_TILE] for g in range(group))
        if kv == C_KV_HEADS - 1:
            o_ref[blk * ATT_TILE:(blk + 1) * ATT_TILE, :] = jnp.concatenate(outs, axis=0).T.astype(BF16)


def _swa_prompt(sinks, layer, q, k, vt, batch, seq):
    step = SWA_BLOCKS * ATT_TILE
    nq = seq // step
    return pl.pallas_call(
        functools.partial(_swa_prompt_kernel, layer=layer),
        out_shape=jax.ShapeDtypeStruct((batch * seq, C_QWIDTH), BF16),
        grid=(batch, nq),
        in_specs=[pl.BlockSpec(memory_space=pltpu.SMEM),
                  pl.BlockSpec((C_HEADS, step, LANES), lambda b, i: (0, b * nq + i, 0)),
                  pl.BlockSpec((seq, C_KWIDTH), lambda b, i: (b, 0)),
                  pl.BlockSpec((C_KWIDTH, seq), lambda b, i: (0, b))],
        out_specs=pl.BlockSpec((step, C_QWIDTH), lambda b, i: (b * nq + i, 0)),
        scratch_shapes=[pltpu.VMEM((seq + C_WINDOW, C_KWIDTH), BF16), pltpu.VMEM((C_KWIDTH, seq + C_WINDOW), BF16)]
                       + [pltpu.VMEM((C_WINDOW + ATT_TILE, C_HEADS // C_KV_HEADS * ATT_TILE), F32)] * 2,
        compiler_params=_params("parallel", "arbitrary"),
        name="swa_prompt",
    )(sinks, q, k, vt)


def _softmax_pv2(s_c, s_n, v_c, v_n, sink=None, v_c_transposed=False):
    m = jnp.maximum(jnp.max(s_c, axis=-1, keepdims=True), jnp.max(s_n, axis=-1, keepdims=True))
    if sink is not None:
        m = jnp.maximum(m, sink)
    p_c = jnp.exp2(s_c - m)
    p_n = jnp.exp2(s_n - m)
    l = jnp.sum(p_c, axis=-1, keepdims=True) + jnp.sum(p_n, axis=-1, keepdims=True)
    if sink is not None:
        l = l + jnp.exp2(sink - m)
    pv_c = _dot_nt(p_c.astype(BF16), v_c) if v_c_transposed else _dot(p_c.astype(BF16), v_c)
    return (pv_c + _dot(p_n.astype(BF16), v_n)) / l


def _even_sample_kernel(qa_ref, ka_ref, va_ref, qb_ref, kb_ref, vb_ref,
                        cckv_ref, ckpe_ref, cbk_ref, cbv_ref, wukv_ref, pick_ref, bias_ref,
                        oa_ref, ob_ref):
    t = qa_ref.shape[0]
    nb = cbk_ref.shape[1]
    kvc = _dot(cckv_ref[...].astype(BF16), wukv_ref[...])
    kpe_t = ckpe_ref[...].astype(BF16)
    scores_a, scores_b = [], []
    for hd in range(A_HEADS):
        sl = slice(hd * LANES, (hd + 1) * LANES)
        q = qa_ref[:, sl]
        q_rope = _dot(q, pick_ref[...]).astype(BF16)
        s_c = _dot_nt(q, kvc[:, sl].astype(BF16)) + _dot(q_rope, kpe_t)
        scores_a.append((s_c, _dot_nt(q, ka_ref[:, sl])))
    for hd in range(B_HEADS):
        pair = slice((hd // 2) * LANES, (hd // 2 + 1) * LANES)
        q = qb_ref[hd]
        scores_b.append((_dot(q, cbk_ref[pair, :].astype(BF16)) + bias_ref[hd, 0:t, 0:nb],
                         _dot_nt(q, kb_ref[:, pair]) + bias_ref[hd, 0:t, nb:nb + t]))
    outs_a, outs_b = [], []
    for hd in range(A_HEADS):
        vsl = slice(hd * A_V, (hd + 1) * A_V)
        v_c = kvc[:, A_HEADS * LANES + hd * A_V:A_HEADS * LANES + (hd + 1) * A_V].astype(BF16)
        outs_a.append(_softmax_pv2(*scores_a[hd], v_c, va_ref[:, vsl]))
    for hd in range(B_HEADS):
        sl = slice(hd * HEAD_DIM, (hd + 1) * HEAD_DIM)
        outs_b.append(_softmax_pv2(*scores_b[hd], cbv_ref[sl, :].astype(BF16), vb_ref[:, sl], v_c_transposed=True))
    oa_ref[...] = jnp.concatenate(outs_a, axis=-1).astype(BF16)
    ob_ref[...] = jnp.concatenate(outs_b, axis=-1).astype(BF16)


def _even_sample(layer, qa, ka, va, qb, kb, vb, c_ckv, c_kpe_t, c_bk_t, c_bv_t, wukv, pick, bias, dec_batch, t):
    past = c_ckv.shape[2]
    nb = c_bk_t.shape[3]

    def new(w):
        return pl.BlockSpec((t, w), lambda b: (b, 0))

    def cache(n, w):
        return pl.BlockSpec((None, None, n, w), lambda b: (layer, b, 0, 0))

    return pl.pallas_call(
        _even_sample_kernel,
        out_shape=[jax.ShapeDtypeStruct((dec_batch * t, B_WIDTH), BF16)] * 2,
        grid=(dec_batch,),
        in_specs=[new(A_HEADS * LANES), new(A_HEADS * LANES), new(A_HEADS * A_V),
                  pl.BlockSpec((B_HEADS, t, LANES), lambda b: (0, b, 0)), new(B_WIDTH), new(B_WIDTH),
                  cache(past, A_KV_LORA), cache(A_ROPE, past), cache(B_WIDTH, nb), cache(B_WIDTH, nb),
                  _const_spec(wukv.shape), _const_spec(pick.shape), _const_spec(bias.shape)],
        out_specs=[new(B_WIDTH), new(B_WIDTH)],
        compiler_params=_params("parallel"),
        name="even_sample",
    )(qa, ka, va, qb, kb, vb, c_ckv, c_kpe_t, c_bk_t, c_bv_t, wukv, pick, bias)


def _odd_sample_kernel(sink_ref, q_ref, k_ref, v_ref, ck_ref, cv_ref, o_ref, *, layer):
    group = C_HEADS // C_KV_HEADS
    scores = []
    for hd in range(C_HEADS):
        ksl = slice((hd // group) * HEAD_DIM, (hd // group + 1) * HEAD_DIM)
        q = q_ref[:, hd * HEAD_DIM:(hd + 1) * HEAD_DIM]
        scores.append((_dot_nt(q, ck_ref[:, ksl].astype(BF16)), _dot_nt(q, k_ref[:, ksl])))
    outs = []
    for hd in range(C_HEADS):
        ksl = slice((hd // group) * HEAD_DIM, (hd // group + 1) * HEAD_DIM)
        outs.append(_softmax_pv2(*scores[hd], cv_ref[:, ksl].astype(BF16), v_ref[:, ksl],
                                 sink_ref[layer, hd] * LOG2E))
    o_ref[...] = jnp.concatenate(outs, axis=-1).astype(BF16)


def _odd_sample(sinks, layer, q, k, v, c_k, c_v, dec_batch, t):
    nc = c_k.shape[2]

    def new(w):
        return pl.BlockSpec((t, w), lambda b: (b, 0))

    def cache():
        return pl.BlockSpec((None, None, nc, C_KWIDTH), lambda b: (layer, b, 0, 0))

    return pl.pallas_call(
        functools.partial(_odd_sample_kernel, layer=layer),
        out_shape=jax.ShapeDtypeStruct((dec_batch * t, C_QWIDTH), BF16),
        grid=(dec_batch,),
        in_specs=[pl.BlockSpec(memory_space=pltpu.SMEM), new(C_QWIDTH), new(C_KWIDTH), new(C_KWIDTH),
                  cache(), cache()],
        out_specs=new(C_QWIDTH),
        compiler_params=_params("parallel"),
        name="odd_sample",
    )(sinks, q, k, v, c_k, c_v)


def _rope_parts(pos, n_rot):
    half = n_rot // 2
    inv = ROPE_THETA ** (-jnp.arange(half, dtype=F32) / half)
    ang = pos.astype(F32)[:, None] * inv[None, :]
    return jnp.cos(ang), jnp.sin(ang)


def _rope_tables(pos, n_rot, pre, width):
    cos, sin = _rope_parts(pos, n_rot)
    n = pos.shape[0]
    post = width - pre - n_rot

    def head(first, second, fill):
        return jnp.concatenate([jnp.full((n, pre), fill, F32), first, second, jnp.full((n, post), fill, F32)], axis=1)

    zero = jnp.zeros_like(sin)
    tabs = (head(cos, cos, 1.0), head(zero, sin, 0.0), head(-sin, zero, 0.0))
    return tuple(jnp.tile(t, (1, LANES // width)) for t in tabs)


def kernel(x_prompt, x_sample, cache_mla_ckv, cache_mla_kpe, cache_band_k, cache_band_v,
           cache_swa_k, cache_swa_v, norm_g, ffn_w_gate, ffn_w_up, ffn_w_down, even_w_in,
           mla_q_norm, mla_w_uq, mla_kv_norm, mla_w_ukv, band_rel_bias, even_w_out,
           odd_w_in, swa_sinks, odd_w_out):
    batch, seq, _ = x_prompt.shape
    dec_batch, t_new, _ = x_sample.shape
    depth = norm_g.shape[0]
    n_even = even_w_in.shape[0]
    n_odd = odd_w_in.shape[0]
    past = cache_mla_ckv.shape[2]
    n_p = batch * seq
    n_s = dec_batch * t_new
    assert seq % TOKEN_TILE == 0 and n_s % TOKEN_TILE == 0 and TOKEN_TILE % t_new == 0
    assert n_p % FFN_TILE == 0 and n_s % FFN_TILE == 0
    assert seq % MLA_Q_TILE == 0 and seq % (BAND_BLOCKS * BAND_TILE) == 0 and seq % (SWA_BLOCKS * ATT_TILE) == 0 and min(B_REACH, seq) == TOKEN_TILE and t_new <= CHUNK
    assert cache_band_k.shape[2] == B_REACH and cache_swa_k.shape[2] == C_WINDOW and past >= B_REACH

    pos_p = jnp.arange(seq, dtype=jnp.int32)
    pos_s = past + (jnp.arange(TOKEN_TILE, dtype=jnp.int32) % t_new)
    mla_tabs = {True: _rope_tables(pos_p, A_ROPE, A_NOPE, LANES), False: _rope_tables(pos_s, A_ROPE, A_NOPE, LANES)}
    swa_tabs = {True: _rope_tables(pos_p, C_ROT, 0, HEAD_DIM), False: _rope_tables(pos_s, C_ROT, 0, HEAD_DIM)}

    wg = ffn_w_gate.astype(BF16)
    wu = ffn_w_up.astype(BF16)
    wd = ffn_w_down.astype(BF16)

    pick = jnp.zeros((LANES, A_ROPE), F32).at[A_NOPE + jnp.arange(A_ROPE), jnp.arange(A_ROPE)].set(1.0).astype(BF16)
    c_kpe_t = cache_mla_kpe.transpose(0, 1, 3, 2)
    c_bk_t = cache_band_k.transpose(0, 1, 3, 4, 2).reshape(n_even, dec_batch, B_WIDTH, B_REACH)
    c_bv_t = cache_band_v.transpose(0, 1, 3, 4, 2).reshape(n_even, dec_batch, B_WIDTH, B_REACH)
    c_sk = cache_swa_k.reshape(n_odd, dec_batch, C_WINDOW, C_KWIDTH)
    c_sv = cache_swa_v.reshape(n_odd, dec_batch, C_WINDOW, C_KWIDTH)

    c0 = A_Q_LORA + A_KV_LORA
    c1 = c0 + A_ROPE
    xs = {True: x_prompt.reshape(n_p, D_MODEL), False: x_sample.reshape(n_s, D_MODEL)}
    even_st = {True: None, False: None}
    odd_st = {True: None, False: None}
    for l in range(depth):
        g = norm_g[l]
        i = l // 2
        if l % 2 == 0:
            w = even_w_in[i].astype(BF16)
            zc = lambda n: jnp.zeros((D_MODEL, n), BF16)
            win = jnp.concatenate([w[:, :c0], zc(A_NOPE), w[:, c0:c1], zc(LANES - A_NOPE - A_ROPE), w[:, c1:]], axis=1)
            wuq = jnp.pad(mla_w_uq[i].reshape(A_Q_LORA, A_HEADS, A_QK),
                          ((0, 0), (0, 0), (0, LANES - A_QK))).reshape(A_Q_LORA, A_HEADS * LANES).astype(BF16)
            wkv = mla_w_ukv[i].reshape(A_KV_LORA, A_HEADS, A_NOPE + A_V)
            wk = jnp.pad(wkv[:, :, :A_NOPE], ((0, 0), (0, 0), (0, LANES - A_NOPE))).reshape(A_KV_LORA, A_HEADS * LANES)
            wv = wkv[:, :, A_NOPE:].reshape(A_KV_LORA, A_HEADS * A_V)
            wukv = jnp.concatenate([wk, wv], axis=1).astype(BF16)
            bias, bias_t = _band_bias(band_rel_bias[i])
            w_band = B_REACH + BAND_TILE
            bias_pairs = bias_t.reshape(B_HEADS // 2, 2, w_band, BAND_TILE).transpose(0, 2, 1, 3).reshape(
                B_HEADS // 2, w_band, 2 * BAND_TILE)
            wo = even_w_out[i].astype(BF16)
        else:
            win = odd_w_in[i].astype(BF16)
            wo = odd_w_out[i].astype(BF16)
        for prompt in (True, False):
            x = _ffn(xs[prompt], g[0:2], wg, wu, wd, l, 0)
            if l % 2 == 0:
                qa, ka, va, qb, kb, vb, *even_st[prompt] = _even_proj(
                    x, g[2:3], win, mla_q_norm[i][None], mla_kv_norm[i][None], wuq, wukv, mla_tabs[prompt],
                    prompt, batch, seq, i, n_even, even_st[prompt])
                if prompt:
                    mixes = [_mla_prompt(qa, ka, va, batch, seq), _band_prompt(qb, kb, vb, bias_pairs, batch, seq)]
                else:
                    mixes = _even_sample(i, qa, ka, va, qb, kb, vb, cache_mla_ckv, c_kpe_t, c_bk_t, c_bv_t,
                                         wukv, pick, bias, dec_batch, t_new)
            else:
                q, k, v, *odd_st[prompt] = _odd_proj(x, g[2:3], win, swa_tabs[prompt], prompt, batch, seq,
                                                     i, n_odd, odd_st[prompt])
                if prompt:
                    mixes = [_swa_prompt(swa_sinks, i, q, k, v, batch, seq)]
                else:
                    mixes = [_odd_sample(swa_sinks, i, q, k, v, c_sk, c_sv, dec_batch, t_new)]
            xs[prompt] = _mix_ffn(x, mixes, wo, g[3:4], g[4:6], wg, wu, wd, l, 1)

    y_prompt = xs[True].reshape(batch, seq, D_MODEL)
    y_sample = xs[False].reshape(dec_batch, t_new, D_MODEL)

    def states(prompt, lead):
        ckv, kpe, kbf, vbf = even_st[prompt]
        kf, vf = odd_st[prompt]
        if prompt:
            kbf, vbf = (a.reshape(n_even, lead[0], B_HEADS, HEAD_DIM, -1).transpose(0, 1, 4, 2, 3) for a in (kbf, vbf))
        return (ckv.reshape(n_even, *lead, A_KV_LORA), kpe.reshape(n_even, *lead, A_ROPE),
                kbf.reshape(n_even, lead[0], -1, B_HEADS, HEAD_DIM), vbf.reshape(n_even, lead[0], -1, B_HEADS, HEAD_DIM),
                kf.reshape(n_odd, lead[0], -1, C_KV_HEADS, HEAD_DIM), vf.reshape(n_odd, lead[0], -1, C_KV_HEADS, HEAD_DIM))

    return (y_prompt, y_sample, *states(True, (batch, seq)), *states(False, (dec_batch, t_new)))
```

```python
import functools
import math

import jax
import jax.numpy as jnp
from jax import lax
from jax.experimental import pallas as pl
from jax.experimental.pallas import tpu as pltpu

F32 = jnp.float32
BF16 = jnp.bfloat16

D_MODEL = 1024
D_FF = 2816
CHUNK = 64
HEAD_DIM = 64
ROPE_THETA = 500000.0
NORM_EPS = 1e-6
NEG_INF = -1e30
A_HEADS = 8
A_Q_LORA = 256
A_KV_LORA = 128
A_NOPE = 64
A_ROPE = 32
A_QK = A_NOPE + A_ROPE
A_V = 64
A_SCALE = A_QK ** -0.5
B_HEADS = 8
B_REACH = 512
B_MAX_REL = 128
C_HEADS = 16
C_KV_HEADS = 2
C_WINDOW = 128
C_ROT = HEAD_DIM // 4
HEAD_SCALE = HEAD_DIM ** -0.5
LOG2E = math.log2(math.e)

LANES = 128
TOKEN_TILE = 512
FFN_TILE = 1024
FFN_PARTS = 2
FFN_ROWS = FFN_TILE // FFN_PARTS
FF_TILE = 256
PROJ_PARTS = 2
PROJ_ROWS = TOKEN_TILE // PROJ_PARTS
ATT_TILE = 128
BAND_TILE = 256
BAND_BLOCKS = 4
MLA_Q_TILE = 512
MLA_K_TILE = 256
BIAS_DIAG_LANES = 1024
SWA_BLOCKS = 8
ONES_ROWS = 16
VMEM_LIMIT = 56 * 1024 * 1024
B_WIDTH = B_HEADS * HEAD_DIM
C_QWIDTH = C_HEADS * HEAD_DIM
C_KWIDTH = C_KV_HEADS * HEAD_DIM
EVEN_CKV = A_Q_LORA
EVEN_KPE = EVEN_CKV + A_KV_LORA
EVEN_QB = EVEN_KPE + LANES
EVEN_KB = EVEN_QB + B_WIDTH
EVEN_VB = EVEN_KB + B_WIDTH


def _params(*sem):
    return pltpu.CompilerParams(dimension_semantics=sem, vmem_limit_bytes=VMEM_LIMIT)


def _const_spec(shape, index=None):
    index = index or (0,) * len(shape)
    return pl.BlockSpec(shape, lambda *_: index, pipeline_mode=pl.Buffered(1))


def _rms(x, g):
    ms = jnp.mean(x * x, axis=-1, keepdims=True)
    return x * lax.rsqrt(ms + NORM_EPS) * g


def _dot(a, b):
    return jnp.dot(a, b, preferred_element_type=F32)


def _dot_nt(a, b):
    return lax.dot_general(a, b, (((1,), (1,)), ((), ())), preferred_element_type=F32)


def _rope_group(x, c, s1, s2, half):
    return x * c + pltpu.roll(x, half, 1) * s1 + pltpu.roll(x, LANES - half, 1) * s2


def _lane_mask(lo):
    lane = lax.broadcasted_iota(jnp.int32, (1, LANES), 1)
    return (lane >= lo) & (lane < lo + HEAD_DIM)


def _swiglu_update(x, g_ref, wg_ref, wu_ref, wd_ref):
    xn = _rms(x, g_ref[0:1, :]).astype(BF16)
    acc = jnp.zeros(x.shape, F32)
    for c in range(D_FF // FF_TILE):
        sl = slice(c * FF_TILE, (c + 1) * FF_TILE)
        a = _dot(xn, wg_ref[:, sl])
        b = _dot(xn, wu_ref[:, sl])
        h = (a * jax.nn.sigmoid(a) * b).astype(BF16)
        acc = acc + _dot(h, wd_ref[sl, :])
    return x + 0.5 * _rms(acc, g_ref[1:2, :])


def _ffn_row_parts():
    return [slice(p * FFN_ROWS, (p + 1) * FFN_ROWS) for p in range(FFN_PARTS)]


def _ffn_kernel(x_ref, g_ref, wg_ref, wu_ref, wd_ref, o_ref):
    for rows in _ffn_row_parts():
        o_ref[rows, :] = _swiglu_update(x_ref[rows, :], g_ref, wg_ref, wu_ref, wd_ref)


def _mix_ffn_kernel(*refs, n_mix):
    x_ref = refs[0]
    mix_refs = refs[1:1 + n_mix]
    wo_ref, gm_ref, g_ref, wg_ref, wu_ref, wd_ref, o_ref = refs[1 + n_mix:]
    ys = []
    for rows in _ffn_row_parts():
        y = None
        row = 0
        for m_ref in mix_refs:
            w = m_ref.shape[1]
            part = _dot(m_ref[rows, :], wo_ref[row:row + w, :])
            y = part if y is None else y + part
            row += w
        ys.append(y)
    for rows, y in zip(_ffn_row_parts(), ys):
        x = x_ref[rows, :] + _rms(y, gm_ref[...])
        o_ref[rows, :] = _swiglu_update(x, g_ref, wg_ref, wu_ref, wd_ref)


def _ffn_weight_specs(layer, which):
    idx = (layer, which, 0, 0)
    return [_const_spec((None, None, D_MODEL, D_FF), idx), _const_spec((None, None, D_MODEL, D_FF), idx),
            _const_spec((None, None, D_FF, D_MODEL), idx)]


def _ffn(x, g, wg, wu, wd, layer, which):
    t = x.shape[0]
    tile = pl.BlockSpec((FFN_TILE, D_MODEL), lambda i: (i, 0))
    return pl.pallas_call(
        _ffn_kernel,
        out_shape=jax.ShapeDtypeStruct((t, D_MODEL), F32),
        grid=(t // FFN_TILE,),
        in_specs=[tile, _const_spec((2, D_MODEL))] + _ffn_weight_specs(layer, which),
        out_specs=tile,
        compiler_params=_params("parallel"),
        name="ffn",
    )(x, g, wg, wu, wd)


def _mix_ffn(x, mixes, wo, gm, g, wg, wu, wd, layer, which):
    t = x.shape[0]
    tile = pl.BlockSpec((FFN_TILE, D_MODEL), lambda i: (i, 0))
    mix_specs = [pl.BlockSpec((FFN_TILE, m.shape[1]), lambda i: (i, 0)) for m in mixes]
    return pl.pallas_call(
        functools.partial(_mix_ffn_kernel, n_mix=len(mixes)),
        out_shape=jax.ShapeDtypeStruct((t, D_MODEL), F32),
        grid=(t // FFN_TILE,),
        in_specs=[tile] + mix_specs + [_const_spec(wo.shape), _const_spec((1, D_MODEL)), _const_spec((2, D_MODEL))]
                 + _ffn_weight_specs(layer, which),
        out_specs=tile,
        compiler_params=_params("parallel"),
        name="mix_ffn",
    )(x, *mixes, wo, gm, g, wg, wu, wd)


def _even_proj_kernel(x_ref, g_ref, win_ref, qn_ref, kvn_ref, wuq_ref, wukv_ref,
                      c_ref, s1_ref, s2_ref,
                      qa_ref, ka_ref, va_ref, qb_ref, kb_ref, vb_ref,
                      ckv_ref, kpe_ref, kbf_ref, vbf_ref, *scratch, prompt, tiles_per_seq):
    half = A_ROPE // 2
    for part in range(PROJ_PARTS):
        rows = slice(part * PROJ_ROWS, (part + 1) * PROJ_ROWS)
        h = _rms(x_ref[rows, :], g_ref[...]).astype(BF16)
        proj = _dot(h, win_ref[...])
        cq = _rms(proj[:, 0:EVEN_CKV], qn_ref[...]).astype(BF16)
        ckv = _rms(proj[:, EVEN_CKV:EVEN_KPE], kvn_ref[...])
        c, s1, s2 = c_ref[rows, :], s1_ref[rows, :], s2_ref[rows, :]
        kpe = _rope_group(proj[:, EVEN_KPE:EVEN_QB], c, s1, s2, half)
        ckv_ref[rows, :] = ckv
        kpe_ref[rows, :] = pltpu.roll(kpe, LANES - A_NOPE, 1)[:, 0:A_ROPE]
        qa = _dot(cq, wuq_ref[...])
        kv = _dot(ckv.astype(BF16), wukv_ref[...])
        for hd in range(A_HEADS):
            sl = slice(hd * LANES, (hd + 1) * LANES)
            qa_ref[rows, sl] = (_rope_group(qa[:, sl], c, s1, s2, half) * (A_SCALE * LOG2E)).astype(BF16)
            ka_ref[rows, sl] = (kv[:, sl] + kpe).astype(BF16)
        for pair in range(B_HEADS // 2):
            grp = proj[:, EVEN_QB + pair * LANES:EVEN_QB + (pair + 1) * LANES] * (HEAD_SCALE * LOG2E)
            for sub in range(2):
                qb_ref[2 * pair + sub, rows, :] = jnp.where(_lane_mask(sub * HEAD_DIM), grp, 0.0).astype(BF16)
        va = kv[:, A_HEADS * LANES:]
        kb = proj[:, EVEN_KB:EVEN_VB]
        vb = proj[:, EVEN_VB:EVEN_VB + B_WIDTH]
        kb_ref[rows, :] = kb.astype(BF16)
        if prompt:
            va_ref[:, rows] = va.T.astype(BF16)
            vb_ref[:, rows] = vb.T.astype(BF16)
            scratch[0][rows, :] = kb
            scratch[1][rows, :] = vb
        else:
            va_ref[rows, :] = va.astype(BF16)
            vb_ref[rows, :] = vb.astype(BF16)
            kbf_ref[rows, :] = kb
            vbf_ref[rows, :] = vb
    if prompt:
        @pl.when(pl.program_id(0) % tiles_per_seq == tiles_per_seq - 1)
        def _():
            kbf_ref[...] = scratch[0][...].T
            vbf_ref[...] = scratch[1][...].T


def _last_tile_index(tiles_per_seq):
    return lambda i: jnp.maximum((i + 1) // tiles_per_seq - 1, 0)


def _skip_aliased(kernel, n_in, n_alias):
    def body(*refs):
        return kernel(*refs[:n_in], *refs[n_in + n_alias:])
    return body


def _layer_slot(n_layers, layer, rows, width, block_rows, row_index=lambda i: i, transposed=False):
    if transposed:
        return (jax.ShapeDtypeStruct((n_layers, rows // block_rows, width, block_rows), F32),
                pl.BlockSpec((None, None, width, block_rows), lambda i: (layer, row_index(i), 0, 0)))
    return (jax.ShapeDtypeStruct((n_layers, rows, width), F32),
            pl.BlockSpec((None, block_rows, width), lambda i: (layer, row_index(i), 0)))


def _alias_args(prev, n_in, first_out):
    if prev is None:
        return [], {}
    return [pl.BlockSpec(memory_space=pl.ANY)] * len(prev), {n_in + k: first_out + k for k in range(len(prev))}


def _even_proj(x, g, win, qn, kvn, wuq, wukv, tabs, prompt, batch, seq, layer, n_layers, prev):
    t = x.shape[0]
    tiles_per_seq = seq // TOKEN_TILE if prompt else 1

    def tile(w):
        return pl.BlockSpec((TOKEN_TILE, w), lambda i: (i, 0))

    def tile_t(w):
        return pl.BlockSpec((w, TOKEN_TILE), lambda i: (0, i))

    def sds(shape, dtype):
        return jax.ShapeDtypeStruct(shape, dtype)

    wa = A_HEADS * LANES
    if prompt:
        v_shapes = [sds((A_HEADS * A_V, t), BF16), sds((B_WIDTH, t), BF16)]
        v_specs = [tile_t(A_HEADS * A_V), tile_t(B_WIDTH)]
        keep = _layer_slot(n_layers, layer, batch * TOKEN_TILE, B_WIDTH, TOKEN_TILE, _last_tile_index(tiles_per_seq),
                           transposed=True)
        tab_idx = lambda i: (i % tiles_per_seq, 0)
    else:
        v_shapes = [sds((t, A_HEADS * A_V), BF16), sds((t, B_WIDTH), BF16)]
        v_specs = [tile(A_HEADS * A_V), tile(B_WIDTH)]
        keep = _layer_slot(n_layers, layer, t, B_WIDTH, TOKEN_TILE)
        tab_idx = lambda i: (0, 0)
    ckv = _layer_slot(n_layers, layer, t, A_KV_LORA, TOKEN_TILE)
    kpe = _layer_slot(n_layers, layer, t, A_ROPE, TOKEN_TILE)
    qb_spec = pl.BlockSpec((B_HEADS, TOKEN_TILE, LANES), lambda i: (0, i, 0))
    out_shape = [sds((t, wa), BF16), sds((t, wa), BF16), v_shapes[0], sds((B_HEADS, t, LANES), BF16),
                 sds((t, B_WIDTH), BF16), v_shapes[1], ckv[0], kpe[0], keep[0], keep[0]]
    out_specs = [tile(wa), tile(wa), v_specs[0], qb_spec, tile(B_WIDTH), v_specs[1], ckv[1], kpe[1], keep[1], keep[1]]
    in_specs = ([tile(D_MODEL), _const_spec((1, D_MODEL)), _const_spec(win.shape),
                 _const_spec((1, A_Q_LORA)), _const_spec((1, A_KV_LORA)),
                 _const_spec(wuq.shape), _const_spec(wukv.shape)]
                + [pl.BlockSpec((TOKEN_TILE, LANES), tab_idx)] * 3)
    alias_specs, aliases = _alias_args(prev, len(in_specs), 6)
    kernel_fn = functools.partial(_even_proj_kernel, prompt=prompt, tiles_per_seq=tiles_per_seq)
    return pl.pallas_call(
        _skip_aliased(kernel_fn, len(in_specs), len(alias_specs)),
        out_shape=out_shape,
        grid=(t // TOKEN_TILE,),
        in_specs=in_specs + alias_specs,
        out_specs=out_specs,
        scratch_shapes=[pltpu.VMEM((TOKEN_TILE, B_WIDTH), F32)] * 2 if prompt else [],
        input_output_aliases=aliases,
        compiler_params=_params("arbitrary"),
        name="even_proj",
    )(x, g, win, qn, kvn, wuq, wukv, *tabs, *(prev or ()))


def _odd_proj_kernel(x_ref, g_ref, win_ref, c_ref, s1_ref, s2_ref,
                     q_ref, k_ref, v_ref, kf_ref, vf_ref, *scratch, prompt, tiles_per_seq):
    half = C_ROT // 2
    group = C_HEADS // C_KV_HEADS
    for part in range(PROJ_PARTS):
        rows = slice(part * PROJ_ROWS, (part + 1) * PROJ_ROWS)
        h = _rms(x_ref[rows, :], g_ref[...]).astype(BF16)
        proj = _dot(h, win_ref[...])
        c, s1, s2 = c_ref[rows, :], s1_ref[rows, :], s2_ref[rows, :]
        k = _rope_group(proj[:, C_QWIDTH:C_QWIDTH + LANES], c, s1, s2, half)
        v = proj[:, C_QWIDTH + LANES:C_QWIDTH + 2 * LANES]
        k_ref[rows, :] = k.astype(BF16)
        if prompt:
            for pair in range(C_HEADS // 2):
                grp = _rope_group(proj[:, pair * LANES:(pair + 1) * LANES], c, s1, s2, half) * (HEAD_SCALE * LOG2E)
                swapped = pltpu.roll(grp, HEAD_DIM, 1)
                for sub in range(2):
                    hd = 2 * pair + sub
                    kv = hd // group
                    src = grp if sub == kv else swapped
                    q_ref[hd, rows, :] = jnp.where(_lane_mask(kv * HEAD_DIM), src, 0.0).astype(BF16)
            v_ref[:, rows] = v.T.astype(BF16)
            if part == PROJ_PARTS - 1:
                scratch[0][...] = k[PROJ_ROWS - C_WINDOW:, :]
                scratch[1][...] = v[PROJ_ROWS - C_WINDOW:, :]
        else:
            for grp in range(C_QWIDTH // LANES):
                sl = slice(grp * LANES, (grp + 1) * LANES)
                q_ref[rows, sl] = (_rope_group(proj[:, sl], c, s1, s2, half) * (HEAD_SCALE * LOG2E)).astype(BF16)
            v_ref[rows, :] = v.astype(BF16)
            kf_ref[rows, :] = k
            vf_ref[rows, :] = v
    if prompt:
        @pl.when(pl.program_id(0) % tiles_per_seq == tiles_per_seq - 1)
        def _():
            kf_ref[...] = scratch[0][...]
            vf_ref[...] = scratch[1][...]


def _odd_proj(x, g, win, tabs, prompt, batch, seq, layer, n_layers, prev):
    t = x.shape[0]
    tiles_per_seq = seq // TOKEN_TILE if prompt else 1

    def tile(w):
        return pl.BlockSpec((TOKEN_TILE, w), lambda i: (i, 0))

    def sds(shape, dtype):
        return jax.ShapeDtypeStruct(shape, dtype)

    if prompt:
        q_shape = sds((C_HEADS, t, LANES), BF16)
        q_spec = pl.BlockSpec((C_HEADS, TOKEN_TILE, LANES), lambda i: (0, i, 0))
        v_shape, v_spec = sds((LANES, t), BF16), pl.BlockSpec((LANES, TOKEN_TILE), lambda i: (0, i))
        keep = _layer_slot(n_layers, layer, batch * C_WINDOW, LANES, C_WINDOW, _last_tile_index(tiles_per_seq))
        tab_idx = lambda i: (i % tiles_per_seq, 0)
    else:
        q_shape, q_spec = sds((t, C_QWIDTH), BF16), tile(C_QWIDTH)
        v_shape, v_spec = sds((t, LANES), BF16), tile(LANES)
        keep = _layer_slot(n_layers, layer, t, LANES, TOKEN_TILE)
        tab_idx = lambda i: (0, 0)
    in_specs = ([tile(D_MODEL), _const_spec((1, D_MODEL)), _const_spec(win.shape)]
                + [pl.BlockSpec((TOKEN_TILE, LANES), tab_idx)] * 3)
    alias_specs, aliases = _alias_args(prev, len(in_specs), 3)
    kernel_fn = functools.partial(_odd_proj_kernel, prompt=prompt, tiles_per_seq=tiles_per_seq)
    return pl.pallas_call(
        _skip_aliased(kernel_fn, len(in_specs), len(alias_specs)),
        out_shape=[q_shape, sds((t, LANES), BF16), v_shape, keep[0], keep[0]],
        grid=(t // TOKEN_TILE,),
        in_specs=in_specs + alias_specs,
        out_specs=[q_spec, tile(LANES), v_spec, keep[1], keep[1]],
        scratch_shapes=[pltpu.VMEM((C_WINDOW, LANES), F32)] * 2 if prompt else [],
        input_output_aliases=aliases,
        compiler_params=_params("arbitrary"),
        name="odd_proj",
    )(x, g, win, *tabs, *(prev or ()))


def _band_bias_kernel(tab_ref, o_ref, ot_ref):
    hd = pl.program_id(0)
    w = B_REACH + BAND_TILE
    row = lax.broadcasted_iota(jnp.int32, (BAND_TILE, w), 0)
    col = lax.broadcasted_iota(jnp.int32, (BAND_TILE, w), 1)
    lo = (row // CHUNK) * CHUNK
    visible = (col >= lo) & (col < lo + B_REACH + CHUNK)
    n = BIAS_DIAG_LANES
    e = lax.broadcasted_iota(jnp.int32, (1, n), 1)
    rel = jnp.clip(B_REACH + BAND_TILE - 1 - e, -B_MAX_REL, B_MAX_REL) + B_MAX_REL

    def body(r, acc):
        return jnp.where(rel == r, tab_ref[hd, r], acc)

    diag = lax.fori_loop(0, 2 * B_MAX_REL + 1, body, jnp.zeros((1, n), F32))
    shifted = pltpu.roll(jnp.broadcast_to(diag, (BAND_TILE, n)), n - (BAND_TILE - 1), 1, stride=1, stride_axis=0)
    bias = jnp.where(visible, shifted[:, 0:w] * LOG2E, NEG_INF)
    o_ref[...] = bias
    ot_ref[...] = bias.T


def _band_bias(table):
    w = B_REACH + BAND_TILE
    return pl.pallas_call(
        _band_bias_kernel,
        out_shape=[jax.ShapeDtypeStruct((B_HEADS, BAND_TILE, w), F32),
                   jax.ShapeDtypeStruct((B_HEADS // 2, w, 2 * BAND_TILE), F32)],
        grid=(B_HEADS,),
        in_specs=[pl.BlockSpec(memory_space=pltpu.SMEM)],
        out_specs=[pl.BlockSpec((None, BAND_TILE, w), lambda h: (h, 0, 0)),
                   pl.BlockSpec((None, w, BAND_TILE), lambda h: (h // 2, 0, h % 2))],
        compiler_params=_params("arbitrary"),
        name="band_bias",
    )(table)


def _ones_rows(n):
    return jnp.ones((ONES_ROWS, n), BF16)


def _mla_prompt_kernel(q_ref, k_ref, vt_ref, o_ref, m_ref, l_ref, acc_ref, s0_ref, s1_ref, smax0_ref, smax1_ref):
    i = pl.program_id(1)
    tq, tk = MLA_Q_TILE, MLA_K_TILE
    m_ref[...] = jnp.full(m_ref.shape, NEG_INF, F32)
    l_ref[...] = jnp.zeros(l_ref.shape, F32)
    acc_ref[...] = jnp.zeros(acc_ref.shape, F32)
    per_q = tq // tk
    assert per_q == 2
    s_refs = (s0_ref, s1_ref)
    smax_refs = (smax0_ref, smax1_ref)

    def live_queries(diag):
        return slice(0 if diag is None else diag * tk, tq)

    def score(j, slot, hd, diag=None):
        ks = pl.multiple_of(j * tk, tk)
        sl = slice(hd * LANES, (hd + 1) * LANES)
        cols = live_queries(diag)
        s = _dot_nt(k_ref[pl.ds(ks, tk), sl], q_ref[cols, sl])
        s_refs[slot][hd, :, cols] = s
        if diag is None:
            smax_refs[slot][hd] = jnp.max(s, axis=0, keepdims=True)

    def block(j, slot, diag, next_diag, stage_next=True):
        ks = pl.multiple_of(j * tk, tk)
        cols = live_queries(diag)
        m_all = [m_ref[hd, :, cols] for hd in range(A_HEADS)]
        l_all = [l_ref[hd, :, cols] for hd in range(A_HEADS)]
        acc_all = [acc_ref[hd, :, cols] for hd in range(A_HEADS)]
        ones = _ones_rows(tk)
        if diag is not None:
            shape = (tk, tq - cols.start)
            key_chunk = (diag * tk + lax.broadcasted_iota(jnp.int32, shape, 0)) // CHUNK
            qry_chunk = (cols.start + lax.broadcasted_iota(jnp.int32, shape, 1)) // CHUNK
            visible = key_chunk <= qry_chunk
        new = []
        for hd in range(A_HEADS):
            if stage_next:
                score(j + 1, 1 - slot, hd, next_diag)
            s = s_refs[slot][hd, :, cols]
            if diag is None:
                s_max = smax_refs[slot][hd]
            else:
                s = jnp.where(visible, s, NEG_INF)
                s_max = jnp.max(s, axis=0, keepdims=True)
            m_new = jnp.maximum(m_all[hd], s_max)
            alpha = jnp.exp2(m_all[hd] - m_new)
            p = jnp.exp2(s - m_new).astype(BF16)
            vt = jnp.concatenate([vt_ref[hd * A_V:(hd + 1) * A_V, pl.ds(ks, tk)], ones], axis=0)
            pv = _dot(vt, p)
            new.append((m_new, alpha * l_all[hd] + pv[A_V:A_V + 1], alpha * acc_all[hd] + pv[0:A_V]))
        for hd, (m_new, l_new, acc_new) in enumerate(new):
            m_ref[hd, :, cols] = m_new
            l_ref[hd, :, cols] = l_new
            acc_ref[hd, :, cols] = acc_new

    for hd in range(A_HEADS):
        score(0, 0, hd)

    def body(pair, carry):
        for d in range(per_q):
            block(pair * per_q + d, d, None, None)
        return carry

    lax.fori_loop(0, i, body, 0)
    for d in range(per_q):
        block(i * per_q + d, d, d, d + 1, stage_next=d + 1 < per_q)
    outs = [acc_ref[hd] / l_ref[hd] for hd in range(A_HEADS)]
    o_ref[...] = jnp.concatenate(outs, axis=0).T.astype(BF16)


def _mla_prompt(qa, ka, vat, batch, seq):
    nq = seq // MLA_Q_TILE
    wa = A_HEADS * LANES
    wv = A_HEADS * A_V
    return pl.pallas_call(
        _mla_prompt_kernel,
        out_shape=jax.ShapeDtypeStruct((batch * seq, wv), BF16),
        grid=(batch, nq),
        in_specs=[pl.BlockSpec((MLA_Q_TILE, wa), lambda b, i: (b * nq + i, 0)),
                  pl.BlockSpec((seq, wa), lambda b, i: (b, 0)),
                  pl.BlockSpec((wv, seq), lambda b, i: (0, b))],
        out_specs=pl.BlockSpec((MLA_Q_TILE, wv), lambda b, i: (b * nq + i, 0)),
        scratch_shapes=[pltpu.VMEM((A_HEADS, 1, MLA_Q_TILE), F32), pltpu.VMEM((A_HEADS, 1, MLA_Q_TILE), F32),
                        pltpu.VMEM((A_HEADS, A_V, MLA_Q_TILE), F32),
                        pltpu.VMEM((A_HEADS, MLA_K_TILE, MLA_Q_TILE), F32),
                        pltpu.VMEM((A_HEADS, MLA_K_TILE, MLA_Q_TILE), F32),
                        pltpu.VMEM((A_HEADS, 1, MLA_Q_TILE), F32), pltpu.VMEM((A_HEADS, 1, MLA_Q_TILE), F32)],
        compiler_params=_params("parallel", "arbitrary"),
        name="mla_prompt",
    )(qa, ka, vat)


def _fill_padded(kpad, vtpad, k_ref, vt_ref, reach):
    kpad[0:reach, :] = jnp.zeros((reach, kpad.shape[1]), kpad.dtype)
    kpad[reach:, :] = k_ref[...]
    vtpad[:, 0:reach] = jnp.zeros((vtpad.shape[0], reach), vtpad.dtype)
    vtpad[:, reach:] = vt_ref[...]


def _window_attend(s, vt, sink=None):
    m = jnp.max(s, axis=0, keepdims=True)
    if sink is not None:
        m = jnp.maximum(m, sink)
    p = jnp.exp2(s - m).astype(BF16)
    d = vt.shape[0]
    pv = _dot(jnp.concatenate([vt, _ones_rows(vt.shape[1])], axis=0), p)
    l = pv[d:d + 1]
    if sink is not None:
        l = l + jnp.exp2(sink - m)
    return pv[0:d] * (1.0 / l)


def _band_prompt_kernel(q_ref, k_ref, vt_ref, bias_ref, o_ref, kpad, vtpad, s0_ref, s1_ref):
    i = pl.program_id(1)
    tq = BAND_TILE
    s_refs = (s0_ref, s1_ref)

    @pl.when(i == 0)
    def _():
        _fill_padded(kpad, vtpad, k_ref, vt_ref, B_REACH)

    w = B_REACH + tq
    n_pairs = B_HEADS // 2
    step = BAND_BLOCKS * tq

    def attend(mask_start):
        key = lax.broadcasted_iota(jnp.int32, (w, 2 * tq), 0)

        def first_row(blk):
            return pl.multiple_of((i * BAND_BLOCKS + blk) * tq, tq)

        def score(unit):
            blk, pair = divmod(unit, n_pairs)
            q2 = q_ref[2 * pair:2 * pair + 2, blk * tq:(blk + 1) * tq, :].reshape(2 * tq, LANES)
            s_refs[unit % 2][...] = _dot_nt(kpad[pl.ds(first_row(blk), w), pair * LANES:(pair + 1) * LANES], q2)

        n_units = BAND_BLOCKS * n_pairs
        score(0)
        for unit in range(n_units):
            if unit + 1 < n_units:
                score(unit + 1)
            blk, pair = divmod(unit, n_pairs)
            if pair == 0:
                outs = []
            s = s_refs[unit % 2][...] + bias_ref[pair]
            if mask_start:
                s = jnp.where(key >= B_REACH - first_row(blk), s, NEG_INF)
            o2 = _window_attend(s, vtpad[pair * LANES:(pair + 1) * LANES, pl.ds(first_row(blk), w)])
            outs.append(o2[0:HEAD_DIM, 0:tq])
            outs.append(o2[HEAD_DIM:, tq:])
            if pair == n_pairs - 1:
                o_ref[blk * tq:(blk + 1) * tq, :] = jnp.concatenate(outs, axis=0).T.astype(BF16)

    pl.when(i * step < B_REACH)(functools.partial(attend, True))
    pl.when(i * step >= B_REACH)(functools.partial(attend, False))


def _band_prompt(qb, kb, vbt, bias_pairs, batch, seq):
    step = BAND_BLOCKS * BAND_TILE
    nq = seq // step
    return pl.pallas_call(
        _band_prompt_kernel,
        out_shape=jax.ShapeDtypeStruct((batch * seq, B_WIDTH), BF16),
        grid=(batch, nq),
        in_specs=[pl.BlockSpec((B_HEADS, step, LANES), lambda b, i: (0, b * nq + i, 0)),
                  pl.BlockSpec((seq, B_WIDTH), lambda b, i: (b, 0)),
                  pl.BlockSpec((B_WIDTH, seq), lambda b, i: (0, b)),
                  _const_spec(bias_pairs.shape)],
        out_specs=pl.BlockSpec((step, B_WIDTH), lambda b, i: (b * nq + i, 0)),
        scratch_shapes=[pltpu.VMEM((seq + B_REACH, B_WIDTH), BF16), pltpu.VMEM((B_WIDTH, seq + B_REACH), BF16)]
                       + [pltpu.VMEM((B_REACH + BAND_TILE, 2 * BAND_TILE), F32)] * 2,
        compiler_params=_params("parallel", "arbitrary"),
        name="band_prompt",
    )(qb, kb, vbt, bias_pairs)


def _swa_prompt_kernel(sink_ref, q_ref, k_ref, vt_ref, o_ref, kpad, vtpad, s0_ref, s1_ref, *, layer):
    i = pl.program_id(1)
    s_refs = (s0_ref, s1_ref)

    @pl.when(i == 0)
    def _():
        _fill_padded(kpad, vtpad, k_ref, vt_ref, C_WINDOW)

    w = C_WINDOW + ATT_TILE
    group = C_HEADS // C_KV_HEADS
    key = lax.broadcasted_iota(jnp.int32, (w, ATT_TILE), 0)
    qry = lax.broadcasted_iota(jnp.int32, (w, ATT_TILE), 1)
    lo = (qry // CHUNK) * CHUNK
    in_window = (key >= lo) & (key < lo + C_WINDOW + CHUNK)

    def score(unit):
        blk, kv = divmod(unit, C_KV_HEADS)
        start = pl.multiple_of((i * SWA_BLOCKS + blk) * ATT_TILE, ATT_TILE)
        q = q_ref[kv * group:(kv + 1) * group, blk * ATT_TILE:(blk + 1) * ATT_TILE, :]
        s_refs[unit % 2][...] = _dot_nt(kpad[pl.ds(start, w), :], q.reshape(group * ATT_TILE, LANES))

    n_units = SWA_BLOCKS * C_KV_HEADS
    score(0)
    for unit in range(n_units):
        if unit + 1 < n_units:
            score(unit + 1)
        blk, kv = divmod(unit, C_KV_HEADS)
        if kv == 0:
            first = (i * SWA_BLOCKS + blk) * ATT_TILE
            start = pl.multiple_of(first, ATT_TILE)
            visible = in_window & (key >= C_WINDOW - first)
            mask = jnp.concatenate([jnp.where(visible, 0.0, NEG_INF)] * group, axis=1)
            outs = []
        sink = jnp.concatenate([jnp.full((1, ATT_TILE), sink_ref[layer, kv * group + g] * LOG2E, F32)
                                for g in range(group)], axis=1)
        og = _window_attend(s_refs[unit % 2][...] + mask, vtpad[kv * HEAD_DIM:(kv + 1) * HEAD_DIM, pl.ds(start, w)],
                            sink)
        outs.extend(og[:, g * ATT_TILE:(g + 1) * ATT_TILE] for g in range(group))
        if kv == C_KV_HEADS - 1:
            o_ref[blk * ATT_TILE:(blk + 1) * ATT_TILE, :] = jnp.concatenate(outs, axis=0).T.astype(BF16)


def _swa_prompt(sinks, layer, q, k, vt, batch, seq):
    step = SWA_BLOCKS * ATT_TILE
    nq = seq // step
    return pl.pallas_call(
        functools.partial(_swa_prompt_kernel, layer=layer),
        out_shape=jax.ShapeDtypeStruct((batch * seq, C_QWIDTH), BF16),
        grid=(batch, nq),
        in_specs=[pl.BlockSpec(memory_space=pltpu.SMEM),
                  pl.BlockSpec((C_HEADS, step, LANES), lambda b, i: (0, b * nq + i, 0)),
                  pl.BlockSpec((seq, C_KWIDTH), lambda b, i: (b, 0)),
                  pl.BlockSpec((C_KWIDTH, seq), lambda b, i: (0, b))],
        out_specs=pl.BlockSpec((step, C_QWIDTH), lambda b, i: (b * nq + i, 0)),
        scratch_shapes=[pltpu.VMEM((seq + C_WINDOW, C_KWIDTH), BF16), pltpu.VMEM((C_KWIDTH, seq + C_WINDOW), BF16)]
                       + [pltpu.VMEM((C_WINDOW + ATT_TILE, C_HEADS // C_KV_HEADS * ATT_TILE), F32)] * 2,
        compiler_params=_params("parallel", "arbitrary"),
        name="swa_prompt",
    )(sinks, q, k, vt)


def _softmax_pv2(s_c, s_n, v_c, v_n, sink=None, v_c_transposed=False):
    m = jnp.maximum(jnp.max(s_c, axis=-1, keepdims=True), jnp.max(s_n, axis=-1, keepdims=True))
    if sink is not None:
        m = jnp.maximum(m, sink)
    p_c = jnp.exp2(s_c - m)
    p_n = jnp.exp2(s_n - m)
    l = jnp.sum(p_c, axis=-1, keepdims=True) + jnp.sum(p_n, axis=-1, keepdims=True)
    if sink is not None:
        l = l + jnp.exp2(sink - m)
    pv_c = _dot_nt(p_c.astype(BF16), v_c) if v_c_transposed else _dot(p_c.astype(BF16), v_c)
    return (pv_c + _dot(p_n.astype(BF16), v_n)) / l


def _even_sample_kernel(qa_ref, ka_ref, va_ref, qb_ref, kb_ref, vb_ref,
                        cckv_ref, ckpe_ref, cbk_ref, cbv_ref, wukv_ref, pick_ref, bias_ref,
                        oa_ref, ob_ref):
    t = qa_ref.shape[0]
    nb = cbk_ref.shape[1]
    kvc = _dot(cckv_ref[...].astype(BF16), wukv_ref[...])
    kpe_t = ckpe_ref[...].astype(BF16)
    scores_a, scores_b = [], []
    for hd in range(A_HEADS):
        sl = slice(hd * LANES, (hd + 1) * LANES)
        q = qa_ref[:, sl]
        q_rope = _dot(q, pick_ref[...]).astype(BF16)
        s_c = _dot_nt(q, kvc[:, sl].astype(BF16)) + _dot(q_rope, kpe_t)
        scores_a.append((s_c, _dot_nt(q, ka_ref[:, sl])))
    for hd in range(B_HEADS):
        pair = slice((hd // 2) * LANES, (hd // 2 + 1) * LANES)
        q = qb_ref[hd]
        scores_b.append((_dot(q, cbk_ref[pair, :].astype(BF16)) + bias_ref[hd, 0:t, 0:nb],
                         _dot_nt(q, kb_ref[:, pair]) + bias_ref[hd, 0:t, nb:nb + t]))
    outs_a, outs_b = [], []
    for hd in range(A_HEADS):
        vsl = slice(hd * A_V, (hd + 1) * A_V)
        v_c = kvc[:, A_HEADS * LANES + hd * A_V:A_HEADS * LANES + (hd + 1) * A_V].astype(BF16)
        outs_a.append(_softmax_pv2(*scores_a[hd], v_c, va_ref[:, vsl]))
    for hd in range(B_HEADS):
        sl = slice(hd * HEAD_DIM, (hd + 1) * HEAD_DIM)
        outs_b.append(_softmax_pv2(*scores_b[hd], cbv_ref[sl, :].astype(BF16), vb_ref[:, sl], v_c_transposed=True))
    oa_ref[...] = jnp.concatenate(outs_a, axis=-1).astype(BF16)
    ob_ref[...] = jnp.concatenate(outs_b, axis=-1).astype(BF16)


def _even_sample(layer, qa, ka, va, qb, kb, vb, c_ckv, c_kpe_t, c_bk_t, c_bv_t, wukv, pick, bias, dec_batch, t):
    past = c_ckv.shape[2]
    nb = c_bk_t.shape[3]

    def new(w):
        return pl.BlockSpec((t, w), lambda b: (b, 0))

    def cache(n, w):
        return pl.BlockSpec((None, None, n, w), lambda b: (layer, b, 0, 0))

    return pl.pallas_call(
        _even_sample_kernel,
        out_shape=[jax.ShapeDtypeStruct((dec_batch * t, B_WIDTH), BF16)] * 2,
        grid=(dec_batch,),
        in_specs=[new(A_HEADS * LANES), new(A_HEADS * LANES), new(A_HEADS * A_V),
                  pl.BlockSpec((B_HEADS, t, LANES), lambda b: (0, b, 0)), new(B_WIDTH), new(B_WIDTH),
                  cache(past, A_KV_LORA), cache(A_ROPE, past), cache(B_WIDTH, nb), cache(B_WIDTH, nb),
                  _const_spec(wukv.shape), _const_spec(pick.shape), _const_spec(bias.shape)],
        out_specs=[new(B_WIDTH), new(B_WIDTH)],
        compiler_params=_params("parallel"),
        name="even_sample",
    )(qa, ka, va, qb, kb, vb, c_ckv, c_kpe_t, c_bk_t, c_bv_t, wukv, pick, bias)


def _odd_sample_kernel(sink_ref, q_ref, k_ref, v_ref, ck_ref, cv_ref, o_ref, *, layer):
    group = C_HEADS // C_KV_HEADS
    scores = []
    for hd in range(C_HEADS):
        ksl = slice((hd // group) * HEAD_DIM, (hd // group + 1) * HEAD_DIM)
        q = q_ref[:, hd * HEAD_DIM:(hd + 1) * HEAD_DIM]
        scores.append((_dot_nt(q, ck_ref[:, ksl].astype(BF16)), _dot_nt(q, k_ref[:, ksl])))
    outs = []
    for hd in range(C_HEADS):
        ksl = slice((hd // group) * HEAD_DIM, (hd // group + 1) * HEAD_DIM)
        outs.append(_softmax_pv2(*scores[hd], cv_ref[:, ksl].astype(BF16), v_ref[:, ksl],
                                 sink_ref[layer, hd] * LOG2E))
    o_ref[...] = jnp.concatenate(outs, axis=-1).astype(BF16)


def _odd_sample(sinks, layer, q, k, v, c_k, c_v, dec_batch, t):
    nc = c_k.shape[2]

    def new(w):
        return pl.BlockSpec((t, w), lambda b: (b, 0))

    def cache():
        return pl.BlockSpec((None, None, nc, C_KWIDTH), lambda b: (layer, b, 0, 0))

    return pl.pallas_call(
        functools.partial(_odd_sample_kernel, layer=layer),
        out_shape=jax.ShapeDtypeStruct((dec_batch * t, C_QWIDTH), BF16),
        grid=(dec_batch,),
        in_specs=[pl.BlockSpec(memory_space=pltpu.SMEM), new(C_QWIDTH), new(C_KWIDTH), new(C_KWIDTH),
                  cache(), cache()],
        out_specs=new(C_QWIDTH),
        compiler_params=_params("parallel"),
        name="odd_sample",
    )(sinks, q, k, v, c_k, c_v)


def _rope_parts(pos, n_rot):
    half = n_rot // 2
    inv = ROPE_THETA ** (-jnp.arange(half, dtype=F32) / half)
    ang = pos.astype(F32)[:, None] * inv[None, :]
    return jnp.cos(ang), jnp.sin(ang)


def _rope_tables(pos, n_rot, pre, width):
    cos, sin = _rope_parts(pos, n_rot)
    n = pos.shape[0]
    post = width - pre - n_rot

    def head(first, second, fill):
        return jnp.concatenate([jnp.full((n, pre), fill, F32), first, second, jnp.full((n, post), fill, F32)], axis=1)

    zero = jnp.zeros_like(sin)
    tabs = (head(cos, cos, 1.0), head(zero, sin, 0.0), head(-sin, zero, 0.0))
    return tuple(jnp.tile(t, (1, LANES // width)) for t in tabs)


def kernel(x_prompt, x_sample, cache_mla_ckv, cache_mla_kpe, cache_band_k, cache_band_v,
           cache_swa_k, cache_swa_v, norm_g, ffn_w_gate, ffn_w_up, ffn_w_down, even_w_in,
           mla_q_norm, mla_w_uq, mla_kv_norm, mla_w_ukv, band_rel_bias, even_w_out,
           odd_w_in, swa_sinks, odd_w_out):
    batch, seq, _ = x_prompt.shape
    dec_batch, t_new, _ = x_sample.shape
    depth = norm_g.shape[0]
    n_even = even_w_in.shape[0]
    n_odd = odd_w_in.shape[0]
    past = cache_mla_ckv.shape[2]
    n_p = batch * seq
    n_s = dec_batch * t_new
    assert seq % TOKEN_TILE == 0 and n_s % TOKEN_TILE == 0 and TOKEN_TILE % t_new == 0
    assert n_p % FFN_TILE == 0 and n_s % FFN_TILE == 0
    assert seq % MLA_Q_TILE == 0 and seq % (BAND_BLOCKS * BAND_TILE) == 0 and seq % (SWA_BLOCKS * ATT_TILE) == 0 and min(B_REACH, seq) == TOKEN_TILE and t_new <= CHUNK
    assert cache_band_k.shape[2] == B_REACH and cache_swa_k.shape[2] == C_WINDOW and past >= B_REACH

    pos_p = jnp.arange(seq, dtype=jnp.int32)
    pos_s = past + (jnp.arange(TOKEN_TILE, dtype=jnp.int32) % t_new)
    mla_tabs = {True: _rope_tables(pos_p, A_ROPE, A_NOPE, LANES), False: _rope_tables(pos_s, A_ROPE, A_NOPE, LANES)}
    swa_tabs = {True: _rope_tables(pos_p, C_ROT, 0, HEAD_DIM), False: _rope_tables(pos_s, C_ROT, 0, HEAD_DIM)}

    wg = ffn_w_gate.astype(BF16)
    wu = ffn_w_up.astype(BF16)
    wd = ffn_w_down.astype(BF16)

    pick = jnp.zeros((LANES, A_ROPE), F32).at[A_NOPE + jnp.arange(A_ROPE), jnp.arange(A_ROPE)].set(1.0).astype(BF16)
    c_kpe_t = cache_mla_kpe.transpose(0, 1, 3, 2)
    c_bk_t = cache_band_k.transpose(0, 1, 3, 4, 2).reshape(n_even, dec_batch, B_WIDTH, B_REACH)
    c_bv_t = cache_band_v.transpose(0, 1, 3, 4, 2).reshape(n_even, dec_batch, B_WIDTH, B_REACH)
    c_sk = cache_swa_k.reshape(n_odd, dec_batch, C_WINDOW, C_KWIDTH)
    c_sv = cache_swa_v.reshape(n_odd, dec_batch, C_WINDOW, C_KWIDTH)

    c0 = A_Q_LORA + A_KV_LORA
    c1 = c0 + A_ROPE
    xs = {True: x_prompt.reshape(n_p, D_MODEL), False: x_sample.reshape(n_s, D_MODEL)}
    even_st = {True: None, False: None}
    odd_st = {True: None, False: None}
    for l in range(depth):
        g = norm_g[l]
        i = l // 2
        if l % 2 == 0:
            w = even_w_in[i].astype(BF16)
            zc = lambda n: jnp.zeros((D_MODEL, n), BF16)
            win = jnp.concatenate([w[:, :c0], zc(A_NOPE), w[:, c0:c1], zc(LANES - A_NOPE - A_ROPE), w[:, c1:]], axis=1)
            wuq = jnp.pad(mla_w_uq[i].reshape(A_Q_LORA, A_HEADS, A_QK),
                          ((0, 0), (0, 0), (0, LANES - A_QK))).reshape(A_Q_LORA, A_HEADS * LANES).astype(BF16)
            wkv = mla_w_ukv[i].reshape(A_KV_LORA, A_HEADS, A_NOPE + A_V)
            wk = jnp.pad(wkv[:, :, :A_NOPE], ((0, 0), (0, 0), (0, LANES - A_NOPE))).reshape(A_KV_LORA, A_HEADS * LANES)
            wv = wkv[:, :, A_NOPE:].reshape(A_KV_LORA, A_HEADS * A_V)
            wukv = jnp.concatenate([wk, wv], axis=1).astype(BF16)
            bias, bias_pairs = _band_bias(band_rel_bias[i])
            wo = even_w_out[i].astype(BF16)
        else:
            win = odd_w_in[i].astype(BF16)
            wo = odd_w_out[i].astype(BF16)
        for prompt in (True, False):
            x = _ffn(xs[prompt], g[0:2], wg, wu, wd, l, 0)
            if l % 2 == 0:
                qa, ka, va, qb, kb, vb, *even_st[prompt] = _even_proj(
                    x, g[2:3], win, mla_q_norm[i][None], mla_kv_norm[i][None], wuq, wukv, mla_tabs[prompt],
                    prompt, batch, seq, i, n_even, even_st[prompt])
                if prompt:
                    mixes = [_mla_prompt(qa, ka, va, batch, seq), _band_prompt(qb, kb, vb, bias_pairs, batch, seq)]
                else:
                    mixes = _even_sample(i, qa, ka, va, qb, kb, vb, cache_mla_ckv, c_kpe_t, c_bk_t, c_bv_t,
                                         wukv, pick, bias, dec_batch, t_new)
            else:
                q, k, v, *odd_st[prompt] = _odd_proj(x, g[2:3], win, swa_tabs[prompt], prompt, batch, seq,
                                                     i, n_odd, odd_st[prompt])
                if prompt:
                    mixes = [_swa_prompt(swa_sinks, i, q, k, v, batch, seq)]
                else:
                    mixes = [_odd_sample(swa_sinks, i, q, k, v, c_sk, c_sv, dec_batch, t_new)]
            xs[prompt] = _mix_ffn(x, mixes, wo, g[3:4], g[4:6], wg, wu, wd, l, 1)

    y_prompt = xs[True].reshape(batch, seq, D_MODEL)
    y_sample = xs[False].reshape(dec_batch, t_new, D_MODEL)

    def states(prompt, lead):
        ckv, kpe, kbf, vbf = even_st[prompt]
        kf, vf = odd_st[prompt]
        if prompt:
            kbf, vbf = (a.reshape(n_even, lead[0], B_HEADS, HEAD_DIM, -1).transpose(0, 1, 4, 2, 3) for a in (kbf, vbf))
        return (ckv.reshape(n_even, *lead, A_KV_LORA), kpe.reshape(n_even, *lead, A_ROPE),
                kbf.reshape(n_even, lead[0], -1, B_HEADS, HEAD_DIM), vbf.reshape(n_even, lead[0], -1, B_HEADS, HEAD_DIM),
                kf.reshape(n_odd, lead[0], -1, C_KV_HEADS, HEAD_DIM), vf.reshape(n_odd, lead[0], -1, C_KV_HEADS, HEAD_DIM))

    return (y_prompt, y_sample, *states(True, (batch, seq)), *states(False, (dec_batch, t_new)))
```
